```python
import math
import jax
import jax.numpy as jnp
from jax import lax
import numpy as np

D_MODEL = 2048
BATCH = 2
SEQ = 4096
DEPTH = 4

HEAD_DIM = 128
DN_HEADS = 6
DN_CONV = 4
DN_CHUNK = 64
POOL_WINDOWS = (2, 4, 8, 16)
POOL_GROUPS = 4
POOL_GROUP_DIM = 128
SWA_HEADS = 6
SWA_KV_HEADS = 2
SWA_WINDOW = 128
SWA_BLOCK = 128
ROPE_THETA = 10000.0
D_FF = 5632
FFN_CONV = 3
NORM_EPS = 1e-6

DN_W = DN_HEADS * HEAD_DIM
POOL_W = POOL_GROUPS * POOL_GROUP_DIM
SWA_W = SWA_HEADS * HEAD_DIM
SWA_KV_W = SWA_KV_HEADS * HEAD_DIM
MIX_W = DN_W + POOL_W + SWA_W
IN_SIZES = (3 * DN_W, DN_W, DN_HEADS, DN_HEADS, POOL_W, SWA_W, SWA_KV_W, SWA_KV_W)
IN_W = int(sum(IN_SIZES))
IN_OFFSETS = tuple(int(v) for v in np.cumsum(IN_SIZES)[:-1])

kernel_name = "hybrid_parallel_heads_deltanet_pool_swa"


def rms_norm(x, w):
    xf = x.astype(jnp.float32)
    y = xf * lax.rsqrt(jnp.mean(xf * xf, axis=-1, keepdims=True) + NORM_EPS)
    return (y * w.astype(jnp.float32)).astype(x.dtype)


def l2_norm(x):
    return x * lax.rsqrt(jnp.sum(x * x, axis=-1, keepdims=True) + NORM_EPS)


def causal_dwconv(x, w):
    K = w.shape[0]
    T = x.shape[1]
    xp = jnp.pad(x, ((0, 0), (K - 1, 0), (0, 0)))
    return sum(xp[:, k:k + T] * w[k] for k in range(K))


def rope(x, cos, sin):
    xf = x.astype(jnp.float32)
    x1, x2 = jnp.split(xf, 2, axis=-1)
    return jnp.concatenate([x1 * cos - x2 * sin, x2 * cos + x1 * sin], axis=-1).astype(x.dtype)


def unit_lower_inverse(lmat):
    C = lmat.shape[-1]
    eye = jnp.eye(C, dtype=jnp.float32)
    nil = -lmat
    inv = eye + nil
    powk = nil
    for _ in range(int(math.log2(C)) - 1):
        powk = jnp.matmul(powk, powk)
        inv = jnp.matmul(inv, eye + powk)
    return inv


def chunk_gated_delta_rule(q, k, v, g, beta):
    B, T, H, D = q.shape
    C = DN_CHUNK
    N = T // C
    q, k, v = (t.reshape(B, N, C, H, D) for t in (q, k, v))
    g = g.reshape(B, N, C, H)
    beta = beta.reshape(B, N, C, H)
    gc = jnp.cumsum(g, axis=2)
    gct = jnp.moveaxis(gc, 2, 3)
    idx = jnp.arange(C)
    causal = idx[:, None] >= idx[None, :]
    strict = idx[:, None] > idx[None, :]
    decay = jnp.exp(jnp.where(causal, gct[..., :, None] - gct[..., None, :], -jnp.inf))
    kb = k * beta[..., None]
    vb = v * beta[..., None]
    lmat = jnp.where(strict, jnp.einsum('bnihd,bnjhd->bnhij', kb, k) * decay, 0.0)
    tinv = unit_lower_inverse(lmat)
    eg = jnp.exp(gc)[..., None]
    u = jnp.einsum('bnhij,bnjhd->bnihd', tinv, vb)
    w = jnp.einsum('bnhij,bnjhd->bnihd', tinv, kb * eg)
    a_intra = jnp.einsum('bnihd,bnjhd->bnhij', q, k) * decay
    g_last = gc[:, :, -1:, :]
    q_dec = q * eg
    k_dec = k * jnp.exp(g_last - gc)[..., None]
    cdec = jnp.exp(g_last[:, :, 0, :])

    def step(S, xs):
        u_n, w_n, q_n, k_n, a_n, c_n = xs
        v_new = u_n - jnp.einsum('bchk,bhkv->bchv', w_n, S)
        o_n = jnp.einsum('bchk,bhkv->bchv', q_n, S) + jnp.einsum('bhij,bjhv->bihv', a_n, v_new)
        S = S * c_n[:, :, None, None] + jnp.einsum('bchk,bchv->bhkv', k_n, v_new)
        return S, o_n

    xs = tuple(jnp.moveaxis(t, 1, 0) for t in (u, w, q_dec, k_dec, a_intra, cdec))
    S0 = jnp.zeros((B, H, D, D), jnp.float32)
    _, o = lax.scan(step, S0, xs)
    return jnp.moveaxis(o, 0, 1).reshape(B, T, H, D)


def gated_deltanet(p_qkv, p_z, p_b, p_a, conv_w, a_log, dt_bias, norm_w):
    B, T, _ = p_qkv.shape
    f32 = jnp.float32
    qkv = jax.nn.silu(causal_dwconv(p_qkv, conv_w)).astype(f32)
    q, k, v = jnp.split(qkv, 3, axis=-1)
    shp = (B, T, DN_HEADS, HEAD_DIM)
    q = l2_norm(q.reshape(shp)) * (HEAD_DIM ** -0.5)
    k = l2_norm(k.reshape(shp))
    v = v.reshape(shp)
    beta = jax.nn.sigmoid(p_b.astype(f32))
    g = -jnp.exp(a_log.astype(f32)) * jax.nn.softplus(p_a.astype(f32) + dt_bias.astype(f32))
    o = chunk_gated_delta_rule(q, k, v, g, beta)
    o = rms_norm(o, norm_w) * jax.nn.silu(p_z.astype(f32).reshape(shp))
    return o.reshape(B, T, DN_W).astype(p_qkv.dtype)


def multiscale_pool(p_pool, pool_w, pool_scale):
    B, T, _ = p_pool.shape
    xg = p_pool.astype(jnp.float32).reshape(B, T, POOL_GROUPS, POOL_GROUP_DIM)
    cs = jnp.cumsum(xg, axis=1)
    t = jnp.arange(T)
    outs = []
    for gi, win in enumerate(POOL_WINDOWS):
        c = cs[:, :, gi]
        prev = jnp.pad(c, ((0, 0), (win, 0), (0, 0)))[:, :T]
        cnt = jnp.minimum(t + 1, win).astype(jnp.float32)
        outs.append((c - prev) / cnt[None, :, None] - xg[:, :, gi])
    y = jnp.stack(outs, axis=2)
    y = jnp.einsum('btgc,gcd->btgd', y, pool_w.astype(jnp.float32))
    return (y.reshape(B, T, POOL_W) * pool_scale.astype(jnp.float32)).astype(p_pool.dtype)


def band_blocks(t, blk):
    B, T = t.shape[:2]
    nb = T // blk
    tp = jnp.pad(t, ((0, 0), (blk, 0), (0, 0), (0, 0))).reshape(B, nb + 1, blk, *t.shape[2:])
    return jnp.concatenate([tp[:, :-1], tp[:, 1:]], axis=2)


def band_mask(nb, blk):
    ii = jnp.arange(blk)[:, None]
    jj = jnp.arange(2 * blk)[None, :]
    diff = blk + ii - jj
    in_win = (diff >= 0) & (diff < SWA_WINDOW)
    blk_idx = jnp.arange(nb)[:, None, None]
    return in_win[None] & ((blk_idx > 0) | (jj >= blk)[None])


def swa_sink_attention(p_q, p_k, p_v, sinks, cos, sin):
    B, T, _ = p_q.shape
    G = SWA_HEADS // SWA_KV_HEADS
    bq = SWA_BLOCK
    nb = T // bq
    q = rope(p_q.reshape(B, T, SWA_HEADS, HEAD_DIM), cos, sin)
    k = rope(p_k.reshape(B, T, SWA_KV_HEADS, HEAD_DIM), cos, sin)
    v = p_v.reshape(B, T, SWA_KV_HEADS, HEAD_DIM)
    qb = q.reshape(B, nb, bq, SWA_KV_HEADS, G, HEAD_DIM)
    kb = band_blocks(k, bq)
    vb = band_blocks(v, bq)
    s = jnp.einsum('bnihgd,bnjhd->bnhgij', qb, kb, preferred_element_type=jnp.float32) * (HEAD_DIM ** -0.5)
    s = jnp.where(band_mask(nb, bq)[None, :, None, None], s, -jnp.inf)
    sink = jnp.broadcast_to(
        sinks.astype(jnp.float32).reshape(SWA_KV_HEADS, G)[None, None, :, :, None, None],
        s.shape[:-1] + (1,))
    p = jax.nn.softmax(jnp.concatenate([s, sink], axis=-1), axis=-1)[..., :-1]
    o = jnp.einsum('bnhgij,bnjhd->bnihgd', p.astype(v.dtype), vb)
    return o.reshape(B, T, SWA_W)


def conv_glu_ffn(h, w_up, conv_w, conv_b, w_down):
    u = causal_dwconv(h @ w_up, conv_w) + conv_b
    a, b = jnp.split(u, 2, axis=-1)
    return (jax.nn.silu(a) * b) @ w_down


def setup_inputs(seed: int = 0) -> dict:
    key = jax.random.key(seed)
    ks = jax.random.split(key, 24)
    f32 = jnp.float32

    def nrm(k, shape, scale):
        return jax.random.normal(k, shape, f32) * scale

    def gain(k, shape):
        return 1.0 + 0.02 * jax.random.normal(k, shape, f32)

    x = jax.random.normal(ks[0], (BATCH, SEQ, D_MODEL), f32)
    offs = jax.random.randint(ks[1], (BATCH, 1), 0, 1024)
    positions = (offs + jnp.arange(SEQ)[None, :]).astype(jnp.int32)
    dt = jnp.exp(jax.random.uniform(ks[2], (DEPTH, DN_HEADS), f32, math.log(1e-3), math.log(1e-1)))
    return {
        "x": x,
        "positions": positions,
        "norm_mix_pre": gain(ks[3], (DEPTH, D_MODEL)),
        "w_in": nrm(ks[4], (DEPTH, D_MODEL, IN_W), D_MODEL ** -0.5),
        "dn_conv_w": nrm(ks[5], (DEPTH, DN_CONV, 3 * DN_W), DN_CONV ** -0.5),
        "dn_a_log": jnp.log(jax.random.uniform(ks[6], (DEPTH, DN_HEADS), f32, 1.0, 16.0)),
        "dn_dt_bias": dt + jnp.log(-jnp.expm1(-dt)),
        "dn_norm_w": gain(ks[7], (DEPTH, HEAD_DIM)),
        "pool_w": nrm(ks[8], (DEPTH, POOL_GROUPS, POOL_GROUP_DIM, POOL_GROUP_DIM), POOL_GROUP_DIM ** -0.5),
        "pool_scale": gain(ks[9], (DEPTH, POOL_W)),
        "swa_sinks": nrm(ks[10], (DEPTH, SWA_HEADS), 1.0),
        "w_out": nrm(ks[11], (DEPTH, MIX_W, D_MODEL), MIX_W ** -0.5),
        "norm_mix_post": gain(ks[12], (DEPTH, D_MODEL)),
        "norm_ffn_pre": gain(ks[13], (DEPTH, D_MODEL)),
        "ffn_w_up": nrm(ks[14], (DEPTH, D_MODEL, 2 * D_FF), D_MODEL ** -0.5),
        "ffn_conv_w": nrm(ks[15], (DEPTH, FFN_CONV, 2 * D_FF), FFN_CONV ** -0.5),
        "ffn_conv_b": nrm(ks[16], (DEPTH, 2 * D_FF), 0.01),
        "ffn_w_down": nrm(ks[17], (DEPTH, D_FF, D_MODEL), D_FF ** -0.5),
        "norm_ffn_post": gain(ks[18], (DEPTH, D_MODEL)),
    }


def reference(x, positions, norm_mix_pre, w_in, dn_conv_w, dn_a_log, dn_dt_bias, dn_norm_w,
              pool_w, pool_scale, swa_sinks, w_out, norm_mix_post, norm_ffn_pre,
              ffn_w_up, ffn_conv_w, ffn_conv_b, ffn_w_down, norm_ffn_post):
    inv_freq = 1.0 / (ROPE_THETA ** (jnp.arange(0, HEAD_DIM, 2, dtype=jnp.float32) / HEAD_DIM))
    ang = positions.astype(jnp.float32)[..., None] * inv_freq
    cos = jnp.cos(ang)[:, :, None, :]
    sin = jnp.sin(ang)[:, :, None, :]
    for l in range(DEPTH):
        h = rms_norm(x, norm_mix_pre[l])
        p = h @ w_in[l]
        dn_qkv, dn_z, dn_b, dn_a, p_pool, sq, sk, sv = jnp.split(p, IN_OFFSETS, axis=-1)
        y_dn = gated_deltanet(dn_qkv, dn_z, dn_b, dn_a, dn_conv_w[l], dn_a_log[l], dn_dt_bias[l], dn_norm_w[l])
        y_pool = multiscale_pool(p_pool, pool_w[l], pool_scale[l])
        y_swa = swa_sink_attention(sq, sk, sv, swa_sinks[l], cos, sin)
        mix = jnp.concatenate([y_dn, y_pool, y_swa], axis=-1) @ w_out[l]
        x = x + rms_norm(mix, norm_mix_post[l])
        h = rms_norm(x, norm_ffn_pre[l])
        f = conv_glu_ffn(h, ffn_w_up[l], ffn_conv_w[l], ffn_conv_b[l], ffn_w_down[l])
        x = x + rms_norm(f, norm_ffn_post[l])
    return x
```

```python
import functools
import math

import numpy as np
import jax
import jax.numpy as jnp
from jax import lax
from jax.experimental import pallas as pl
from jax.experimental.pallas import tpu as pltpu

HEAD_DIM = 128
DN_HEADS = 6
DN_CONV = 4
DN_CHUNK = 64
POOL_WINDOWS = (2, 4, 8, 16)
POOL_GROUPS = 4
POOL_GROUP_DIM = 128
SWA_HEADS = 6
SWA_KV_HEADS = 2
SWA_WINDOW = 128
SWA_BLOCK = 128
ROPE_THETA = 10000.0
FFN_CONV = 3
NORM_EPS = 1e-6

DN_W = DN_HEADS * HEAD_DIM
POOL_W = POOL_GROUPS * POOL_GROUP_DIM
SWA_W = SWA_HEADS * HEAD_DIM
SWA_KV_W = SWA_KV_HEADS * HEAD_DIM
LANES = 128
SUBLANES = 8

SWA_GROUP_W = SWA_W // SWA_KV_HEADS
P_POOL = 0
P_SK = P_POOL + POOL_W
P_SV = P_SK + SWA_KV_W
P_BA = P_SV + SWA_KV_W
P_QKV = P_BA + LANES
P_Z = P_QKV + 3 * DN_W
P_SQ = P_Z + DN_W
P_W = P_SQ + SWA_W
assert P_POOL % POOL_W == 0 and P_SK % SWA_KV_W == 0 and P_SV % SWA_KV_W == 0 and P_SQ % SWA_GROUP_W == 0
assert P_BA % LANES == 0 and P_QKV % LANES == 0 and P_Z % LANES == 0

VMEM_LIMIT = 56 * 1024 * 1024

F32 = jnp.float32
BF16 = jnp.bfloat16


def _cparams(sem):
    return pltpu.CompilerParams(dimension_semantics=sem, vmem_limit_bytes=VMEM_LIMIT)


def _rms(v, gain):
    return v * lax.rsqrt(jnp.mean(v * v, axis=-1, keepdims=True) + NORM_EPS) * gain


def _sigmoid(v):
    return 1.0 / (1.0 + jnp.exp(-v))


def _dot(a, b):
    return jnp.dot(a.astype(BF16), b.astype(BF16), preferred_element_type=F32)


def _dot_nt(a, b):
    return lax.dot_general(a.astype(BF16), b.astype(BF16), (((1,), (1,)), ((), ())),
                           preferred_element_type=F32)


def _dot_tn(a, b):
    return lax.dot_general(a.astype(BF16), b.astype(BF16), (((0,), (0,)), ((), ())),
                           preferred_element_type=F32)


def _prenorm_kernel(x_ref, g_ref, h_ref):
    h_ref[...] = _rms(x_ref[...], g_ref[...]).astype(h_ref.dtype)


def _prenorm(x2, gain, tm):
    M, D = x2.shape
    return pl.pallas_call(
        _prenorm_kernel,
        grid=(M // tm,),
        in_specs=[pl.BlockSpec((tm, D), lambda i: (i, 0)), pl.BlockSpec((1, D), lambda i: (0, 0))],
        out_specs=pl.BlockSpec((tm, D), lambda i: (i, 0)),
        out_shape=jax.ShapeDtypeStruct((M, D), BF16),
        compiler_params=_cparams(("parallel",)),
        name="prenorm",
    )(x2, gain)


def _rope_kernel(pos_ref, freq_ref, cos_ref, sin_ref):
    ang = pos_ref[...] * freq_ref[...]
    lane = lax.broadcasted_iota(jnp.int32, ang.shape, 1)
    cos_ref[...] = jnp.cos(ang)
    sin_ref[...] = jnp.where(lane < HEAD_DIM // 2, -jnp.sin(ang), jnp.sin(ang))


def _rope_tables(positions, tm):
    M = positions.size
    pos = positions.astype(F32).reshape(M, 1)
    inv = (1.0 / (ROPE_THETA ** (np.arange(0, HEAD_DIM, 2, dtype=np.float32) / HEAD_DIM))).astype(np.float32)
    freq = jnp.asarray(np.concatenate([inv, inv])[None, :])
    return pl.pallas_call(
        _rope_kernel,
        grid=(M // tm,),
        in_specs=[pl.BlockSpec((tm, 1), lambda i: (i, 0)), pl.BlockSpec((1, HEAD_DIM), lambda i: (0, 0))],
        out_specs=[pl.BlockSpec((tm, HEAD_DIM), lambda i: (i, 0))] * 2,
        out_shape=[jax.ShapeDtypeStruct((M, HEAD_DIM), F32)] * 2,
        compiler_params=_cparams(("parallel",)),
        name="rope_tables",
    )(pos, freq)


def _mm_kernel(a_ref, w_ref, o_ref):
    o_ref[...] = jnp.dot(a_ref[...], w_ref[...], preferred_element_type=F32)


def _in_proj(h, w, tm, tn):
    M, K = h.shape
    N = w.shape[1]
    return pl.pallas_call(
        _mm_kernel,
        grid=(N // tn, M // tm),
        in_specs=[pl.BlockSpec((tm, K), lambda j, i: (i, 0)), pl.BlockSpec((K, tn), lambda j, i: (0, j))],
        out_specs=pl.BlockSpec((tm, tn), lambda j, i: (i, j)),
        out_shape=jax.ShapeDtypeStruct((M, N), F32),
        compiler_params=_cparams(("parallel", "parallel")),
        name="in_proj",
    )(h, w)


def _gate_kernel(ba_ref, alog_ref, dt_ref, beta_ref, gc_ref):
    p = ba_ref[...]
    T = p.shape[0]
    beta_ref[...] = _sigmoid(p)
    sp = p + dt_ref[...]
    softplus = jnp.maximum(sp, 0.0) + jnp.log(1.0 + jnp.exp(-jnp.abs(sp)))
    g = -jnp.exp(alog_ref[...]) * softplus
    row = lax.broadcasted_iota(jnp.int32, g.shape, 0) % DN_CHUNK
    s = 1
    while s < DN_CHUNK:
        g = g + jnp.where(row >= s, pltpu.roll(g, s, axis=0), 0.0)
        s *= 2
    gc_ref[...] = g


def _gates(p, alog_row, dt_row, B, T):
    M = p.shape[0]
    cb = P_BA // LANES
    return pl.pallas_call(
        _gate_kernel,
        grid=(B,),
        in_specs=[pl.BlockSpec((T, LANES), lambda b: (b, cb)),
                  pl.BlockSpec((1, LANES), lambda b: (0, 0)),
                  pl.BlockSpec((1, LANES), lambda b: (0, 0))],
        out_specs=[pl.BlockSpec((T, LANES), lambda b: (b, 0))] * 2,
        out_shape=[jax.ShapeDtypeStruct((M, LANES), F32)] * 2,
        compiler_params=_cparams(("parallel",)),
        name="dn_gates",
    )(p, alog_row, dt_row)


def _dn_kernel(q_ref, k_ref, v_ref, z_ref, beta_ref, gc_ref, gcrow_ref, cwq_ref, cwk_ref, cwv_ref, nw_ref,
               o_ref, qs, ks, vs):
    h = pl.program_id(1)
    T = q_ref.shape[0]
    C = DN_CHUNK
    D = HEAD_DIM
    pad = SUBLANES
    for src, dst in ((q_ref, qs), (k_ref, ks), (v_ref, vs)):
        dst[0:pad, :] = jnp.zeros((pad, D), F32)
        dst[pad:pad + T, :] = src[...]

    ii = lax.broadcasted_iota(jnp.int32, (C, C), 0)
    jj = lax.broadcasted_iota(jnp.int32, (C, C), 1)
    eye = (ii == jj).astype(F32)
    lane = lax.broadcasted_iota(jnp.int32, (C, LANES), 1)
    nw = nw_ref[...]

    def conv_silu(scr, cw_ref, r0):
        win = scr[pl.ds(r0, C + pad), :]
        acc = win[pad:pad + C] * cw_ref[DN_CONV - 1:DN_CONV, :]
        for s in range(1, DN_CONV):
            acc = acc + win[pad - s:pad - s + C] * cw_ref[DN_CONV - 1 - s:DN_CONV - s, :]
        return acc * _sigmoid(acc)

    def body(n, S):
        r0 = pl.multiple_of(n * C, C)
        q = conv_silu(qs, cwq_ref, r0)
        k = conv_silu(ks, cwk_ref, r0)
        v = conv_silu(vs, cwv_ref, r0)
        q = q * lax.rsqrt(jnp.sum(q * q, axis=-1, keepdims=True) + NORM_EPS) * (D ** -0.5)
        k = k * lax.rsqrt(jnp.sum(k * k, axis=-1, keepdims=True) + NORM_EPS)
        beta = jnp.sum(jnp.where(lane == h, beta_ref[pl.ds(r0, C), :], 0.0), axis=-1, keepdims=True)
        gcc = jnp.sum(jnp.where(lane == h + DN_HEADS, gc_ref[pl.ds(r0, C), :], 0.0), axis=-1, keepdims=True)
        gcr = gcrow_ref[n]
        decay = jnp.exp(jnp.where(ii >= jj, gcc - gcr, -jnp.inf))
        kb = k * beta
        vb = v * beta
        lmat = jnp.where(ii > jj, _dot_nt(kb, k) * decay, 0.0)
        nil = -lmat
        inv = eye + nil
        powk = nil
        for _ in range(int(math.log2(C)) - 1):
            powk = _dot(powk, powk)
            inv = _dot(inv, eye + powk)
        eg = jnp.exp(gcc)
        u = _dot(inv, vb)
        w = _dot(inv, kb * eg)
        a_intra = _dot_nt(q, k) * decay
        glast = gcr[:, C - 1:C]
        q_dec = q * eg
        k_dec = k * jnp.exp(glast - gcc)
        v_new = u - _dot(w, S)
        o = _dot(q_dec, S) + _dot(a_intra, v_new)
        S = S * jnp.exp(glast) + _dot_tn(k_dec, v_new)
        zz = z_ref[pl.ds(r0, C), :]
        o_ref[pl.ds(r0, C), :] = (_rms(o, nw) * (zz * _sigmoid(zz))).astype(o_ref.dtype)
        return S

    lax.fori_loop(0, T // C, body, jnp.zeros((D, D), F32))


def _deltanet(p, beta, gc, gcrow, conv_w, norm_w, B, T):
    M = p.shape[0]
    H = DN_HEADS
    N = T // DN_CHUNK
    qb, kb, vb, zb = P_QKV // LANES, (P_QKV + DN_W) // LANES, (P_QKV + 2 * DN_W) // LANES, P_Z // LANES
    tok = lambda off: pl.BlockSpec((T, HEAD_DIM), lambda b, h: (b, off + h))
    cw = lambda off: pl.BlockSpec((DN_CONV, HEAD_DIM), lambda b, h: (0, off + h))
    return pl.pallas_call(
        _dn_kernel,
        grid=(B, H),
        in_specs=[tok(qb), tok(kb), tok(vb), tok(zb),
                  pl.BlockSpec((T, LANES), lambda b, h: (b, 0)),
                  pl.BlockSpec((T, LANES), lambda b, h: (b, 0)),
                  pl.BlockSpec((None, None, N, 1, DN_CHUNK), lambda b, h: (b, h, 0, 0, 0)),
                  cw(0), cw(H), cw(2 * H),
                  pl.BlockSpec((1, HEAD_DIM), lambda b, h: (0, 0))],
        out_specs=pl.BlockSpec((T, HEAD_DIM), lambda b, h: (b, h)),
        out_shape=jax.ShapeDtypeStruct((M, DN_W), BF16),
        scratch_shapes=[pltpu.VMEM((T + SUBLANES, HEAD_DIM), F32)] * 3,
        compiler_params=_cparams(("parallel", "parallel")),
        name="deltanet",
    )(p, p, p, p, beta, gc, gcrow, conv_w, conv_w, conv_w, norm_w)


POOL_HALO = 16


def _pool_kernel(x_ref, halo_ref, w_ref, scale_ref, o_ref, *, tiles_per_seq):
    i = pl.program_id(0)
    tm = x_ref.shape[0]
    t0 = (i % tiles_per_seq) * tm
    first = (i % tiles_per_seq) == 0
    pos = t0 + 1 + lax.broadcasted_iota(jnp.int32, (tm, POOL_GROUP_DIM), 0)
    for gi, win in enumerate(POOL_WINDOWS):
        cols = slice(gi * POOL_GROUP_DIM, (gi + 1) * POOL_GROUP_DIM)
        halo = jnp.where(first, 0.0, halo_ref[:, cols])
        xg = x_ref[:, cols]
        ext = jnp.concatenate([halo, xg], axis=0)
        acc = ext
        s = 1
        while s < win:
            acc = acc + pltpu.roll(acc, s, axis=0)
            s *= 2
        cnt = jnp.minimum(pos, win).astype(F32)
        y = acc[POOL_HALO:] / cnt - xg
        y = _dot(y, w_ref[gi])
        o_ref[:, cols] = (y * scale_ref[:, cols]).astype(o_ref.dtype)


def _pool(p, pool_w, pool_scale, T, tm):
    M = p.shape[0]
    cb = P_POOL // POOL_W
    hb = tm // POOL_HALO
    return pl.pallas_call(
        functools.partial(_pool_kernel, tiles_per_seq=T // tm),
        grid=(M // tm,),
        in_specs=[pl.BlockSpec((tm, POOL_W), lambda i: (i, cb)),
                  pl.BlockSpec((POOL_HALO, POOL_W), lambda i: (jnp.maximum(i * hb - 1, 0), cb)),
                  pl.BlockSpec((POOL_GROUPS, POOL_GROUP_DIM, POOL_GROUP_DIM), lambda i: (0, 0, 0)),
                  pl.BlockSpec((1, POOL_W), lambda i: (0, 0))],
        out_specs=pl.BlockSpec((tm, POOL_W), lambda i: (i, 0)),
        out_shape=jax.ShapeDtypeStruct((M, POOL_W), BF16),
        compiler_params=_cparams(("parallel",)),
        name="pool",
    )(p, p, pool_w, pool_scale)


def _swa_kernel(q_ref, k_ref, kp_ref, v_ref, vp_ref, cos_ref, sin_ref, cosp_ref, sinp_ref, sink_ref, o_ref,
                *, tiles_per_seq):
    i = pl.program_id(1)
    first = (i % tiles_per_seq) == 0
    tm = q_ref.shape[0]
    blk = SWA_BLOCK
    G = SWA_HEADS // SWA_KV_HEADS
    D = HEAD_DIM

    def rope(x, cos, sin):
        return x * cos + pltpu.roll(x, D // 2, axis=1) * sin

    cos = cos_ref[...]
    sin = sin_ref[...]
    cosx = jnp.concatenate([cosp_ref[...], cos], axis=0)
    sinx = jnp.concatenate([sinp_ref[...], sin], axis=0)
    kx = rope(jnp.concatenate([kp_ref[...], k_ref[...]], axis=0), cosx, sinx).astype(BF16)
    vx = jnp.concatenate([vp_ref[...], v_ref[...]], axis=0).astype(BF16)
    scale = D ** -0.5
    qh = [(rope(q_ref[:, g * D:(g + 1) * D], cos, sin) * scale).astype(BF16) for g in range(G)]

    ri = lax.broadcasted_iota(jnp.int32, (G * blk, 2 * blk), 0)
    ii = ri % blk
    jj = lax.broadcasted_iota(jnp.int32, (G * blk, 2 * blk), 1)
    band = jnp.logical_or(jnp.logical_and(jj < blk, jj > ii), jnp.logical_and(jj >= blk, jj - blk <= ii))
    no_prev = jnp.where(first, blk, 0)
    band_first = jnp.logical_and(band, jj >= no_prev)
    sink = sink_ref[:, 0:1]
    for g in range(1, G):
        sink = jnp.where(ri[:, 0:1] >= g * blk, sink_ref[:, g:g + 1], sink)
    for b in range(tm // blk):
        qs = jnp.concatenate([q[b * blk:(b + 1) * blk] for q in qh], axis=0)
        keys = kx[b * blk:(b + 2) * blk]
        s = lax.dot_general(qs, keys, (((1,), (1,)), ((), ())), preferred_element_type=F32)
        s = jnp.where(band_first if b == 0 else band, s, -jnp.inf)
        m = jnp.maximum(jnp.max(s, axis=-1, keepdims=True), sink)
        e = jnp.exp(s - m)
        den = jnp.sum(e, axis=-1, keepdims=True) + jnp.exp(sink - m)
        o = jnp.dot(e.astype(BF16), vx[b * blk:(b + 2) * blk], preferred_element_type=F32) / den
        for g in range(G):
            o_ref[b * blk:(b + 1) * blk, g * D:(g + 1) * D] = o[g * blk:(g + 1) * blk].astype(o_ref.dtype)


def _swa(p, cos, sin, sinks, T, tm):
    M = p.shape[0]
    blk = SWA_BLOCK
    r = tm // blk
    G = SWA_HEADS // SWA_KV_HEADS
    qb, kb, vb = P_SQ // SWA_GROUP_W, P_SK // HEAD_DIM, P_SV // HEAD_DIM
    prev = lambda i: jnp.maximum(i * r - 1, 0)
    sink_rows = jnp.zeros((SWA_KV_HEADS, 1, LANES), F32).at[:, 0, :G].set(sinks.astype(F32).reshape(SWA_KV_HEADS, G))
    return pl.pallas_call(
        functools.partial(_swa_kernel, tiles_per_seq=T // tm),
        grid=(SWA_KV_HEADS, M // tm),
        in_specs=[pl.BlockSpec((tm, SWA_GROUP_W), lambda kv, i: (i, qb + kv)),
                  pl.BlockSpec((tm, HEAD_DIM), lambda kv, i: (i, kb + kv)),
                  pl.BlockSpec((blk, HEAD_DIM), lambda kv, i: (prev(i), kb + kv)),
                  pl.BlockSpec((tm, HEAD_DIM), lambda kv, i: (i, vb + kv)),
                  pl.BlockSpec((blk, HEAD_DIM), lambda kv, i: (prev(i), vb + kv)),
                  pl.BlockSpec((tm, HEAD_DIM), lambda kv, i: (i, 0)),
                  pl.BlockSpec((tm, HEAD_DIM), lambda kv, i: (i, 0)),
                  pl.BlockSpec((blk, HEAD_DIM), lambda kv, i: (prev(i), 0)),
                  pl.BlockSpec((blk, HEAD_DIM), lambda kv, i: (prev(i), 0)),
                  pl.BlockSpec((None, 1, LANES), lambda kv, i: (kv, 0, 0))],
        out_specs=pl.BlockSpec((tm, SWA_GROUP_W), lambda kv, i: (i, kv)),
        out_shape=jax.ShapeDtypeStruct((M, SWA_W), BF16),
        compiler_params=_cparams(("parallel", "parallel")),
        name="swa",
    )(p, p, p, p, p, cos, sin, cos, sin, sink_rows)


def _outproj_kernel(ydn_ref, ypool_ref, yswa_ref, w1_ref, w2_ref, w3_ref, x_ref, gpost_ref, gffn_ref,
                    xo_ref, h_ref):
    mix = jnp.dot(ydn_ref[...], w1_ref[...], preferred_element_type=F32)
    mix = mix + jnp.dot(ypool_ref[...], w2_ref[...], preferred_element_type=F32)
    mix = mix + jnp.dot(yswa_ref[...], w3_ref[...], preferred_element_type=F32)
    xn = x_ref[...] + _rms(mix, gpost_ref[...])
    xo_ref[...] = xn
    h_ref[...] = _rms(xn, gffn_ref[...]).astype(h_ref.dtype)


def _out_proj(ydn, ypool, yswa, w_dn, w_pool, w_swa, x2, gpost, gffn, tm):
    M, D = x2.shape
    row = lambda w: pl.BlockSpec((tm, w), lambda i: (i, 0))
    wsp = lambda r: pl.BlockSpec((r, D), lambda i: (0, 0))
    vec = pl.BlockSpec((1, D), lambda i: (0, 0))
    return pl.pallas_call(
        _outproj_kernel,
        grid=(M // tm,),
        in_specs=[row(DN_W), row(POOL_W), row(SWA_W), wsp(DN_W), wsp(POOL_W), wsp(SWA_W), row(D), vec, vec],
        out_specs=[row(D), row(D)],
        out_shape=[jax.ShapeDtypeStruct((M, D), F32), jax.ShapeDtypeStruct((M, D), BF16)],
        compiler_params=_cparams(("parallel",)),
        name="out_proj",
    )(ydn, ypool, yswa, w_dn, w_pool, w_swa, x2, gpost, gffn)


def _ffn_up_kernel(h_ref, wa_ref, wb_ref, cwa_ref, cwb_ref, cba_ref, cbb_ref, g_ref, ua, ub, *, tiles_per_seq):
    i = pl.program_id(1)
    tm = h_ref.shape[0]
    pad = SUBLANES

    @pl.when(i % tiles_per_seq == 0)
    def _():
        ua[0:pad, :] = jnp.zeros((pad, ua.shape[1]), F32)
        ub[0:pad, :] = jnp.zeros((pad, ub.shape[1]), F32)

    hb = h_ref[...]

    def branch(w_ref, scr, cw_ref, cb_ref):
        scr[pad:pad + tm, :] = jnp.dot(hb, w_ref[...], preferred_element_type=F32)
        acc = cb_ref[...] + scr[pad:pad + tm, :] * cw_ref[FFN_CONV - 1:FFN_CONV, :]
        for s in range(1, FFN_CONV):
            acc = acc + scr[pad - s:pad - s + tm, :] * cw_ref[FFN_CONV - 1 - s:FFN_CONV - s, :]
        scr[0:pad, :] = scr[tm:tm + pad, :]
        return acc

    a = branch(wa_ref, ua, cwa_ref, cba_ref)
    b = branch(wb_ref, ub, cwb_ref, cbb_ref)
    g_ref[...] = (a * _sigmoid(a) * b).astype(g_ref.dtype)


def _ffn_up(h, w_up, conv_w, conv_b, T, tm, tn):
    M, D = h.shape
    F = w_up.shape[1] // 2
    nj = F // tn
    return pl.pallas_call(
        functools.partial(_ffn_up_kernel, tiles_per_seq=T // tm),
        grid=(nj, M // tm),
        in_specs=[pl.BlockSpec((tm, D), lambda j, i: (i, 0)),
                  pl.BlockSpec((D, tn), lambda j, i: (0, j)),
                  pl.BlockSpec((D, tn), lambda j, i: (0, nj + j)),
                  pl.BlockSpec((FFN_CONV, tn), lambda j, i: (0, j)),
                  pl.BlockSpec((FFN_CONV, tn), lambda j, i: (0, nj + j)),
                  pl.BlockSpec((1, tn), lambda j, i: (0, j)),
                  pl.BlockSpec((1, tn), lambda j, i: (0, nj + j))],
        out_specs=pl.BlockSpec((tm, tn), lambda j, i: (i, j)),
        out_shape=jax.ShapeDtypeStruct((M, F), BF16),
        scratch_shapes=[pltpu.VMEM((tm + SUBLANES, tn), F32)] * 2,
        compiler_params=_cparams(("parallel", "arbitrary")),
        name="ffn_up",
    )(h, w_up, w_up, conv_w, conv_w, conv_b, conv_b)


def _ffn_down_kernel(g_ref, w_ref, x_ref, gpost_ref, gnext_ref, xo_ref, h_ref, acc):
    k = pl.program_id(1)

    @pl.when(k == 0)
    def _():
        acc[...] = jnp.zeros_like(acc)

    acc[...] += jnp.dot(g_ref[...], w_ref[...], preferred_element_type=F32)

    @pl.when(k == pl.num_programs(1) - 1)
    def _():
        xn = x_ref[...] + _rms(acc[...], gpost_ref[...])
        xo_ref[...] = xn
        h_ref[...] = _rms(xn, gnext_ref[...]).astype(h_ref.dtype)


def _ffn_down(g, w_down, x2, gpost, gnext, tm, tk):
    M, D = x2.shape
    F = g.shape[1]
    vec = pl.BlockSpec((1, D), lambda i, k: (0, 0))
    return pl.pallas_call(
        _ffn_down_kernel,
        grid=(M // tm, F // tk),
        in_specs=[pl.BlockSpec((tm, tk), lambda i, k: (i, k)),
                  pl.BlockSpec((tk, D), lambda i, k: (k, 0)),
                  pl.BlockSpec((tm, D), lambda i, k: (i, 0)), vec, vec],
        out_specs=[pl.BlockSpec((tm, D), lambda i, k: (i, 0))] * 2,
        out_shape=[jax.ShapeDtypeStruct((M, D), F32), jax.ShapeDtypeStruct((M, D), BF16)],
        scratch_shapes=[pltpu.VMEM((tm, D), F32)],
        compiler_params=_cparams(("parallel", "arbitrary")),
        name="ffn_down",
    )(g, w_down, x2, gpost, gnext)


def _reorder_w_in(w):
    o = np.cumsum([0, 3 * DN_W, DN_W, DN_HEADS, DN_HEADS, POOL_W, SWA_W, SWA_KV_W, SWA_KV_W])
    qkv, z, b, a, pool, sq, sk, sv = (w[:, o[i]:o[i + 1]] for i in range(8))
    padc = jnp.zeros((w.shape[0], LANES - 2 * DN_HEADS), w.dtype)
    return jnp.concatenate([pool, sk, sv, b, a, padc, qkv, z, sq], axis=1).astype(BF16)


def _lane_row(v, offset):
    return jnp.zeros((1, LANES), F32).at[0, offset:offset + v.shape[0]].set(v.astype(F32))


def _dn_layer(p, conv_w, a_log, dt_bias, norm_w, B, T):
    N = T // DN_CHUNK
    beta, gc = _gates(p, _lane_row(a_log, DN_HEADS), _lane_row(dt_bias, DN_HEADS), B, T)
    gcrow = gc[:, DN_HEADS:2 * DN_HEADS].reshape(B, N, DN_CHUNK, DN_HEADS).transpose(0, 3, 1, 2)
    gcrow = gcrow.reshape(B, DN_HEADS, N, 1, DN_CHUNK)
    return _deltanet(p, beta, gc, gcrow, conv_w, norm_w[None, :], B, T)


def kernel(x, positions, norm_mix_pre, w_in, dn_conv_w, dn_a_log, dn_dt_bias, dn_norm_w, pool_w, pool_scale,
           swa_sinks, w_out, norm_mix_post, norm_ffn_pre, ffn_w_up, ffn_conv_w, ffn_conv_b, ffn_w_down,
           norm_ffn_post):
    B, T, D = x.shape
    depth = w_in.shape[0]
    M = B * T
    N = T // DN_CHUNK
    tm = min(512, T)
    x2 = x.reshape(M, D).astype(F32)
    cos, sin = _rope_tables(positions, tm)
    h = _prenorm(x2, norm_mix_pre[0][None, :], tm)
    for l in range(depth):
        p = _in_proj(h, _reorder_w_in(w_in[l]), tm, P_W // 3)
        y_dn = _dn_layer(p, dn_conv_w[l], dn_a_log[l], dn_dt_bias[l], dn_norm_w[l], B, T)
        y_pool = _pool(p, pool_w[l].astype(BF16), pool_scale[l][None, :], T, tm)
        y_swa = _swa(p, cos, sin, swa_sinks[l], T, tm)
        wo = w_out[l].astype(BF16)
        x2, h = _out_proj(y_dn, y_pool, y_swa, wo[:DN_W], wo[DN_W:DN_W + POOL_W], wo[DN_W + POOL_W:], x2,
                          norm_mix_post[l][None, :], norm_ffn_pre[l][None, :], tm)
        g = _ffn_up(h, ffn_w_up[l].astype(BF16), ffn_conv_w[l], ffn_conv_b[l][None, :], T, tm, 512)
        gnext = norm_mix_pre[l + 1] if l + 1 < depth else jnp.ones((D,), F32)
        x2, h = _ffn_down(g, ffn_w_down[l].astype(BF16), x2, norm_ffn_post[l][None, :], gnext[None, :], tm, 512)
    return x2.reshape(B, T, D).astype(x.dtype)
```

```python
import functools
import math

import numpy as np
import jax
import jax.numpy as jnp
from jax import lax
from jax.experimental import pallas as pl
from jax.experimental.pallas import tpu as pltpu

HEAD_DIM = 128
DN_HEADS = 6
DN_CONV = 4
DN_CHUNK = 64
POOL_WINDOWS = (2, 4, 8, 16)
POOL_GROUPS = 4
POOL_GROUP_DIM = 128
SWA_HEADS = 6
SWA_KV_HEADS = 2
SWA_WINDOW = 128
SWA_BLOCK = 128
ROPE_THETA = 10000.0
FFN_CONV = 3
NORM_EPS = 1e-6

DN_W = DN_HEADS * HEAD_DIM
POOL_W = POOL_GROUPS * POOL_GROUP_DIM
SWA_W = SWA_HEADS * HEAD_DIM
SWA_KV_W = SWA_KV_HEADS * HEAD_DIM
SWA_GROUP_W = SWA_W // SWA_KV_HEADS
LANES = 128
SUBLANES = 8

R_QKV = 0
R_Z = R_QKV + 3 * DN_W
R_B = R_Z + DN_W
R_POOL = R_B + 2 * DN_HEADS
R_SQ = R_POOL + POOL_W
R_SK = R_SQ + SWA_W
R_SV = R_SK + SWA_KV_W
R_W = R_SV + SWA_KV_W
P_Z = 0
P_SQ = P_Z + DN_W
P_POOL = P_SQ + SWA_W
P_SK = P_POOL + POOL_W
P_SV = P_SK + SWA_KV_W
P_BA = P_SV + SWA_KV_W
P_QKV = P_BA + LANES
P_W = P_QKV + 3 * DN_W
assert P_Z % DN_W == 0 and P_SQ % SWA_GROUP_W == 0 and P_POOL % POOL_W == 0
assert P_SK % HEAD_DIM == 0 and P_SV % HEAD_DIM == 0 and P_BA % LANES == 0 and P_QKV % LANES == 0

VMEM_LIMIT = 56 * 1024 * 1024

F32 = jnp.float32
BF16 = jnp.bfloat16


def _cparams(sem):
    return pltpu.CompilerParams(dimension_semantics=sem, vmem_limit_bytes=VMEM_LIMIT)


def _rms(v, gain):
    return v * lax.rsqrt(jnp.mean(v * v, axis=-1, keepdims=True) + NORM_EPS) * gain


def _sigmoid(v):
    return 1.0 / (1.0 + jnp.exp(-v))


def _bdot(a, b):
    return jnp.einsum("gik,gkj->gij", a.astype(BF16), b.astype(BF16), preferred_element_type=F32)


def _bdot_nt(a, b):
    return jnp.einsum("gik,gjk->gij", a.astype(BF16), b.astype(BF16), preferred_element_type=F32)


def _cast_kernel(w_ref, o_ref):
    o_ref[...] = w_ref[...].astype(o_ref.dtype)


def _cast_bf16(w, tr):
    L, R, C = w.shape
    return pl.pallas_call(
        _cast_kernel,
        grid=(L, R // tr),
        in_specs=[pl.BlockSpec((None, tr, C), lambda l, i: (l, i, 0))],
        out_specs=pl.BlockSpec((None, tr, C), lambda l, i: (l, i, 0)),
        out_shape=jax.ShapeDtypeStruct((L, R, C), BF16),
        compiler_params=_cparams(("parallel", "parallel")),
        name="cast_bf16",
    )(w)


def _win_kernel(w_ref, o_ref):
    w = w_ref[...]
    sec = lambda start, width: w[:, start:start + width].astype(o_ref.dtype)
    o_ref[:, P_Z:P_Z + DN_W] = sec(R_Z, DN_W)
    o_ref[:, P_SQ:P_SQ + SWA_W] = sec(R_SQ, SWA_W)
    o_ref[:, P_POOL:P_POOL + POOL_W] = sec(R_POOL, POOL_W)
    o_ref[:, P_SK:P_SK + SWA_KV_W] = sec(R_SK, SWA_KV_W)
    o_ref[:, P_SV:P_SV + SWA_KV_W] = sec(R_SV, SWA_KV_W)
    gates = w[:, R_B:R_B + LANES]
    lane = lax.broadcasted_iota(jnp.int32, gates.shape, 1)
    o_ref[:, P_BA:P_BA + LANES] = jnp.where(lane < 2 * DN_HEADS, gates, 0.0).astype(o_ref.dtype)
    o_ref[:, P_QKV:P_QKV + 3 * DN_W] = sec(R_QKV, 3 * DN_W)


def _prep_w_in(w_in, tr):
    L, K, _ = w_in.shape
    return pl.pallas_call(
        _win_kernel,
        grid=(L, K // tr),
        in_specs=[pl.BlockSpec((None, tr, R_W), lambda l, i: (l, i, 0))],
        out_specs=pl.BlockSpec((None, tr, P_W), lambda l, i: (l, i, 0)),
        out_shape=jax.ShapeDtypeStruct((L, K, P_W), BF16),
        compiler_params=_cparams(("parallel", "parallel")),
        name="prep_w_in",
    )(w_in)


def _prenorm_kernel(x_ref, g_ref, h_ref):
    h_ref[...] = _rms(x_ref[...], g_ref[...]).astype(h_ref.dtype)


def _prenorm(x2, gains, tm):
    M, D = x2.shape
    return pl.pallas_call(
        _prenorm_kernel,
        grid=(M // tm,),
        in_specs=[pl.BlockSpec((tm, D), lambda i: (i, 0)), pl.BlockSpec((None, 1, D), lambda i: (0, 0, 0))],
        out_specs=pl.BlockSpec((tm, D), lambda i: (i, 0)),
        out_shape=jax.ShapeDtypeStruct((M, D), BF16),
        compiler_params=_cparams(("parallel",)),
        name="prenorm",
    )(x2, gains)


def _rope_kernel(pos_ref, freq_ref, cos_ref, sin_ref):
    ang = pos_ref[...] * freq_ref[...]
    lane = lax.broadcasted_iota(jnp.int32, ang.shape, 1)
    cos_ref[...] = jnp.cos(ang)
    sin_ref[...] = jnp.where(lane < HEAD_DIM // 2, -jnp.sin(ang), jnp.sin(ang))


def _rope_tables(positions, tm):
    M = positions.size
    pos = positions.astype(F32).reshape(M, 1)
    inv = (1.0 / (ROPE_THETA ** (np.arange(0, HEAD_DIM, 2, dtype=np.float32) / HEAD_DIM))).astype(np.float32)
    freq = jnp.asarray(np.concatenate([inv, inv])[None, :])
    return pl.pallas_call(
        _rope_kernel,
        grid=(M // tm,),
        in_specs=[pl.BlockSpec((tm, 1), lambda i: (i, 0)), pl.BlockSpec((1, HEAD_DIM), lambda i: (0, 0))],
        out_specs=[pl.BlockSpec((tm, HEAD_DIM), lambda i: (i, 0))] * 2,
        out_shape=[jax.ShapeDtypeStruct((M, HEAD_DIM), F32)] * 2,
        compiler_params=_cparams(("parallel",)),
        name="rope_tables",
    )(pos, freq)


def _mm_kernel(a_ref, w_ref, o_ref):
    o_ref[...] = jnp.dot(a_ref[...], w_ref[...], preferred_element_type=F32)


def _in_proj(h, w, l, tm, tn):
    M, K = h.shape
    N = w.shape[2]
    return pl.pallas_call(
        _mm_kernel,
        grid=(N // tn, M // tm),
        in_specs=[pl.BlockSpec((tm, K), lambda j, i: (i, 0)), pl.BlockSpec((None, K, tn), lambda j, i: (l, 0, j))],
        out_specs=pl.BlockSpec((tm, tn), lambda j, i: (i, j)),
        out_shape=jax.ShapeDtypeStruct((M, N), F32),
        compiler_params=_cparams(("parallel", "parallel")),
        name="in_proj",
    )(h, w)


def _gate_kernel(ba_ref, alog_ref, dt_ref, beta_ref, gc_ref):
    p = ba_ref[...]
    beta_ref[...] = _sigmoid(p)
    sp = p + dt_ref[...]
    softplus = jnp.maximum(sp, 0.0) + jnp.log(1.0 + jnp.exp(-jnp.abs(sp)))
    g = -jnp.exp(alog_ref[...]) * softplus
    row = lax.broadcasted_iota(jnp.int32, g.shape, 0) % DN_CHUNK
    s = 1
    while s < DN_CHUNK:
        g = g + jnp.where(row >= s, pltpu.roll(g, s, axis=0), 0.0)
        s *= 2
    gc_ref[...] = g


def _gates(p, alog_rows, dt_rows, l, B, T):
    M = p.shape[0]
    cb = P_BA // LANES
    vec = pl.BlockSpec((None, 1, LANES), lambda b: (l, 0, 0))
    return pl.pallas_call(
        _gate_kernel,
        grid=(B,),
        in_specs=[pl.BlockSpec((T, LANES), lambda b: (b, cb)), vec, vec],
        out_specs=[pl.BlockSpec((T, LANES), lambda b: (b, 0))] * 2,
        out_shape=[jax.ShapeDtypeStruct((M, LANES), F32)] * 2,
        compiler_params=_cparams(("parallel",)),
        name="dn_gates",
    )(p, alog_rows, dt_rows)


def _dn_prep_kernel(q_ref, k_ref, v_ref, qh_ref, kh_ref, vh_ref, beta_ref, gc_ref, gcrow_ref,
                    cwq_ref, cwk_ref, cwv_ref, wq_ref, u_ref, kd_ref, a_ref, c_ref, *, tiles_per_seq):
    h = pl.program_id(1)
    first = (pl.program_id(2) % tiles_per_seq) == 0
    tt = q_ref.shape[0]
    C = DN_CHUNK
    G = tt // C
    D = HEAD_DIM
    pad = SUBLANES

    def conv_silu(x_ref, halo_ref, cw_ref):
        halo = jnp.where(first, 0.0, halo_ref[...])
        ext = jnp.concatenate([halo, x_ref[...]], axis=0)
        acc = ext[pad:pad + tt] * cw_ref[DN_CONV - 1:DN_CONV, :]
        for s in range(1, DN_CONV):
            acc = acc + ext[pad - s:pad - s + tt] * cw_ref[DN_CONV - 1 - s:DN_CONV - s, :]
        return acc * _sigmoid(acc)

    q = conv_silu(q_ref, qh_ref, cwq_ref)
    k = conv_silu(k_ref, kh_ref, cwk_ref)
    v = conv_silu(v_ref, vh_ref, cwv_ref)
    q = q * lax.rsqrt(jnp.sum(q * q, axis=-1, keepdims=True) + NORM_EPS) * (D ** -0.5)
    k = k * lax.rsqrt(jnp.sum(k * k, axis=-1, keepdims=True) + NORM_EPS)
    lane = lax.broadcasted_iota(jnp.int32, (tt, LANES), 1)
    beta = jnp.sum(jnp.where(lane == h, beta_ref[...], 0.0), axis=-1, keepdims=True)
    gcc = jnp.sum(jnp.where(lane == h + DN_HEADS, gc_ref[...], 0.0), axis=-1, keepdims=True)
    eg = jnp.exp(gcc)
    kb = k * beta
    vb = v * beta

    q3 = q.reshape(G, C, D)
    k3 = k.reshape(G, C, D)
    kb3 = kb.reshape(G, C, D)
    gcc3 = gcc.reshape(G, C, 1)
    gcr3 = gcrow_ref[...]
    ii = lax.broadcasted_iota(jnp.int32, (G, C, C), 1)
    jj = lax.broadcasted_iota(jnp.int32, (G, C, C), 2)
    decay = jnp.exp(jnp.where(ii >= jj, gcc3 - gcr3, -jnp.inf))
    eye = (ii == jj).astype(F32)
    nil = jnp.where(ii > jj, -_bdot_nt(kb3, k3) * decay, 0.0)
    inv = eye + nil
    powk = nil
    for _ in range(int(math.log2(C)) - 1):
        powk = _bdot(powk, powk)
        inv = _bdot(inv, eye + powk)
    rhs = jnp.concatenate([vb, kb * eg], axis=-1).reshape(G, C, 2 * D)
    uw = _bdot(inv, rhs)
    glast = gcr3[:, :, C - 1:C]
    u_ref[...] = uw[:, :, :D]
    wq_ref[:, 0:C, :] = uw[:, :, D:].astype(wq_ref.dtype)
    wq_ref[:, C:2 * C, :] = (q * eg).reshape(G, C, D).astype(wq_ref.dtype)
    kd_ref[...] = (k3 * jnp.exp(glast - gcc3)).astype(kd_ref.dtype)
    a_ref[...] = (_bdot_nt(q3, k3) * decay).astype(a_ref.dtype)
    c_ref[...] = jnp.broadcast_to(jnp.exp(glast), c_ref.shape)


def _dn_prep(p, beta, gc, gcrow, conv_w, l, B, T, tt):
    H = DN_HEADS
    C = DN_CHUNK
    N = T // C
    G = tt // C
    nt = T // tt
    hb = tt // SUBLANES
    qb, kb, vb = P_QKV // LANES, (P_QKV + DN_W) // LANES, (P_QKV + 2 * DN_W) // LANES
    tok = lambda off: pl.BlockSpec((tt, HEAD_DIM), lambda b, h, i: (b * nt + i, off + h))
    halo = lambda off: pl.BlockSpec((SUBLANES, HEAD_DIM),
                                    lambda b, h, i: (jnp.maximum((b * nt + i) * hb - 1, 0), off + h))
    gate = pl.BlockSpec((tt, LANES), lambda b, h, i: (b * nt + i, 0))
    cw = lambda off: pl.BlockSpec((None, DN_CONV, HEAD_DIM), lambda b, h, i: (l, 0, off + h))
    chunked = lambda r, c: pl.BlockSpec((None, None, G, r, c), lambda b, h, i: (b, h, i, 0, 0))
    return pl.pallas_call(
        functools.partial(_dn_prep_kernel, tiles_per_seq=nt),
        grid=(B, H, nt),
        in_specs=[tok(qb), tok(kb), tok(vb), halo(qb), halo(kb), halo(vb), gate, gate, chunked(1, C),
                  cw(0), cw(H), cw(2 * H)],
        out_specs=[chunked(2 * C, HEAD_DIM), chunked(C, HEAD_DIM), chunked(C, HEAD_DIM), chunked(C, C),
                   chunked(1, LANES)],
        out_shape=[jax.ShapeDtypeStruct((B, H, N, 2 * C, HEAD_DIM), BF16),
                   jax.ShapeDtypeStruct((B, H, N, C, HEAD_DIM), F32),
                   jax.ShapeDtypeStruct((B, H, N, C, HEAD_DIM), BF16),
                   jax.ShapeDtypeStruct((B, H, N, C, C), BF16),
                   jax.ShapeDtypeStruct((B, H, N, 1, LANES), F32)],
        compiler_params=_cparams(("parallel", "parallel", "parallel")),
        name="dn_prep",
    )(p, p, p, p, p, p, beta, gc, gcrow, conv_w, conv_w, conv_w)


def _dn_scan_kernel(wq_ref, u_ref, kd_ref, a_ref, c_ref, z_ref, nw_ref, y_ref, s_ref):
    H, G = wq_ref.shape[0], wq_ref.shape[1]
    C = DN_CHUNK
    D = HEAD_DIM

    @pl.when(pl.program_id(1) == 0)
    def _():
        s_ref[...] = jnp.zeros_like(s_ref)

    nw = nw_ref[...]

    def body(n, carry):
        r0 = pl.multiple_of(n * C, C)
        for h in range(H):
            S = s_ref[h]
            m1 = jnp.dot(wq_ref[h, n], S.astype(BF16), preferred_element_type=F32)
            v_new = (u_ref[h, n] - m1[0:C]).astype(BF16)
            o = m1[C:2 * C] + jnp.dot(a_ref[h, n], v_new, preferred_element_type=F32)
            s_ref[h] = S * c_ref[h, n] + lax.dot_general(kd_ref[h, n], v_new, (((0,), (0,)), ((), ())),
                                                         preferred_element_type=F32)
            zz = z_ref[pl.ds(r0, C), h * D:(h + 1) * D]
            y_ref[pl.ds(r0, C), h * D:(h + 1) * D] = (_rms(o, nw) * (zz * _sigmoid(zz))).astype(y_ref.dtype)
        return carry

    lax.fori_loop(0, G, body, 0)


def _dn_scan(wq, u, kd, a, c, p, norm_w, l, B, T, tt):
    H = DN_HEADS
    C = DN_CHUNK
    G = tt // C
    nt = T // tt
    M = B * T
    chunked = lambda r, cc: pl.BlockSpec((None, H, G, r, cc), lambda b, i: (b, 0, i, 0, 0))
    return pl.pallas_call(
        _dn_scan_kernel,
        grid=(B, nt),
        in_specs=[chunked(2 * C, HEAD_DIM), chunked(C, HEAD_DIM), chunked(C, HEAD_DIM), chunked(C, C),
                  chunked(1, LANES),
                  pl.BlockSpec((tt, DN_W), lambda b, i: (b * nt + i, P_Z // DN_W)),
                  pl.BlockSpec((None, 1, HEAD_DIM), lambda b, i: (l, 0, 0))],
        out_specs=pl.BlockSpec((tt, DN_W), lambda b, i: (b * nt + i, 0)),
        out_shape=jax.ShapeDtypeStruct((M, DN_W), BF16),
        scratch_shapes=[pltpu.VMEM((H, HEAD_DIM, HEAD_DIM), F32)],
        compiler_params=_cparams(("parallel", "arbitrary")),
        name="dn_scan",
    )(wq, u, kd, a, c, p, norm_w)


def _dn_layer(p, conv_w, alog_rows, dt_rows, norm_w, l, B, T):
    N = T // DN_CHUNK
    tt = min(1024, T)
    beta, gc = _gates(p, alog_rows, dt_rows, l, B, T)
    gcrow = gc[:, DN_HEADS:2 * DN_HEADS].reshape(B, N, DN_CHUNK, DN_HEADS).transpose(0, 3, 1, 2)
    gcrow = gcrow.reshape(B, DN_HEADS, N, 1, DN_CHUNK)
    wq, u, kd, a, c = _dn_prep(p, beta, gc, gcrow, conv_w, l, B, T, tt)
    return _dn_scan(wq, u, kd, a, c, p, norm_w, l, B, T, tt)


POOL_HALO = 16


def _pool_kernel(x_ref, halo_ref, w_ref, scale_ref, o_ref, *, tiles_per_seq):
    i = pl.program_id(0)
    tm = x_ref.shape[0]
    t0 = (i % tiles_per_seq) * tm
    first = (i % tiles_per_seq) == 0
    pos = t0 + 1 + lax.broadcasted_iota(jnp.int32, (tm, POOL_GROUP_DIM), 0)
    for gi, win in enumerate(POOL_WINDOWS):
        cols = slice(gi * POOL_GROUP_DIM, (gi + 1) * POOL_GROUP_DIM)
        halo = jnp.where(first, 0.0, halo_ref[:, cols])
        xg = x_ref[:, cols]
        ext = jnp.concatenate([halo, xg], axis=0)
        acc = ext
        s = 1
        while s < win:
            acc = acc + pltpu.roll(acc, s, axis=0)
            s *= 2
        cnt = jnp.minimum(pos, win).astype(F32)
        y = acc[POOL_HALO:] / cnt - xg
        y = jnp.dot(y.astype(BF16), w_ref[gi].astype(BF16), preferred_element_type=F32)
        o_ref[:, cols] = (y * scale_ref[:, cols]).astype(o_ref.dtype)


def _pool(p, pool_w, pool_scale, l, T, tm):
    M = p.shape[0]
    cb = P_POOL // POOL_W
    hb = tm // POOL_HALO
    return pl.pallas_call(
        functools.partial(_pool_kernel, tiles_per_seq=T // tm),
        grid=(M // tm,),
        in_specs=[pl.BlockSpec((tm, POOL_W), lambda i: (i, cb)),
                  pl.BlockSpec((POOL_HALO, POOL_W), lambda i: (jnp.maximum(i * hb - 1, 0), cb)),
                  pl.BlockSpec((None, POOL_GROUPS, POOL_GROUP_DIM, POOL_GROUP_DIM), lambda i: (l, 0, 0, 0)),
                  pl.BlockSpec((None, 1, POOL_W), lambda i: (l, 0, 0))],
        out_specs=pl.BlockSpec((tm, POOL_W), lambda i: (i, 0)),
        out_shape=jax.ShapeDtypeStruct((M, POOL_W), BF16),
        compiler_params=_cparams(("parallel",)),
        name="pool",
    )(p, p, pool_w, pool_scale)


def _swa_kernel(q_ref, k_ref, kp_ref, v_ref, vp_ref, cos_ref, sin_ref, cosp_ref, sinp_ref, sink_ref, o_ref,
                *, tiles_per_seq):
    i = pl.program_id(1)
    first = (i % tiles_per_seq) == 0
    tm = q_ref.shape[0]
    blk = SWA_BLOCK
    G = SWA_HEADS // SWA_KV_HEADS
    D = HEAD_DIM

    def rope(x, cos, sin):
        return x * cos + pltpu.roll(x, D // 2, axis=1) * sin

    cos = cos_ref[...]
    sin = sin_ref[...]
    cosx = jnp.concatenate([cosp_ref[...], cos], axis=0)
    sinx = jnp.concatenate([sinp_ref[...], sin], axis=0)
    kx = rope(jnp.concatenate([kp_ref[...], k_ref[...]], axis=0), cosx, sinx).astype(BF16)
    vx = jnp.concatenate([vp_ref[...], v_ref[...]], axis=0).astype(BF16)
    scale = D ** -0.5
    qh = [(rope(q_ref[:, g * D:(g + 1) * D], cos, sin) * scale).astype(BF16) for g in range(G)]

    ri = lax.broadcasted_iota(jnp.int32, (G * blk, 2 * blk), 0)
    ii = ri % blk
    jj = lax.broadcasted_iota(jnp.int32, (G * blk, 2 * blk), 1)
    band = jnp.logical_or(jnp.logical_and(jj < blk, jj > ii), jnp.logical_and(jj >= blk, jj - blk <= ii))
    no_prev = jnp.where(first, blk, 0)
    band_first = jnp.logical_and(band, jj >= no_prev)
    sink = sink_ref[:, 0:1]
    for g in range(1, G):
        sink = jnp.where(ri[:, 0:1] >= g * blk, sink_ref[:, g:g + 1], sink)
    for b in range(tm // blk):
        qs = jnp.concatenate([q[b * blk:(b + 1) * blk] for q in qh], axis=0)
        keys = kx[b * blk:(b + 2) * blk]
        s = lax.dot_general(qs, keys, (((1,), (1,)), ((), ())), preferred_element_type=F32)
        s = jnp.where(band_first if b == 0 else band, s, -jnp.inf)
        m = jnp.maximum(jnp.max(s, axis=-1, keepdims=True), sink)
        e = jnp.exp(s - m)
        den = jnp.sum(e, axis=-1, keepdims=True) + jnp.exp(sink - m)
        o = jnp.dot(e.astype(BF16), vx[b * blk:(b + 2) * blk], preferred_element_type=F32) / den
        for g in range(G):
            o_ref[b * blk:(b + 1) * blk, g * D:(g + 1) * D] = o[g * blk:(g + 1) * blk].astype(o_ref.dtype)


def _swa(p, cos, sin, sink_rows, l, T, tm):
    M = p.shape[0]
    blk = SWA_BLOCK
    r = tm // blk
    qb, kb, vb = P_SQ // SWA_GROUP_W, P_SK // HEAD_DIM, P_SV // HEAD_DIM
    prev = lambda i: jnp.maximum(i * r - 1, 0)
    return pl.pallas_call(
        functools.partial(_swa_kernel, tiles_per_seq=T // tm),
        grid=(SWA_KV_HEADS, M // tm),
        in_specs=[pl.BlockSpec((tm, SWA_GROUP_W), lambda kv, i: (i, qb + kv)),
                  pl.BlockSpec((tm, HEAD_DIM), lambda kv, i: (i, kb + kv)),
                  pl.BlockSpec((blk, HEAD_DIM), lambda kv, i: (prev(i), kb + kv)),
                  pl.BlockSpec((tm, HEAD_DIM), lambda kv, i: (i, vb + kv)),
                  pl.BlockSpec((blk, HEAD_DIM), lambda kv, i: (prev(i), vb + kv)),
                  pl.BlockSpec((tm, HEAD_DIM), lambda kv, i: (i, 0)),
                  pl.BlockSpec((tm, HEAD_DIM), lambda kv, i: (i, 0)),
                  pl.BlockSpec((blk, HEAD_DIM), lambda kv, i: (prev(i), 0)),
                  pl.BlockSpec((blk, HEAD_DIM), lambda kv, i: (prev(i), 0)),
                  pl.BlockSpec((None, None, 1, LANES), lambda kv, i: (l, kv, 0, 0))],
        out_specs=pl.BlockSpec((tm, SWA_GROUP_W), lambda kv, i: (i, kv)),
        out_shape=jax.ShapeDtypeStruct((M, SWA_W), BF16),
        compiler_params=_cparams(("parallel", "parallel")),
        name="swa",
    )(p, p, p, p, p, cos, sin, cos, sin, sink_rows)


def _outproj_kernel(ydn_ref, ypool_ref, yswa_ref, w_ref, x_ref, gpost_ref, gffn_ref, xo_ref, h_ref):
    mix = jnp.dot(ydn_ref[...], w_ref[0:DN_W, :], preferred_element_type=F32)
    mix = mix + jnp.dot(ypool_ref[...], w_ref[DN_W:DN_W + POOL_W, :], preferred_element_type=F32)
    mix = mix + jnp.dot(yswa_ref[...], w_ref[DN_W + POOL_W:, :], preferred_element_type=F32)
    xn = x_ref[...] + _rms(mix, gpost_ref[...])
    xo_ref[...] = xn
    h_ref[...] = _rms(xn, gffn_ref[...]).astype(h_ref.dtype)


def _out_proj(ydn, ypool, yswa, w_out, x2, gpost, gffn, l, tm):
    M, D = x2.shape
    row = lambda w: pl.BlockSpec((tm, w), lambda i: (i, 0))
    vec = pl.BlockSpec((None, 1, D), lambda i: (l, 0, 0))
    return pl.pallas_call(
        _outproj_kernel,
        grid=(M // tm,),
        in_specs=[row(DN_W), row(POOL_W), row(SWA_W),
                  pl.BlockSpec((None, w_out.shape[1], D), lambda i: (l, 0, 0)), row(D), vec, vec],
        out_specs=[row(D), row(D)],
        out_shape=[jax.ShapeDtypeStruct((M, D), F32), jax.ShapeDtypeStruct((M, D), BF16)],
        compiler_params=_cparams(("parallel",)),
        name="out_proj",
    )(ydn, ypool, yswa, w_out, x2, gpost, gffn)


def _ffn_up_kernel(h_ref, wa_ref, wb_ref, cwa_ref, cwb_ref, cba_ref, cbb_ref, g_ref, wa16, wb16, ua, ub,
                   *, tiles_per_seq):
    i = pl.program_id(1)
    tm = h_ref.shape[0]
    pad = SUBLANES

    @pl.when(i == 0)
    def _():
        wa16[...] = wa_ref[...].astype(BF16)
        wb16[...] = wb_ref[...].astype(BF16)

    @pl.when(i % tiles_per_seq == 0)
    def _():
        ua[0:pad, :] = jnp.zeros((pad, ua.shape[1]), F32)
        ub[0:pad, :] = jnp.zeros((pad, ub.shape[1]), F32)

    hb = h_ref[...]

    def branch(w16, scr, cw_ref, cb_ref):
        scr[pad:pad + tm, :] = jnp.dot(hb, w16[...], preferred_element_type=F32)
        acc = cb_ref[...] + scr[pad:pad + tm, :] * cw_ref[FFN_CONV - 1:FFN_CONV, :]
        for s in range(1, FFN_CONV):
            acc = acc + scr[pad - s:pad - s + tm, :] * cw_ref[FFN_CONV - 1 - s:FFN_CONV - s, :]
        scr[0:pad, :] = scr[tm:tm + pad, :]
        return acc

    a = branch(wa16, ua, cwa_ref, cba_ref)
    b = branch(wb16, ub, cwb_ref, cbb_ref)
    g_ref[...] = (a * _sigmoid(a) * b).astype(g_ref.dtype)


def _ffn_up(h, w_up, conv_w, conv_b, l, T, tm, tn):
    M, D = h.shape
    F = w_up.shape[2] // 2
    nj = F // tn
    return pl.pallas_call(
        functools.partial(_ffn_up_kernel, tiles_per_seq=T // tm),
        grid=(nj, M // tm),
        in_specs=[pl.BlockSpec((tm, D), lambda j, i: (i, 0)),
                  pl.BlockSpec((None, D, tn), lambda j, i: (l, 0, j)),
                  pl.BlockSpec((None, D, tn), lambda j, i: (l, 0, nj + j)),
                  pl.BlockSpec((None, FFN_CONV, tn), lambda j, i: (l, 0, j)),
                  pl.BlockSpec((None, FFN_CONV, tn), lambda j, i: (l, 0, nj + j)),
                  pl.BlockSpec((None, 1, tn), lambda j, i: (l, 0, j)),
                  pl.BlockSpec((None, 1, tn), lambda j, i: (l, 0, nj + j))],
        out_specs=pl.BlockSpec((tm, tn), lambda j, i: (i, j)),
        out_shape=jax.ShapeDtypeStruct((M, F), BF16),
        scratch_shapes=[pltpu.VMEM((D, tn), BF16)] * 2 + [pltpu.VMEM((tm + SUBLANES, tn), F32)] * 2,
        compiler_params=_cparams(("parallel", "arbitrary")),
        name="ffn_up",
    )(h, w_up, w_up, conv_w, conv_w, conv_b, conv_b)


def _ffn_down_kernel(g_ref, w_ref, x_ref, gpost_ref, gnext_ref, xo_ref, h_ref):
    k = pl.program_id(1)
    part = jnp.dot(g_ref[...], w_ref[...], preferred_element_type=F32)

    @pl.when(k == 0)
    def _():
        xo_ref[...] = part

    @pl.when(k > 0)
    def _():
        xo_ref[...] += part

    @pl.when(k == pl.num_programs(1) - 1)
    def _():
        xn = x_ref[...] + _rms(xo_ref[...], gpost_ref[...])
        xo_ref[...] = xn
        h_ref[...] = _rms(xn, gnext_ref[...]).astype(h_ref.dtype)


def _ffn_down(g, w_down, x2, gpost, gnext, l, lnext, tm, tk):
    M, D = x2.shape
    F = g.shape[1]
    return pl.pallas_call(
        _ffn_down_kernel,
        grid=(M // tm, F // tk),
        in_specs=[pl.BlockSpec((tm, tk), lambda i, k: (i, k)),
                  pl.BlockSpec((None, tk, D), lambda i, k: (l, k, 0)),
                  pl.BlockSpec((tm, D), lambda i, k: (i, 0)),
                  pl.BlockSpec((None, 1, D), lambda i, k: (l, 0, 0)),
                  pl.BlockSpec((None, 1, D), lambda i, k: (lnext, 0, 0))],
        out_specs=[pl.BlockSpec((tm, D), lambda i, k: (i, 0))] * 2,
        out_shape=[jax.ShapeDtypeStruct((M, D), F32), jax.ShapeDtypeStruct((M, D), BF16)],
        compiler_params=_cparams(("parallel", "arbitrary")),
        name="ffn_down",
    )(g, w_down, x2, gpost, gnext)


def _rows(v):
    return v.astype(F32).reshape(v.shape[0], 1, v.shape[1])


def kernel(x, positions, norm_mix_pre, w_in, dn_conv_w, dn_a_log, dn_dt_bias, dn_norm_w, pool_w, pool_scale,
           swa_sinks, w_out, norm_mix_post, norm_ffn_pre, ffn_w_up, ffn_conv_w, ffn_conv_b, ffn_w_down,
           norm_ffn_post):
    B, T, D = x.shape
    depth = w_in.shape[0]
    M = B * T
    tm = min(512, T)
    G = SWA_HEADS // SWA_KV_HEADS
    x2 = x.reshape(M, D).astype(F32)

    w_in16 = _prep_w_in(w_in, 256)
    w_out16 = _cast_bf16(w_out, 512)
    w_down16 = _cast_bf16(ffn_w_down, 512)
    gate_pad = ((0, 0), (DN_HEADS, LANES - 2 * DN_HEADS))
    alog_rows = _rows(jnp.pad(dn_a_log, gate_pad))
    dt_rows = _rows(jnp.pad(dn_dt_bias, gate_pad))
    sink_rows = jnp.pad(swa_sinks.astype(F32).reshape(depth, SWA_KV_HEADS, 1, G),
                        ((0, 0), (0, 0), (0, 0), (0, LANES - G)))
    g_mix_pre, g_mix_post = _rows(norm_mix_pre), _rows(norm_mix_post)
    g_ffn_pre, g_ffn_post = _rows(norm_ffn_pre), _rows(norm_ffn_post)
    dn_nw, pscale, conv_b = _rows(dn_norm_w), _rows(pool_scale), _rows(ffn_conv_b)

    cos, sin = _rope_tables(positions, tm)
    h = _prenorm(x2, g_mix_pre, tm)
    for l in range(depth):
        p = _in_proj(h, w_in16, l, tm, P_W // 3)
        y_dn = _dn_layer(p, dn_conv_w, alog_rows, dt_rows, dn_nw, l, B, T)
        y_pool = _pool(p, pool_w, pscale, l, T, tm)
        y_swa = _swa(p, cos, sin, sink_rows, l, T, tm)
        x2, h = _out_proj(y_dn, y_pool, y_swa, w_out16, x2, g_mix_post, g_ffn_pre, l, tm)
        g = _ffn_up(h, ffn_w_up, ffn_conv_w, conv_b, l, T, tm, 512)
        x2, h = _ffn_down(g, w_down16, x2, g_ffn_post, g_mix_pre, l, min(l + 1, depth - 1), 2 * tm, 512)
    return x2.reshape(B, T, D).astype(x.dtype)
```

```python
import functools
import math

import numpy as np
import jax
import jax.numpy as jnp
from jax import lax
from jax.experimental import pallas as pl
from jax.experimental.pallas import tpu as pltpu

HEAD_DIM = 128
DN_HEADS = 6
DN_CONV = 4
DN_CHUNK = 64
POOL_WINDOWS = (2, 4, 8, 16)
POOL_GROUPS = 4
POOL_GROUP_DIM = 128
SWA_HEADS = 6
SWA_KV_HEADS = 2
SWA_WINDOW = 128
SWA_BLOCK = 128
ROPE_THETA = 10000.0
FFN_CONV = 3
NORM_EPS = 1e-6

DN_W = DN_HEADS * HEAD_DIM
POOL_W = POOL_GROUPS * POOL_GROUP_DIM
SWA_W = SWA_HEADS * HEAD_DIM
SWA_KV_W = SWA_KV_HEADS * HEAD_DIM
SWA_GROUP_W = SWA_W // SWA_KV_HEADS
LANES = 128
SUBLANES = 8

R_QKV = 0
R_Z = R_QKV + 3 * DN_W
R_B = R_Z + DN_W
R_POOL = R_B + 2 * DN_HEADS
R_SQ = R_POOL + POOL_W
R_SK = R_SQ + SWA_W
R_SV = R_SK + SWA_KV_W
R_W = R_SV + SWA_KV_W
P_Z = 0
P_SQ = P_Z + DN_W
P_POOL = P_SQ + SWA_W
P_SK = P_POOL + POOL_W
P_SV = P_SK + SWA_KV_W
P_BA = P_SV + SWA_KV_W
P_QKV = P_BA + LANES
P_W = P_QKV + 3 * DN_W
assert P_Z % DN_W == 0 and P_SQ % SWA_GROUP_W == 0 and P_POOL % POOL_W == 0
assert P_SK % HEAD_DIM == 0 and P_SV % HEAD_DIM == 0 and P_BA % LANES == 0 and P_QKV % LANES == 0

VMEM_LIMIT = 56 * 1024 * 1024

F32 = jnp.float32
BF16 = jnp.bfloat16


def _cparams(sem):
    return pltpu.CompilerParams(dimension_semantics=sem, vmem_limit_bytes=VMEM_LIMIT)


def _rms(v, gain):
    return v * lax.rsqrt(jnp.mean(v * v, axis=-1, keepdims=True) + NORM_EPS) * gain


def _sigmoid(v):
    return 1.0 / (1.0 + jnp.exp(-v))


def _bdot(a, b):
    return jnp.einsum("gik,gkj->gij", a.astype(BF16), b.astype(BF16), preferred_element_type=F32)


def _bdot_nt(a, b):
    return jnp.einsum("gik,gjk->gij", a.astype(BF16), b.astype(BF16), preferred_element_type=F32)


def _cast_kernel(w_ref, o_ref):
    o_ref[...] = w_ref[...].astype(o_ref.dtype)


def _cast_bf16(w, tr):
    L, R, C = w.shape
    return pl.pallas_call(
        _cast_kernel,
        grid=(L, R // tr),
        in_specs=[pl.BlockSpec((None, tr, C), lambda l, i: (l, i, 0))],
        out_specs=pl.BlockSpec((None, tr, C), lambda l, i: (l, i, 0)),
        out_shape=jax.ShapeDtypeStruct((L, R, C), BF16),
        compiler_params=_cparams(("parallel", "parallel")),
        name="cast_bf16",
    )(w)


def _win_kernel(w_ref, o_ref):
    w = w_ref[...]
    sec = lambda start, width: w[:, start:start + width].astype(o_ref.dtype)
    o_ref[:, P_Z:P_Z + DN_W] = sec(R_Z, DN_W)
    o_ref[:, P_SQ:P_SQ + SWA_W] = sec(R_SQ, SWA_W)
    o_ref[:, P_POOL:P_POOL + POOL_W] = sec(R_POOL, POOL_W)
    o_ref[:, P_SK:P_SK + SWA_KV_W] = sec(R_SK, SWA_KV_W)
    o_ref[:, P_SV:P_SV + SWA_KV_W] = sec(R_SV, SWA_KV_W)
    gates = w[:, R_B:R_B + LANES]
    lane = lax.broadcasted_iota(jnp.int32, gates.shape, 1)
    o_ref[:, P_BA:P_BA + LANES] = jnp.where(lane < 2 * DN_HEADS, gates, 0.0).astype(o_ref.dtype)
    o_ref[:, P_QKV:P_QKV + 3 * DN_W] = sec(R_QKV, 3 * DN_W)


def _prep_w_in(w_in, tr):
    L, K, _ = w_in.shape
    return pl.pallas_call(
        _win_kernel,
        grid=(L, K // tr),
        in_specs=[pl.BlockSpec((None, tr, R_W), lambda l, i: (l, i, 0))],
        out_specs=pl.BlockSpec((None, tr, P_W), lambda l, i: (l, i, 0)),
        out_shape=jax.ShapeDtypeStruct((L, K, P_W), BF16),
        compiler_params=_cparams(("parallel", "parallel")),
        name="prep_w_in",
    )(w_in)


def _prenorm_kernel(x_ref, g_ref, h_ref):
    h_ref[...] = _rms(x_ref[...], g_ref[...]).astype(h_ref.dtype)


def _prenorm(x2, gains, tm):
    M, D = x2.shape
    return pl.pallas_call(
        _prenorm_kernel,
        grid=(M // tm,),
        in_specs=[pl.BlockSpec((tm, D), lambda i: (i, 0)), pl.BlockSpec((None, 1, D), lambda i: (0, 0, 0))],
        out_specs=pl.BlockSpec((tm, D), lambda i: (i, 0)),
        out_shape=jax.ShapeDtypeStruct((M, D), BF16),
        compiler_params=_cparams(("parallel",)),
        name="prenorm",
    )(x2, gains)


def _rope_kernel(pos_ref, freq_ref, cos_ref, sin_ref):
    ang = pos_ref[...] * freq_ref[...]
    lane = lax.broadcasted_iota(jnp.int32, ang.shape, 1)
    cos_ref[...] = jnp.cos(ang)
    sin_ref[...] = jnp.where(lane < HEAD_DIM // 2, -jnp.sin(ang), jnp.sin(ang))


def _rope_tables(positions, tm):
    M = positions.size
    pos = positions.astype(F32).reshape(M, 1)
    inv = (1.0 / (ROPE_THETA ** (np.arange(0, HEAD_DIM, 2, dtype=np.float32) / HEAD_DIM))).astype(np.float32)
    freq = jnp.asarray(np.concatenate([inv, inv])[None, :])
    return pl.pallas_call(
        _rope_kernel,
        grid=(M // tm,),
        in_specs=[pl.BlockSpec((tm, 1), lambda i: (i, 0)), pl.BlockSpec((1, HEAD_DIM), lambda i: (0, 0))],
        out_specs=[pl.BlockSpec((tm, HEAD_DIM), lambda i: (i, 0))] * 2,
        out_shape=[jax.ShapeDtypeStruct((M, HEAD_DIM), F32)] * 2,
        compiler_params=_cparams(("parallel",)),
        name="rope_tables",
    )(pos, freq)


def _mm_kernel(a_ref, w_ref, o_ref):
    o_ref[...] = jnp.dot(a_ref[...], w_ref[...], preferred_element_type=F32)


def _in_proj(h, w, l, tm, tn):
    M, K = h.shape
    N = w.shape[2]
    return pl.pallas_call(
        _mm_kernel,
        grid=(N // tn, M // tm),
        in_specs=[pl.BlockSpec((tm, K), lambda j, i: (i, 0)), pl.BlockSpec((None, K, tn), lambda j, i: (l, 0, j))],
        out_specs=pl.BlockSpec((tm, tn), lambda j, i: (i, j)),
        out_shape=jax.ShapeDtypeStruct((M, N), F32),
        compiler_params=_cparams(("parallel", "parallel")),
        name="in_proj",
    )(h, w)


def _gate_kernel(ba_ref, alog_ref, dt_ref, beta_ref, gc_ref):
    p = ba_ref[...]
    beta_ref[...] = _sigmoid(p)
    sp = p + dt_ref[...]
    softplus = jnp.maximum(sp, 0.0) + jnp.log(1.0 + jnp.exp(-jnp.abs(sp)))
    g = -jnp.exp(alog_ref[...]) * softplus
    row = lax.broadcasted_iota(jnp.int32, g.shape, 0) % DN_CHUNK
    s = 1
    while s < DN_CHUNK:
        g = g + jnp.where(row >= s, pltpu.roll(g, s, axis=0), 0.0)
        s *= 2
    gc_ref[...] = g


def _gates(p, alog_rows, dt_rows, l, B, T):
    M = p.shape[0]
    cb = P_BA // LANES
    vec = pl.BlockSpec((None, 1, LANES), lambda b: (l, 0, 0))
    return pl.pallas_call(
        _gate_kernel,
        grid=(B,),
        in_specs=[pl.BlockSpec((T, LANES), lambda b: (b, cb)), vec, vec],
        out_specs=[pl.BlockSpec((T, LANES), lambda b: (b, 0))] * 2,
        out_shape=[jax.ShapeDtypeStruct((M, LANES), F32)] * 2,
        compiler_params=_cparams(("parallel",)),
        name="dn_gates",
    )(p, alog_rows, dt_rows)


def _dn_prep_kernel(q_ref, k_ref, v_ref, qh_ref, kh_ref, vh_ref, beta_ref, gc_ref, gcrow_ref,
                    cwq_ref, cwk_ref, cwv_ref, wq_ref, u_ref, kd_ref, a_ref, c_ref, *, tiles_per_seq):
    h = pl.program_id(1)
    first = (pl.program_id(2) % tiles_per_seq) == 0
    tt = q_ref.shape[0]
    C = DN_CHUNK
    G = tt // C
    D = HEAD_DIM
    pad = SUBLANES

    def conv_silu(x_ref, halo_ref, cw_ref):
        halo = jnp.where(first, 0.0, halo_ref[...])
        ext = jnp.concatenate([halo, x_ref[...]], axis=0)
        acc = ext[pad:pad + tt] * cw_ref[DN_CONV - 1:DN_CONV, :]
        for s in range(1, DN_CONV):
            acc = acc + ext[pad - s:pad - s + tt] * cw_ref[DN_CONV - 1 - s:DN_CONV - s, :]
        return acc * _sigmoid(acc)

    q = conv_silu(q_ref, qh_ref, cwq_ref)
    k = conv_silu(k_ref, kh_ref, cwk_ref)
    v = conv_silu(v_ref, vh_ref, cwv_ref)
    q = q * lax.rsqrt(jnp.sum(q * q, axis=-1, keepdims=True) + NORM_EPS) * (D ** -0.5)
    k = k * lax.rsqrt(jnp.sum(k * k, axis=-1, keepdims=True) + NORM_EPS)
    lane = lax.broadcasted_iota(jnp.int32, (tt, LANES), 1)
    beta = jnp.sum(jnp.where(lane == h, beta_ref[...], 0.0), axis=-1, keepdims=True)
    gcc = jnp.sum(jnp.where(lane == h + DN_HEADS, gc_ref[...], 0.0), axis=-1, keepdims=True)
    eg = jnp.exp(gcc)
    kb = k * beta
    vb = v * beta

    q3 = q.reshape(G, C, D)
    k3 = k.reshape(G, C, D)
    kb3 = kb.reshape(G, C, D)
    gcc3 = gcc.reshape(G, C, 1)
    gcr3 = gcrow_ref[...]
    ii = lax.broadcasted_iota(jnp.int32, (G, C, C), 1)
    jj = lax.broadcasted_iota(jnp.int32, (G, C, C), 2)
    decay = jnp.exp(jnp.where(ii >= jj, gcc3 - gcr3, -jnp.inf))
    eye = (ii == jj).astype(F32)
    nil = jnp.where(ii > jj, -_bdot_nt(kb3, k3) * decay, 0.0)
    inv = eye + nil
    powk = nil
    for _ in range(int(math.log2(C)) - 1):
        powk = _bdot(powk, powk)
        inv = _bdot(inv, eye + powk)
    rhs = jnp.concatenate([vb, kb * eg], axis=-1).reshape(G, C, 2 * D)
    uw = _bdot(inv, rhs)
    glast = gcr3[:, :, C - 1:C]
    u_ref[...] = uw[:, :, :D]
    wq_ref[:, 0:C, :] = uw[:, :, D:].astype(wq_ref.dtype)
    wq_ref[:, C:2 * C, :] = (q * eg).reshape(G, C, D).astype(wq_ref.dtype)
    kd_ref[...] = (k3 * jnp.exp(glast - gcc3)).astype(kd_ref.dtype)
    a_ref[...] = (_bdot_nt(q3, k3) * decay).astype(a_ref.dtype)
    c_ref[...] = jnp.broadcast_to(jnp.exp(glast), c_ref.shape)


def _dn_prep(p, beta, gc, gcrow, conv_w, l, B, T, tt):
    H = DN_HEADS
    C = DN_CHUNK
    N = T // C
    G = tt // C
    nt = T // tt
    hb = tt // SUBLANES
    qb, kb, vb = P_QKV // LANES, (P_QKV + DN_W) // LANES, (P_QKV + 2 * DN_W) // LANES
    tok = lambda off: pl.BlockSpec((tt, HEAD_DIM), lambda b, h, i: (b * nt + i, off + h))
    halo = lambda off: pl.BlockSpec((SUBLANES, HEAD_DIM),
                                    lambda b, h, i: (jnp.maximum((b * nt + i) * hb - 1, 0), off + h))
    gate = pl.BlockSpec((tt, LANES), lambda b, h, i: (b * nt + i, 0))
    cw = lambda off: pl.BlockSpec((None, DN_CONV, HEAD_DIM), lambda b, h, i: (l, 0, off + h))
    chunked = lambda r, c: pl.BlockSpec((None, None, G, r, c), lambda b, h, i: (b, h, i, 0, 0))
    return pl.pallas_call(
        functools.partial(_dn_prep_kernel, tiles_per_seq=nt),
        grid=(B, H, nt),
        in_specs=[tok(qb), tok(kb), tok(vb), halo(qb), halo(kb), halo(vb), gate, gate, chunked(1, C),
                  cw(0), cw(H), cw(2 * H)],
        out_specs=[chunked(2 * C, HEAD_DIM), chunked(C, HEAD_DIM), chunked(C, HEAD_DIM), chunked(C, C),
                   chunked(1, LANES)],
        out_shape=[jax.ShapeDtypeStruct((B, H, N, 2 * C, HEAD_DIM), BF16),
                   jax.ShapeDtypeStruct((B, H, N, C, HEAD_DIM), F32),
                   jax.ShapeDtypeStruct((B, H, N, C, HEAD_DIM), BF16),
                   jax.ShapeDtypeStruct((B, H, N, C, C), BF16),
                   jax.ShapeDtypeStruct((B, H, N, 1, LANES), F32)],
        compiler_params=_cparams(("parallel", "parallel", "parallel")),
        name="dn_prep",
    )(p, p, p, p, p, p, beta, gc, gcrow, conv_w, conv_w, conv_w)


def _dn_scan_kernel(wq_ref, u_ref, kd_ref, a_ref, c_ref, z_ref, nw_ref, y_ref, s_ref):
    B, H, G = wq_ref.shape[0], wq_ref.shape[1], wq_ref.shape[2]
    C = DN_CHUNK
    D = HEAD_DIM

    @pl.when(pl.program_id(0) == 0)
    def _():
        s_ref[...] = jnp.zeros_like(s_ref)

    nw = nw_ref[...]
    chains = [(b, h) for b in range(B) for h in range(H)]

    def body(n, carry):
        r0 = pl.multiple_of(n * C, C)
        S = [s_ref[b * H + h] for b, h in chains]
        m1 = [jnp.dot(wq_ref[b, h, n], S[i].astype(BF16), preferred_element_type=F32)
              for i, (b, h) in enumerate(chains)]
        v_new = [(u_ref[b, h, n] - m1[i][0:C]).astype(BF16) for i, (b, h) in enumerate(chains)]
        o = [m1[i][C:2 * C] + jnp.dot(a_ref[b, h, n], v_new[i], preferred_element_type=F32)
             for i, (b, h) in enumerate(chains)]
        kv = [lax.dot_general(kd_ref[b, h, n], v_new[i], (((0,), (0,)), ((), ())), preferred_element_type=F32)
              for i, (b, h) in enumerate(chains)]
        for i, (b, h) in enumerate(chains):
            s_ref[b * H + h] = S[i] * c_ref[b, h, n] + kv[i]
            zz = z_ref[b, pl.ds(r0, C), h * D:(h + 1) * D]
            y_ref[b, pl.ds(r0, C), h * D:(h + 1) * D] = (_rms(o[i], nw) * (zz * _sigmoid(zz))).astype(y_ref.dtype)
        return carry

    lax.fori_loop(0, G, body, 0)


def _dn_scan(wq, u, kd, a, c, p, norm_w, l, B, T, tt):
    H = DN_HEADS
    C = DN_CHUNK
    G = tt // C
    nt = T // tt
    chunked = lambda r, cc: pl.BlockSpec((B, H, G, r, cc), lambda i: (0, 0, i, 0, 0))
    y = pl.pallas_call(
        _dn_scan_kernel,
        grid=(nt,),
        in_specs=[chunked(2 * C, HEAD_DIM), chunked(C, HEAD_DIM), chunked(C, HEAD_DIM), chunked(C, C),
                  chunked(1, LANES),
                  pl.BlockSpec((B, tt, DN_W), lambda i: (0, i, P_Z // DN_W)),
                  pl.BlockSpec((None, 1, HEAD_DIM), lambda i: (l, 0, 0))],
        out_specs=pl.BlockSpec((B, tt, DN_W), lambda i: (0, i, 0)),
        out_shape=jax.ShapeDtypeStruct((B, T, DN_W), BF16),
        scratch_shapes=[pltpu.VMEM((B * H, HEAD_DIM, HEAD_DIM), F32)],
        compiler_params=_cparams(("arbitrary",)),
        name="dn_scan",
    )(wq, u, kd, a, c, p.reshape(B, T, p.shape[1]), norm_w)
    return y.reshape(B * T, DN_W)


def _dn_layer(p, conv_w, alog_rows, dt_rows, norm_w, l, B, T):
    N = T // DN_CHUNK
    tt = min(1024, T)
    beta, gc = _gates(p, alog_rows, dt_rows, l, B, T)
    gcrow = gc[:, DN_HEADS:2 * DN_HEADS].reshape(B, N, DN_CHUNK, DN_HEADS).transpose(0, 3, 1, 2)
    gcrow = gcrow.reshape(B, DN_HEADS, N, 1, DN_CHUNK)
    wq, u, kd, a, c = _dn_prep(p, beta, gc, gcrow, conv_w, l, B, T, tt)
    return _dn_scan(wq, u, kd, a, c, p, norm_w, l, B, T, tt // 2)


POOL_HALO = 16


def _pool_kernel(x_ref, halo_ref, w_ref, scale_ref, o_ref, *, tiles_per_seq):
    i = pl.program_id(0)
    tm = x_ref.shape[0]
    t0 = (i % tiles_per_seq) * tm
    first = (i % tiles_per_seq) == 0
    pos = t0 + 1 + lax.broadcasted_iota(jnp.int32, (tm, POOL_GROUP_DIM), 0)
    for gi, win in enumerate(POOL_WINDOWS):
        cols = slice(gi * POOL_GROUP_DIM, (gi + 1) * POOL_GROUP_DIM)
        halo = jnp.where(first, 0.0, halo_ref[:, cols])
        xg = x_ref[:, cols]
        ext = jnp.concatenate([halo, xg], axis=0)
        acc = ext
        s = 1
        while s < win:
            acc = acc + pltpu.roll(acc, s, axis=0)
            s *= 2
        cnt = jnp.minimum(pos, win).astype(F32)
        y = acc[POOL_HALO:] / cnt - xg
        y = jnp.dot(y.astype(BF16), w_ref[gi].astype(BF16), preferred_element_type=F32)
        o_ref[:, cols] = (y * scale_ref[:, cols]).astype(o_ref.dtype)


def _pool(p, pool_w, pool_scale, l, T, tm):
    M = p.shape[0]
    cb = P_POOL // POOL_W
    hb = tm // POOL_HALO
    return pl.pallas_call(
        functools.partial(_pool_kernel, tiles_per_seq=T // tm),
        grid=(M // tm,),
        in_specs=[pl.BlockSpec((tm, POOL_W), lambda i: (i, cb)),
                  pl.BlockSpec((POOL_HALO, POOL_W), lambda i: (jnp.maximum(i * hb - 1, 0), cb)),
                  pl.BlockSpec((None, POOL_GROUPS, POOL_GROUP_DIM, POOL_GROUP_DIM), lambda i: (l, 0, 0, 0)),
                  pl.BlockSpec((None, 1, POOL_W), lambda i: (l, 0, 0))],
        out_specs=pl.BlockSpec((tm, POOL_W), lambda i: (i, 0)),
        out_shape=jax.ShapeDtypeStruct((M, POOL_W), BF16),
        compiler_params=_cparams(("parallel",)),
        name="pool",
    )(p, p, pool_w, pool_scale)


def _swa_kernel(q_ref, k_ref, kp_ref, v_ref, vp_ref, cos_ref, sin_ref, cosp_ref, sinp_ref, sink_ref, o_ref,
                *, tiles_per_seq):
    i = pl.program_id(1)
    first = (i % tiles_per_seq) == 0
    tm = q_ref.shape[0]
    blk = SWA_BLOCK
    G = SWA_HEADS // SWA_KV_HEADS
    D = HEAD_DIM

    def rope(x, cos, sin):
        return x * cos + pltpu.roll(x, D // 2, axis=1) * sin

    cos = cos_ref[...]
    sin = sin_ref[...]
    cosx = jnp.concatenate([cosp_ref[...], cos], axis=0)
    sinx = jnp.concatenate([sinp_ref[...], sin], axis=0)
    kx = rope(jnp.concatenate([kp_ref[...], k_ref[...]], axis=0), cosx, sinx).astype(BF16)
    vx = jnp.concatenate([vp_ref[...], v_ref[...]], axis=0).astype(BF16)
    scale = D ** -0.5
    qh = [(rope(q_ref[:, g * D:(g + 1) * D], cos, sin) * scale).astype(BF16) for g in range(G)]

    ri = lax.broadcasted_iota(jnp.int32, (G * blk, 2 * blk), 0)
    ii = ri % blk
    jj = lax.broadcasted_iota(jnp.int32, (G * blk, 2 * blk), 1)
    band = jnp.logical_or(jnp.logical_and(jj < blk, jj > ii), jnp.logical_and(jj >= blk, jj - blk <= ii))
    no_prev = jnp.where(first, blk, 0)
    band_first = jnp.logical_and(band, jj >= no_prev)
    sink = sink_ref[:, 0:1]
    for g in range(1, G):
        sink = jnp.where(ri[:, 0:1] >= g * blk, sink_ref[:, g:g + 1], sink)
    for b in range(tm // blk):
        qs = jnp.concatenate([q[b * blk:(b + 1) * blk] for q in qh], axis=0)
        keys = kx[b * blk:(b + 2) * blk]
        s = lax.dot_general(qs, keys, (((1,), (1,)), ((), ())), preferred_element_type=F32)
        s = jnp.where(band_first if b == 0 else band, s, -jnp.inf)
        m = jnp.maximum(jnp.max(s, axis=-1, keepdims=True), sink)
        e = jnp.exp(s - m)
        den = jnp.sum(e, axis=-1, keepdims=True) + jnp.exp(sink - m)
        o = jnp.dot(e.astype(BF16), vx[b * blk:(b + 2) * blk], preferred_element_type=F32) / den
        for g in range(G):
            o_ref[b * blk:(b + 1) * blk, g * D:(g + 1) * D] = o[g * blk:(g + 1) * blk].astype(o_ref.dtype)


def _swa(p, cos, sin, sink_rows, l, T, tm):
    M = p.shape[0]
    blk = SWA_BLOCK
    r = tm // blk
    qb, kb, vb = P_SQ // SWA_GROUP_W, P_SK // HEAD_DIM, P_SV // HEAD_DIM
    prev = lambda i: jnp.maximum(i * r - 1, 0)
    return pl.pallas_call(
        functools.partial(_swa_kernel, tiles_per_seq=T // tm),
        grid=(SWA_KV_HEADS, M // tm),
        in_specs=[pl.BlockSpec((tm, SWA_GROUP_W), lambda kv, i: (i, qb + kv)),
                  pl.BlockSpec((tm, HEAD_DIM), lambda kv, i: (i, kb + kv)),
                  pl.BlockSpec((blk, HEAD_DIM), lambda kv, i: (prev(i), kb + kv)),
                  pl.BlockSpec((tm, HEAD_DIM), lambda kv, i: (i, vb + kv)),
                  pl.BlockSpec((blk, HEAD_DIM), lambda kv, i: (prev(i), vb + kv)),
                  pl.BlockSpec((tm, HEAD_DIM), lambda kv, i: (i, 0)),
                  pl.BlockSpec((tm, HEAD_DIM), lambda kv, i: (i, 0)),
                  pl.BlockSpec((blk, HEAD_DIM), lambda kv, i: (prev(i), 0)),
                  pl.BlockSpec((blk, HEAD_DIM), lambda kv, i: (prev(i), 0)),
                  pl.BlockSpec((None, None, 1, LANES), lambda kv, i: (l, kv, 0, 0))],
        out_specs=pl.BlockSpec((tm, SWA_GROUP_W), lambda kv, i: (i, kv)),
        out_shape=jax.ShapeDtypeStruct((M, SWA_W), BF16),
        compiler_params=_cparams(("parallel", "parallel")),
        name="swa",
    )(p, p, p, p, p, cos, sin, cos, sin, sink_rows)


def _outproj_kernel(ydn_ref, ypool_ref, yswa_ref, w_ref, x_ref, gpost_ref, gffn_ref, xo_ref, h_ref):
    mix = jnp.dot(ydn_ref[...], w_ref[0:DN_W, :], preferred_element_type=F32)
    mix = mix + jnp.dot(ypool_ref[...], w_ref[DN_W:DN_W + POOL_W, :], preferred_element_type=F32)
    mix = mix + jnp.dot(yswa_ref[...], w_ref[DN_W + POOL_W:, :], preferred_element_type=F32)
    xn = x_ref[...] + _rms(mix, gpost_ref[...])
    xo_ref[...] = xn
    h_ref[...] = _rms(xn, gffn_ref[...]).astype(h_ref.dtype)


def _out_proj(ydn, ypool, yswa, w_out, x2, gpost, gffn, l, tm):
    M, D = x2.shape
    row = lambda w: pl.BlockSpec((tm, w), lambda i: (i, 0))
    vec = pl.BlockSpec((None, 1, D), lambda i: (l, 0, 0))
    return pl.pallas_call(
        _outproj_kernel,
        grid=(M // tm,),
        in_specs=[row(DN_W), row(POOL_W), row(SWA_W),
                  pl.BlockSpec((None, w_out.shape[1], D), lambda i: (l, 0, 0)), row(D), vec, vec],
        out_specs=[row(D), row(D)],
        out_shape=[jax.ShapeDtypeStruct((M, D), F32), jax.ShapeDtypeStruct((M, D), BF16)],
        compiler_params=_cparams(("parallel",)),
        name="out_proj",
    )(ydn, ypool, yswa, w_out, x2, gpost, gffn)


def _ffn_up_kernel(h_ref, wa_ref, wb_ref, cwa_ref, cwb_ref, cba_ref, cbb_ref, g_ref, wa16, wb16, ua, ub,
                   *, tiles_per_seq):
    i = pl.program_id(1)
    tm = h_ref.shape[0]
    pad = SUBLANES

    @pl.when(i == 0)
    def _():
        wa16[...] = wa_ref[...].astype(BF16)
        wb16[...] = wb_ref[...].astype(BF16)

    @pl.when(i % tiles_per_seq == 0)
    def _():
        ua[0:pad, :] = jnp.zeros((pad, ua.shape[1]), F32)
        ub[0:pad, :] = jnp.zeros((pad, ub.shape[1]), F32)

    hb = h_ref[...]

    def branch(w16, scr, cw_ref, cb_ref):
        scr[pad:pad + tm, :] = jnp.dot(hb, w16[...], preferred_element_type=F32)
        acc = cb_ref[...] + scr[pad:pad + tm, :] * cw_ref[FFN_CONV - 1:FFN_CONV, :]
        for s in range(1, FFN_CONV):
            acc = acc + scr[pad - s:pad - s + tm, :] * cw_ref[FFN_CONV - 1 - s:FFN_CONV - s, :]
        scr[0:pad, :] = scr[tm:tm + pad, :]
        return acc

    a = branch(wa16, ua, cwa_ref, cba_ref)
    b = branch(wb16, ub, cwb_ref, cbb_ref)
    g_ref[...] = (a * _sigmoid(a) * b).astype(g_ref.dtype)


def _ffn_up(h, w_up, conv_w, conv_b, l, T, tm, tn):
    M, D = h.shape
    F = w_up.shape[2] // 2
    nj = F // tn
    return pl.pallas_call(
        functools.partial(_ffn_up_kernel, tiles_per_seq=T // tm),
        grid=(nj, M // tm),
        in_specs=[pl.BlockSpec((tm, D), lambda j, i: (i, 0)),
                  pl.BlockSpec((None, D, tn), lambda j, i: (l, 0, j)),
                  pl.BlockSpec((None, D, tn), lambda j, i: (l, 0, nj + j)),
                  pl.BlockSpec((None, FFN_CONV, tn), lambda j, i: (l, 0, j)),
                  pl.BlockSpec((None, FFN_CONV, tn), lambda j, i: (l, 0, nj + j)),
                  pl.BlockSpec((None, 1, tn), lambda j, i: (l, 0, j)),
                  pl.BlockSpec((None, 1, tn), lambda j, i: (l, 0, nj + j))],
        out_specs=pl.BlockSpec((tm, tn), lambda j, i: (i, j)),
        out_shape=jax.ShapeDtypeStruct((M, F), BF16),
        scratch_shapes=[pltpu.VMEM((D, tn), BF16)] * 2 + [pltpu.VMEM((tm + SUBLANES, tn), F32)] * 2,
        compiler_params=_cparams(("parallel", "arbitrary")),
        name="ffn_up",
    )(h, w_up, w_up, conv_w, conv_w, conv_b, conv_b)


def _ffn_down_kernel(g_ref, w_ref, x_ref, gpost_ref, gnext_ref, xo_ref, h_ref):
    f = jnp.dot(g_ref[...], w_ref[...], preferred_element_type=F32)
    xn = x_ref[...] + _rms(f, gpost_ref[...])
    xo_ref[...] = xn
    h_ref[...] = _rms(xn, gnext_ref[...]).astype(h_ref.dtype)


def _ffn_down(g, w_down, x2, gpost, gnext, l, lnext, tm):
    M, D = x2.shape
    F = g.shape[1]
    return pl.pallas_call(
        _ffn_down_kernel,
        grid=(M // tm,),
        in_specs=[pl.BlockSpec((tm, F), lambda i: (i, 0)),
                  pl.BlockSpec((None, F, D), lambda i: (l, 0, 0), pipeline_mode=pl.Buffered(1)),
                  pl.BlockSpec((tm, D), lambda i: (i, 0)),
                  pl.BlockSpec((None, 1, D), lambda i: (l, 0, 0)),
                  pl.BlockSpec((None, 1, D), lambda i: (lnext, 0, 0))],
        out_specs=[pl.BlockSpec((tm, D), lambda i: (i, 0))] * 2,
        out_shape=[jax.ShapeDtypeStruct((M, D), F32), jax.ShapeDtypeStruct((M, D), BF16)],
        compiler_params=_cparams(("parallel",)),
        name="ffn_down",
    )(g, w_down, x2, gpost, gnext)


def _rows(v):
    return v.astype(F32).reshape(v.shape[0], 1, v.shape[1])


def kernel(x, positions, norm_mix_pre, w_in, dn_conv_w, dn_a_log, dn_dt_bias, dn_norm_w, pool_w, pool_scale,
           swa_sinks, w_out, norm_mix_post, norm_ffn_pre, ffn_w_up, ffn_conv_w, ffn_conv_b, ffn_w_down,
           norm_ffn_post):
    B, T, D = x.shape
    depth = w_in.shape[0]
    M = B * T
    tm = min(512, T)
    G = SWA_HEADS // SWA_KV_HEADS
    x2 = x.reshape(M, D).astype(F32)

    w_in16 = _prep_w_in(w_in, 256)
    w_out16 = _cast_bf16(w_out, 512)
    w_down16 = _cast_bf16(ffn_w_down, 512)
    gate_pad = ((0, 0), (DN_HEADS, LANES - 2 * DN_HEADS))
    alog_rows = _rows(jnp.pad(dn_a_log, gate_pad))
    dt_rows = _rows(jnp.pad(dn_dt_bias, gate_pad))
    sink_rows = jnp.pad(swa_sinks.astype(F32).reshape(depth, SWA_KV_HEADS, 1, G),
                        ((0, 0), (0, 0), (0, 0), (0, LANES - G)))
    g_mix_pre, g_mix_post = _rows(norm_mix_pre), _rows(norm_mix_post)
    g_ffn_pre, g_ffn_post = _rows(norm_ffn_pre), _rows(norm_ffn_post)
    dn_nw, pscale, conv_b = _rows(dn_norm_w), _rows(pool_scale), _rows(ffn_conv_b)

    cos, sin = _rope_tables(positions, tm)
    h = _prenorm(x2, g_mix_pre, tm)
    for l in range(depth):
        p = _in_proj(h, w_in16, l, tm, P_W // 3)
        y_dn = _dn_layer(p, dn_conv_w, alog_rows, dt_rows, dn_nw, l, B, T)
        y_pool = _pool(p, pool_w, pscale, l, T, tm)
        y_swa = _swa(p, cos, sin, sink_rows, l, T, tm)
        x2, h = _out_proj(y_dn, y_pool, y_swa, w_out16, x2, g_mix_post, g_ffn_pre, l, tm)
        g = _ffn_up(h, ffn_w_up, ffn_conv_w, conv_b, l, T, tm, 512)
        x2, h = _ffn_down(g, w_down16, x2, g_ffn_post, g_mix_pre, l, min(l + 1, depth - 1), tm // 2)
    return x2.reshape(B, T, D).astype(x.dtype)
```

```python
import functools
import math

import numpy as np
import jax
import jax.numpy as jnp
from jax import lax
from jax.experimental import pallas as pl
from jax.experimental.pallas import tpu as pltpu

HEAD_DIM = 128
DN_HEADS = 6
DN_CONV = 4
DN_CHUNK = 64
POOL_WINDOWS = (2, 4, 8, 16)
POOL_GROUPS = 4
POOL_GROUP_DIM = 128
SWA_HEADS = 6
SWA_KV_HEADS = 2
SWA_WINDOW = 128
SWA_BLOCK = 128
ROPE_THETA = 10000.0
FFN_CONV = 3
NORM_EPS = 1e-6

DN_W = DN_HEADS * HEAD_DIM
POOL_W = POOL_GROUPS * POOL_GROUP_DIM
SWA_W = SWA_HEADS * HEAD_DIM
SWA_KV_W = SWA_KV_HEADS * HEAD_DIM
SWA_GROUP_W = SWA_W // SWA_KV_HEADS
LANES = 128
SUBLANES = 8

R_QKV = 0
R_Z = R_QKV + 3 * DN_W
R_B = R_Z + DN_W
R_POOL = R_B + 2 * DN_HEADS
R_SQ = R_POOL + POOL_W
R_SK = R_SQ + SWA_W
R_SV = R_SK + SWA_KV_W
R_W = R_SV + SWA_KV_W
P_Z = 0
P_SQ = P_Z + DN_W
P_POOL = P_SQ + SWA_W
P_SK = P_POOL + POOL_W
P_SV = P_SK + SWA_KV_W
P_BA = P_SV + SWA_KV_W
P_QKV = P_BA + LANES
P_W = P_QKV + 3 * DN_W
assert P_Z % DN_W == 0 and P_SQ % SWA_GROUP_W == 0 and P_POOL % POOL_W == 0
assert P_SK % HEAD_DIM == 0 and P_SV % HEAD_DIM == 0 and P_BA % LANES == 0 and P_QKV % LANES == 0

VMEM_LIMIT = 56 * 1024 * 1024

F32 = jnp.float32
BF16 = jnp.bfloat16


def _cparams(sem):
    return pltpu.CompilerParams(dimension_semantics=sem, vmem_limit_bytes=VMEM_LIMIT)


def _rms(v, gain):
    return v * lax.rsqrt(jnp.mean(v * v, axis=-1, keepdims=True) + NORM_EPS) * gain


def _sigmoid(v):
    return 1.0 / (1.0 + jnp.exp(-v))


def _bdot(a, b):
    return jnp.einsum("gik,gkj->gij", a.astype(BF16), b.astype(BF16), preferred_element_type=F32)


def _bdot_nt(a, b):
    return jnp.einsum("gik,gjk->gij", a.astype(BF16), b.astype(BF16), preferred_element_type=F32)


def _cast_kernel(w_ref, o_ref):
    o_ref[...] = w_ref[...].astype(o_ref.dtype)


def _cast_bf16(w, tr):
    L, R, C = w.shape
    return pl.pallas_call(
        _cast_kernel,
        grid=(L, R // tr),
        in_specs=[pl.BlockSpec((None, tr, C), lambda l, i: (l, i, 0))],
        out_specs=pl.BlockSpec((None, tr, C), lambda l, i: (l, i, 0)),
        out_shape=jax.ShapeDtypeStruct((L, R, C), BF16),
        compiler_params=_cparams(("parallel", "parallel")),
        name="cast_bf16",
    )(w)


def _win_kernel(w_ref, o_ref):
    w = w_ref[...]
    sec = lambda start, width: w[:, start:start + width].astype(o_ref.dtype)
    o_ref[:, P_Z:P_Z + DN_W] = sec(R_Z, DN_W)
    o_ref[:, P_SQ:P_SQ + SWA_W] = sec(R_SQ, SWA_W)
    o_ref[:, P_POOL:P_POOL + POOL_W] = sec(R_POOL, POOL_W)
    o_ref[:, P_SK:P_SK + SWA_KV_W] = sec(R_SK, SWA_KV_W)
    o_ref[:, P_SV:P_SV + SWA_KV_W] = sec(R_SV, SWA_KV_W)
    gates = w[:, R_B:R_B + LANES]
    lane = lax.broadcasted_iota(jnp.int32, gates.shape, 1)
    o_ref[:, P_BA:P_BA + LANES] = jnp.where(lane < 2 * DN_HEADS, gates, 0.0).astype(o_ref.dtype)
    o_ref[:, P_QKV:P_QKV + 3 * DN_W] = sec(R_QKV, 3 * DN_W)


def _prep_w_in(w_in, tr):
    L, K, _ = w_in.shape
    return pl.pallas_call(
        _win_kernel,
        grid=(L, K // tr),
        in_specs=[pl.BlockSpec((None, tr, R_W), lambda l, i: (l, i, 0))],
        out_specs=pl.BlockSpec((None, tr, P_W), lambda l, i: (l, i, 0)),
        out_shape=jax.ShapeDtypeStruct((L, K, P_W), BF16),
        compiler_params=_cparams(("parallel", "parallel")),
        name="prep_w_in",
    )(w_in)


def _prenorm_kernel(x_ref, g_ref, h_ref):
    h_ref[...] = _rms(x_ref[...], g_ref[...]).astype(h_ref.dtype)


def _prenorm(x2, gains, tm):
    M, D = x2.shape
    return pl.pallas_call(
        _prenorm_kernel,
        grid=(M // tm,),
        in_specs=[pl.BlockSpec((tm, D), lambda i: (i, 0)), pl.BlockSpec((None, 1, D), lambda i: (0, 0, 0))],
        out_specs=pl.BlockSpec((tm, D), lambda i: (i, 0)),
        out_shape=jax.ShapeDtypeStruct((M, D), BF16),
        compiler_params=_cparams(("parallel",)),
        name="prenorm",
    )(x2, gains)


def _rope_kernel(pos_ref, freq_ref, cos_ref, sin_ref):
    ang = pos_ref[...] * freq_ref[...]
    lane = lax.broadcasted_iota(jnp.int32, ang.shape, 1)
    cos_ref[...] = jnp.cos(ang)
    sin_ref[...] = jnp.where(lane < HEAD_DIM // 2, -jnp.sin(ang), jnp.sin(ang))


def _rope_tables(positions, tm):
    M = positions.size
    pos = positions.astype(F32).reshape(M, 1)
    inv = (1.0 / (ROPE_THETA ** (np.arange(0, HEAD_DIM, 2, dtype=np.float32) / HEAD_DIM))).astype(np.float32)
    freq = jnp.asarray(np.concatenate([inv, inv])[None, :])
    return pl.pallas_call(
        _rope_kernel,
        grid=(M // tm,),
        in_specs=[pl.BlockSpec((tm, 1), lambda i: (i, 0)), pl.BlockSpec((1, HEAD_DIM), lambda i: (0, 0))],
        out_specs=[pl.BlockSpec((tm, HEAD_DIM), lambda i: (i, 0))] * 2,
        out_shape=[jax.ShapeDtypeStruct((M, HEAD_DIM), F32)] * 2,
        compiler_params=_cparams(("parallel",)),
        name="rope_tables",
    )(pos, freq)


def _mm_kernel(a_ref, w_ref, o_ref):
    o_ref[...] = jnp.dot(a_ref[...], w_ref[...], preferred_element_type=F32)


def _in_proj(h, w, l, tm, tn):
    M, K = h.shape
    N = w.shape[2]
    return pl.pallas_call(
        _mm_kernel,
        grid=(N // tn, M // tm),
        in_specs=[pl.BlockSpec((tm, K), lambda j, i: (i, 0)), pl.BlockSpec((None, K, tn), lambda j, i: (l, 0, j))],
        out_specs=pl.BlockSpec((tm, tn), lambda j, i: (i, j)),
        out_shape=jax.ShapeDtypeStruct((M, N), F32),
        compiler_params=_cparams(("parallel", "parallel")),
        name="in_proj",
    )(h, w)


def _gate_kernel(ba_ref, alog_ref, dt_ref, beta_ref, gc_ref):
    p = ba_ref[...]
    beta_ref[...] = _sigmoid(p)
    sp = p + dt_ref[...]
    softplus = jnp.maximum(sp, 0.0) + jnp.log(1.0 + jnp.exp(-jnp.abs(sp)))
    g = -jnp.exp(alog_ref[...]) * softplus
    row = lax.broadcasted_iota(jnp.int32, g.shape, 0) % DN_CHUNK
    s = 1
    while s < DN_CHUNK:
        g = g + jnp.where(row >= s, pltpu.roll(g, s, axis=0), 0.0)
        s *= 2
    gc_ref[...] = g


def _gates(p, alog_rows, dt_rows, l, B, T):
    M = p.shape[0]
    cb = P_BA // LANES
    vec = pl.BlockSpec((None, 1, LANES), lambda b: (l, 0, 0))
    return pl.pallas_call(
        _gate_kernel,
        grid=(B,),
        in_specs=[pl.BlockSpec((T, LANES), lambda b: (b, cb)), vec, vec],
        out_specs=[pl.BlockSpec((T, LANES), lambda b: (b, 0))] * 2,
        out_shape=[jax.ShapeDtypeStruct((M, LANES), F32)] * 2,
        compiler_params=_cparams(("parallel",)),
        name="dn_gates",
    )(p, alog_rows, dt_rows)


def _dn_prep_kernel(q_ref, k_ref, v_ref, qh_ref, kh_ref, vh_ref, beta_ref, gc_ref, gcrow_ref,
                    cwq_ref, cwk_ref, cwv_ref, wq_ref, u_ref, kd_ref, a_ref, c_ref, *, tiles_per_seq):
    h = pl.program_id(1)
    first = (pl.program_id(2) % tiles_per_seq) == 0
    tt = q_ref.shape[0]
    C = DN_CHUNK
    G = tt // C
    D = HEAD_DIM
    pad = SUBLANES

    def conv_silu(x_ref, halo_ref, cw_ref):
        halo = jnp.where(first, 0.0, halo_ref[...])
        ext = jnp.concatenate([halo, x_ref[...]], axis=0)
        acc = ext[pad:pad + tt] * cw_ref[DN_CONV - 1:DN_CONV, :]
        for s in range(1, DN_CONV):
            acc = acc + ext[pad - s:pad - s + tt] * cw_ref[DN_CONV - 1 - s:DN_CONV - s, :]
        return acc * _sigmoid(acc)

    q = conv_silu(q_ref, qh_ref, cwq_ref)
    k = conv_silu(k_ref, kh_ref, cwk_ref)
    v = conv_silu(v_ref, vh_ref, cwv_ref)
    q = q * lax.rsqrt(jnp.sum(q * q, axis=-1, keepdims=True) + NORM_EPS) * (D ** -0.5)
    k = k * lax.rsqrt(jnp.sum(k * k, axis=-1, keepdims=True) + NORM_EPS)
    lane = lax.broadcasted_iota(jnp.int32, (tt, LANES), 1)
    beta = jnp.sum(jnp.where(lane == h, beta_ref[...], 0.0), axis=-1, keepdims=True)
    gcc = jnp.sum(jnp.where(lane == h + DN_HEADS, gc_ref[...], 0.0), axis=-1, keepdims=True)
    eg = jnp.exp(gcc)
    kb = k * beta
    vb = v * beta

    q3 = q.reshape(G, C, D)
    k3 = k.reshape(G, C, D)
    kb3 = kb.reshape(G, C, D)
    gcc3 = gcc.reshape(G, C, 1)
    gcr3 = gcrow_ref[...]
    ii = lax.broadcasted_iota(jnp.int32, (G, C, C), 1)
    jj = lax.broadcasted_iota(jnp.int32, (G, C, C), 2)
    decay = jnp.exp(jnp.where(ii >= jj, gcc3 - gcr3, -jnp.inf))
    eye = (ii == jj).astype(F32)
    nil = jnp.where(ii > jj, -_bdot_nt(kb3, k3) * decay, 0.0)
    inv = eye + nil
    powk = nil
    for _ in range(int(math.log2(C)) - 1):
        powk = _bdot(powk, powk)
        inv = _bdot(inv, eye + powk)
    rhs = jnp.concatenate([vb, kb * eg], axis=-1).reshape(G, C, 2 * D)
    uw = _bdot(inv, rhs)
    glast = gcr3[:, :, C - 1:C]
    u_ref[...] = uw[:, :, :D]
    wq_ref[:, 0:C, :] = uw[:, :, D:].astype(wq_ref.dtype)
    wq_ref[:, C:2 * C, :] = (q * eg).reshape(G, C, D).astype(wq_ref.dtype)
    kd_ref[...] = (k3 * jnp.exp(glast - gcc3)).astype(kd_ref.dtype)
    a_ref[...] = (_bdot_nt(q3, k3) * decay).astype(a_ref.dtype)
    c_ref[...] = jnp.broadcast_to(jnp.exp(glast), c_ref.shape)


def _dn_prep(p, beta, gc, gcrow, conv_w, l, B, T, tt):
    H = DN_HEADS
    C = DN_CHUNK
    N = T // C
    G = tt // C
    nt = T // tt
    hb = tt // SUBLANES
    qb, kb, vb = P_QKV // LANES, (P_QKV + DN_W) // LANES, (P_QKV + 2 * DN_W) // LANES
    tok = lambda off: pl.BlockSpec((tt, HEAD_DIM), lambda b, h, i: (b * nt + i, off + h))
    halo = lambda off: pl.BlockSpec((SUBLANES, HEAD_DIM),
                                    lambda b, h, i: (jnp.maximum((b * nt + i) * hb - 1, 0), off + h))
    gate = pl.BlockSpec((tt, LANES), lambda b, h, i: (b * nt + i, 0))
    cw = lambda off: pl.BlockSpec((None, DN_CONV, HEAD_DIM), lambda b, h, i: (l, 0, off + h))
    chunked = lambda r, c: pl.BlockSpec((None, None, G, r, c), lambda b, h, i: (b, h, i, 0, 0))
    return pl.pallas_call(
        functools.partial(_dn_prep_kernel, tiles_per_seq=nt),
        grid=(B, H, nt),
        in_specs=[tok(qb), tok(kb), tok(vb), halo(qb), halo(kb), halo(vb), gate, gate, chunked(1, C),
                  cw(0), cw(H), cw(2 * H)],
        out_specs=[chunked(2 * C, HEAD_DIM), chunked(C, HEAD_DIM), chunked(C, HEAD_DIM), chunked(C, C),
                   chunked(1, LANES)],
        out_shape=[jax.ShapeDtypeStruct((B, H, N, 2 * C, HEAD_DIM), BF16),
                   jax.ShapeDtypeStruct((B, H, N, C, HEAD_DIM), F32),
                   jax.ShapeDtypeStruct((B, H, N, C, HEAD_DIM), BF16),
                   jax.ShapeDtypeStruct((B, H, N, C, C), BF16),
                   jax.ShapeDtypeStruct((B, H, N, 1, LANES), F32)],
        compiler_params=_cparams(("parallel", "parallel", "parallel")),
        name="dn_prep",
    )(p, p, p, p, p, p, beta, gc, gcrow, conv_w, conv_w, conv_w)


def _dn_scan_kernel(wq_ref, u_ref, kd_ref, a_ref, c_ref, z_ref, nw_ref, y_ref, s_ref):
    B, H, G = wq_ref.shape[0], wq_ref.shape[1], wq_ref.shape[2]
    C = DN_CHUNK
    D = HEAD_DIM

    @pl.when(pl.program_id(0) == 0)
    def _():
        s_ref[...] = jnp.zeros_like(s_ref)

    nw = nw_ref[...]
    chains = [(b, h) for b in range(B) for h in range(H)]

    def body(n, carry):
        r0 = pl.multiple_of(n * C, C)
        S = [s_ref[b * H + h] for b, h in chains]
        m1 = [jnp.dot(wq_ref[b, h, n], S[i].astype(BF16), preferred_element_type=F32)
              for i, (b, h) in enumerate(chains)]
        v_new = [(u_ref[b, h, n] - m1[i][0:C]).astype(BF16) for i, (b, h) in enumerate(chains)]
        o = [m1[i][C:2 * C] + jnp.dot(a_ref[b, h, n], v_new[i], preferred_element_type=F32)
             for i, (b, h) in enumerate(chains)]
        kv = [lax.dot_general(kd_ref[b, h, n], v_new[i], (((0,), (0,)), ((), ())), preferred_element_type=F32)
              for i, (b, h) in enumerate(chains)]
        for i, (b, h) in enumerate(chains):
            s_ref[b * H + h] = S[i] * c_ref[b, h, n] + kv[i]
            zz = z_ref[b, pl.ds(r0, C), h * D:(h + 1) * D]
            y_ref[b, pl.ds(r0, C), h * D:(h + 1) * D] = (_rms(o[i], nw) * (zz * _sigmoid(zz))).astype(y_ref.dtype)
        return carry

    lax.fori_loop(0, G, body, 0)


def _dn_scan(wq, u, kd, a, c, p, norm_w, l, B, T, tt):
    H = DN_HEADS
    C = DN_CHUNK
    G = tt // C
    nt = T // tt
    chunked = lambda r, cc: pl.BlockSpec((B, H, G, r, cc), lambda i: (0, 0, i, 0, 0))
    y = pl.pallas_call(
        _dn_scan_kernel,
        grid=(nt,),
        in_specs=[chunked(2 * C, HEAD_DIM), chunked(C, HEAD_DIM), chunked(C, HEAD_DIM), chunked(C, C),
                  chunked(1, LANES),
                  pl.BlockSpec((B, tt, DN_W), lambda i: (0, i, P_Z // DN_W)),
                  pl.BlockSpec((None, 1, HEAD_DIM), lambda i: (l, 0, 0))],
        out_specs=pl.BlockSpec((B, tt, DN_W), lambda i: (0, i, 0)),
        out_shape=jax.ShapeDtypeStruct((B, T, DN_W), BF16),
        scratch_shapes=[pltpu.VMEM((B * H, HEAD_DIM, HEAD_DIM), F32)],
        compiler_params=_cparams(("arbitrary",)),
        name="dn_scan",
    )(wq, u, kd, a, c, p.reshape(B, T, p.shape[1]), norm_w)
    return y.reshape(B * T, DN_W)


def _dn_layer(p, conv_w, alog_rows, dt_rows, norm_w, l, B, T):
    N = T // DN_CHUNK
    tt = min(1024, T)
    beta, gc = _gates(p, alog_rows, dt_rows, l, B, T)
    gcrow = gc[:, DN_HEADS:2 * DN_HEADS].reshape(B, N, DN_CHUNK, DN_HEADS).transpose(0, 3, 1, 2)
    gcrow = gcrow.reshape(B, DN_HEADS, N, 1, DN_CHUNK)
    wq, u, kd, a, c = _dn_prep(p, beta, gc, gcrow, conv_w, l, B, T, tt)
    return _dn_scan(wq, u, kd, a, c, p, norm_w, l, B, T, tt // 2)


POOL_HALO = 16


def _pool_kernel(x_ref, halo_ref, w_ref, scale_ref, o_ref, *, tiles_per_seq):
    i = pl.program_id(0)
    tm = x_ref.shape[0]
    t0 = (i % tiles_per_seq) * tm
    first = (i % tiles_per_seq) == 0
    pos = t0 + 1 + lax.broadcasted_iota(jnp.int32, (tm, POOL_GROUP_DIM), 0)
    for gi, win in enumerate(POOL_WINDOWS):
        cols = slice(gi * POOL_GROUP_DIM, (gi + 1) * POOL_GROUP_DIM)
        halo = jnp.where(first, 0.0, halo_ref[:, cols])
        xg = x_ref[:, cols]
        ext = jnp.concatenate([halo, xg], axis=0)
        acc = ext
        s = 1
        while s < win:
            acc = acc + pltpu.roll(acc, s, axis=0)
            s *= 2
        cnt = jnp.minimum(pos, win).astype(F32)
        y = acc[POOL_HALO:] / cnt - xg
        y = jnp.dot(y.astype(BF16), w_ref[gi].astype(BF16), preferred_element_type=F32)
        o_ref[:, cols] = (y * scale_ref[:, cols]).astype(o_ref.dtype)


def _pool(p, pool_w, pool_scale, l, T, tm):
    M = p.shape[0]
    cb = P_POOL // POOL_W
    hb = tm // POOL_HALO
    return pl.pallas_call(
        functools.partial(_pool_kernel, tiles_per_seq=T // tm),
        grid=(M // tm,),
        in_specs=[pl.BlockSpec((tm, POOL_W), lambda i: (i, cb)),
                  pl.BlockSpec((POOL_HALO, POOL_W), lambda i: (jnp.maximum(i * hb - 1, 0), cb)),
                  pl.BlockSpec((None, POOL_GROUPS, POOL_GROUP_DIM, POOL_GROUP_DIM), lambda i: (l, 0, 0, 0)),
                  pl.BlockSpec((None, 1, POOL_W), lambda i: (l, 0, 0))],
        out_specs=pl.BlockSpec((tm, POOL_W), lambda i: (i, 0)),
        out_shape=jax.ShapeDtypeStruct((M, POOL_W), BF16),
        compiler_params=_cparams(("parallel",)),
        name="pool",
    )(p, p, pool_w, pool_scale)


def _swa_kernel(q_ref, k_ref, kp_ref, v_ref, vp_ref, cos_ref, sin_ref, cosp_ref, sinp_ref, sink_ref, o_ref,
                *, tiles_per_seq):
    i = pl.program_id(1)
    first = (i % tiles_per_seq) == 0
    tm = q_ref.shape[0]
    blk = SWA_BLOCK
    G = SWA_HEADS // SWA_KV_HEADS
    D = HEAD_DIM

    def rope(x, cos, sin):
        return x * cos + pltpu.roll(x, D // 2, axis=1) * sin

    cos = cos_ref[...]
    sin = sin_ref[...]
    cosx = jnp.concatenate([cosp_ref[...], cos], axis=0)
    sinx = jnp.concatenate([sinp_ref[...], sin], axis=0)
    kx = rope(jnp.concatenate([kp_ref[...], k_ref[...]], axis=0), cosx, sinx).astype(BF16)
    vx = jnp.concatenate([vp_ref[...], v_ref[...]], axis=0).astype(BF16)
    vx = jnp.concatenate([vx, jnp.ones_like(vx)], axis=1)
    scale = D ** -0.5
    qh = [(rope(q_ref[:, g * D:(g + 1) * D], cos, sin) * scale).astype(BF16) for g in range(G)]

    ri = lax.broadcasted_iota(jnp.int32, (G * blk, 2 * blk), 0)
    ii = ri % blk
    jj = lax.broadcasted_iota(jnp.int32, (G * blk, 2 * blk), 1)
    band = jnp.logical_or(jnp.logical_and(jj < blk, jj > ii), jnp.logical_and(jj >= blk, jj - blk <= ii))
    no_prev = jnp.where(first, blk, 0)
    band_first = jnp.logical_and(band, jj >= no_prev)
    sink = sink_ref[:, 0:1]
    for g in range(1, G):
        sink = jnp.where(ri[:, 0:1] >= g * blk, sink_ref[:, g:g + 1], sink)
    for b in range(tm // blk):
        qs = jnp.concatenate([q[b * blk:(b + 1) * blk] for q in qh], axis=0)
        keys = kx[b * blk:(b + 2) * blk]
        s = lax.dot_general(qs, keys, (((1,), (1,)), ((), ())), preferred_element_type=F32)
        s = jnp.where(band_first if b == 0 else band, s, -jnp.inf)
        m = jnp.maximum(jnp.max(s, axis=-1, keepdims=True), sink)
        e = jnp.exp(s - m).astype(BF16)
        pv = jnp.dot(e, vx[b * blk:(b + 2) * blk], preferred_element_type=F32)
        o = pv[:, 0:D] / (pv[:, D:2 * D] + jnp.exp(sink - m))
        for g in range(G):
            o_ref[b * blk:(b + 1) * blk, g * D:(g + 1) * D] = o[g * blk:(g + 1) * blk].astype(o_ref.dtype)


def _swa(p, cos, sin, sink_rows, l, T, tm):
    M = p.shape[0]
    blk = SWA_BLOCK
    r = tm // blk
    qb, kb, vb = P_SQ // SWA_GROUP_W, P_SK // HEAD_DIM, P_SV // HEAD_DIM
    prev = lambda i: jnp.maximum(i * r - 1, 0)
    return pl.pallas_call(
        functools.partial(_swa_kernel, tiles_per_seq=T // tm),
        grid=(SWA_KV_HEADS, M // tm),
        in_specs=[pl.BlockSpec((tm, SWA_GROUP_W), lambda kv, i: (i, qb + kv)),
                  pl.BlockSpec((tm, HEAD_DIM), lambda kv, i: (i, kb + kv)),
                  pl.BlockSpec((blk, HEAD_DIM), lambda kv, i: (prev(i), kb + kv)),
                  pl.BlockSpec((tm, HEAD_DIM), lambda kv, i: (i, vb + kv)),
                  pl.BlockSpec((blk, HEAD_DIM), lambda kv, i: (prev(i), vb + kv)),
                  pl.BlockSpec((tm, HEAD_DIM), lambda kv, i: (i, 0)),
                  pl.BlockSpec((tm, HEAD_DIM), lambda kv, i: (i, 0)),
                  pl.BlockSpec((blk, HEAD_DIM), lambda kv, i: (prev(i), 0)),
                  pl.BlockSpec((blk, HEAD_DIM), lambda kv, i: (prev(i), 0)),
                  pl.BlockSpec((None, None, 1, LANES), lambda kv, i: (l, kv, 0, 0))],
        out_specs=pl.BlockSpec((tm, SWA_GROUP_W), lambda kv, i: (i, kv)),
        out_shape=jax.ShapeDtypeStruct((M, SWA_W), BF16),
        compiler_params=_cparams(("parallel", "parallel")),
        name="swa",
    )(p, p, p, p, p, cos, sin, cos, sin, sink_rows)


OUT_ROWS = 256


def _outproj_kernel(ydn_ref, ypool_ref, yswa_ref, w_ref, x_ref, gpost_ref, gffn_ref, xo_ref, h_ref):
    for r0 in range(0, x_ref.shape[0], OUT_ROWS):
        rows = slice(r0, r0 + OUT_ROWS)
        mix = jnp.dot(ydn_ref[rows, :], w_ref[0:DN_W, :], preferred_element_type=F32)
        mix = mix + jnp.dot(ypool_ref[rows, :], w_ref[DN_W:DN_W + POOL_W, :], preferred_element_type=F32)
        mix = mix + jnp.dot(yswa_ref[rows, :], w_ref[DN_W + POOL_W:, :], preferred_element_type=F32)
        xn = x_ref[rows, :] + _rms(mix, gpost_ref[...])
        xo_ref[rows, :] = xn
        h_ref[rows, :] = _rms(xn, gffn_ref[...]).astype(h_ref.dtype)


def _out_proj(ydn, ypool, yswa, w_out, x2, gpost, gffn, l, tm):
    M, D = x2.shape
    row = lambda w: pl.BlockSpec((tm, w), lambda i: (i, 0))
    vec = pl.BlockSpec((None, 1, D), lambda i: (l, 0, 0))
    return pl.pallas_call(
        _outproj_kernel,
        grid=(M // tm,),
        in_specs=[row(DN_W), row(POOL_W), row(SWA_W),
                  pl.BlockSpec((None, w_out.shape[1], D), lambda i: (l, 0, 0)), row(D), vec, vec],
        out_specs=[row(D), row(D)],
        out_shape=[jax.ShapeDtypeStruct((M, D), F32), jax.ShapeDtypeStruct((M, D), BF16)],
        compiler_params=_cparams(("parallel",)),
        name="out_proj",
    )(ydn, ypool, yswa, w_out, x2, gpost, gffn)


FFN_ROWS = 256


def _ffn_up_kernel(h_ref, wa_ref, wb_ref, cwa_ref, cwb_ref, cba_ref, cbb_ref, g_ref, wa16, wb16, ua, ub,
                   *, tiles_per_seq):
    i = pl.program_id(1)
    tm = h_ref.shape[0]
    pad = SUBLANES

    @pl.when(i == 0)
    def _():
        wa16[...] = wa_ref[...].astype(BF16)
        wb16[...] = wb_ref[...].astype(BF16)

    @pl.when(i % tiles_per_seq == 0)
    def _():
        ua[0:pad, :] = jnp.zeros((pad, ua.shape[1]), F32)
        ub[0:pad, :] = jnp.zeros((pad, ub.shape[1]), F32)

    def conv(scr, cw_ref, cb_ref, r0):
        acc = cb_ref[...] + scr[pad + r0:pad + r0 + FFN_ROWS, :] * cw_ref[FFN_CONV - 1:FFN_CONV, :]
        for s in range(1, FFN_CONV):
            acc = acc + scr[pad + r0 - s:pad + r0 - s + FFN_ROWS, :] * cw_ref[FFN_CONV - 1 - s:FFN_CONV - s, :]
        return acc

    for r0 in range(0, tm, FFN_ROWS):
        hb = h_ref[r0:r0 + FFN_ROWS, :]
        ua[pad + r0:pad + r0 + FFN_ROWS, :] = jnp.dot(hb, wa16[...], preferred_element_type=F32)
        ub[pad + r0:pad + r0 + FFN_ROWS, :] = jnp.dot(hb, wb16[...], preferred_element_type=F32)
        a = conv(ua, cwa_ref, cba_ref, r0)
        b = conv(ub, cwb_ref, cbb_ref, r0)
        g_ref[r0:r0 + FFN_ROWS, :] = (a * _sigmoid(a) * b).astype(g_ref.dtype)
    ua[0:pad, :] = ua[tm:tm + pad, :]
    ub[0:pad, :] = ub[tm:tm + pad, :]


def _ffn_up(h, w_up, conv_w, conv_b, l, T, tm, tn):
    M, D = h.shape
    F = w_up.shape[2] // 2
    nj = F // tn
    return pl.pallas_call(
        functools.partial(_ffn_up_kernel, tiles_per_seq=T // tm),
        grid=(nj, M // tm),
        in_specs=[pl.BlockSpec((tm, D), lambda j, i: (i, 0)),
                  pl.BlockSpec((None, D, tn), lambda j, i: (l, 0, j)),
                  pl.BlockSpec((None, D, tn), lambda j, i: (l, 0, nj + j)),
                  pl.BlockSpec((None, FFN_CONV, tn), lambda j, i: (l, 0, j)),
                  pl.BlockSpec((None, FFN_CONV, tn), lambda j, i: (l, 0, nj + j)),
                  pl.BlockSpec((None, 1, tn), lambda j, i: (l, 0, j)),
                  pl.BlockSpec((None, 1, tn), lambda j, i: (l, 0, nj + j))],
        out_specs=pl.BlockSpec((tm, tn), lambda j, i: (i, j)),
        out_shape=jax.ShapeDtypeStruct((M, F), BF16),
        scratch_shapes=[pltpu.VMEM((D, tn), BF16)] * 2 + [pltpu.VMEM((tm + SUBLANES, tn), F32)] * 2,
        compiler_params=_cparams(("parallel", "arbitrary")),
        name="ffn_up",
    )(h, w_up, w_up, conv_w, conv_w, conv_b, conv_b)


def _ffn_down_kernel(g_ref, w_ref, x_ref, gpost_ref, gnext_ref, xo_ref, h_ref):
    f = jnp.dot(g_ref[...], w_ref[...], preferred_element_type=F32)
    xn = x_ref[...] + _rms(f, gpost_ref[...])
    xo_ref[...] = xn
    h_ref[...] = _rms(xn, gnext_ref[...]).astype(h_ref.dtype)


def _ffn_down(g, w_down, x2, gpost, gnext, l, lnext, tm):
    M, D = x2.shape
    F = g.shape[1]
    return pl.pallas_call(
        _ffn_down_kernel,
        grid=(M // tm,),
        in_specs=[pl.BlockSpec((tm, F), lambda i: (i, 0)),
                  pl.BlockSpec((None, F, D), lambda i: (l, 0, 0), pipeline_mode=pl.Buffered(1)),
                  pl.BlockSpec((tm, D), lambda i: (i, 0)),
                  pl.BlockSpec((None, 1, D), lambda i: (l, 0, 0)),
                  pl.BlockSpec((None, 1, D), lambda i: (lnext, 0, 0))],
        out_specs=[pl.BlockSpec((tm, D), lambda i: (i, 0))] * 2,
        out_shape=[jax.ShapeDtypeStruct((M, D), F32), jax.ShapeDtypeStruct((M, D), BF16)],
        compiler_params=_cparams(("parallel",)),
        name="ffn_down",
    )(g, w_down, x2, gpost, gnext)


def _rows(v):
    return v.astype(F32).reshape(v.shape[0], 1, v.shape[1])


def kernel(x, positions, norm_mix_pre, w_in, dn_conv_w, dn_a_log, dn_dt_bias, dn_norm_w, pool_w, pool_scale,
           swa_sinks, w_out, norm_mix_post, norm_ffn_pre, ffn_w_up, ffn_conv_w, ffn_conv_b, ffn_w_down,
           norm_ffn_post):
    B, T, D = x.shape
    depth = w_in.shape[0]
    M = B * T
    tm = min(512, T)
    G = SWA_HEADS // SWA_KV_HEADS
    x2 = x.reshape(M, D).astype(F32)

    w_in16 = _prep_w_in(w_in, 256)
    w_out16 = _cast_bf16(w_out, 512)
    w_down16 = _cast_bf16(ffn_w_down, 512)
    gate_pad = ((0, 0), (DN_HEADS, LANES - 2 * DN_HEADS))
    alog_rows = _rows(jnp.pad(dn_a_log, gate_pad))
    dt_rows = _rows(jnp.pad(dn_dt_bias, gate_pad))
    sink_rows = jnp.pad(swa_sinks.astype(F32).reshape(depth, SWA_KV_HEADS, 1, G),
                        ((0, 0), (0, 0), (0, 0), (0, LANES - G)))
    g_mix_pre, g_mix_post = _rows(norm_mix_pre), _rows(norm_mix_post)
    g_ffn_pre, g_ffn_post = _rows(norm_ffn_pre), _rows(norm_ffn_post)
    dn_nw, pscale, conv_b = _rows(dn_norm_w), _rows(pool_scale), _rows(ffn_conv_b)

    cos, sin = _rope_tables(positions, tm)
    h = _prenorm(x2, g_mix_pre, tm)
    for l in range(depth):
        p = _in_proj(h, w_in16, l, min(2 * tm, T), P_W // 3)
        y_dn = _dn_layer(p, dn_conv_w, alog_rows, dt_rows, dn_nw, l, B, T)
        y_pool = _pool(p, pool_w, pscale, l, T, tm)
        y_swa = _swa(p, cos, sin, sink_rows, l, T, tm)
        x2, h = _out_proj(y_dn, y_pool, y_swa, w_out16, x2, g_mix_post, g_ffn_pre, l, tm)
        g = _ffn_up(h, ffn_w_up, ffn_conv_w, conv_b, l, T, min(4 * FFN_ROWS, T), 512)
        x2, h = _ffn_down(g, w_down16, x2, g_ffn_post, g_mix_pre, l, min(l + 1, depth - 1), tm // 2)
    return x2.reshape(B, T, D).astype(x.dtype)
```

```python
import functools
import math

import numpy as np
import jax
import jax.numpy as jnp
from jax import lax
from jax.experimental import pallas as pl
from jax.experimental.pallas import tpu as pltpu

HEAD_DIM = 128
DN_HEADS = 6
DN_CONV = 4
DN_CHUNK = 64
POOL_WINDOWS = (2, 4, 8, 16)
POOL_GROUPS = 4
POOL_GROUP_DIM = 128
SWA_HEADS = 6
SWA_KV_HEADS = 2
SWA_WINDOW = 128
SWA_BLOCK = 128
ROPE_THETA = 10000.0
FFN_CONV = 3
NORM_EPS = 1e-6

DN_W = DN_HEADS * HEAD_DIM
POOL_W = POOL_GROUPS * POOL_GROUP_DIM
SWA_W = SWA_HEADS * HEAD_DIM
SWA_KV_W = SWA_KV_HEADS * HEAD_DIM
SWA_GROUP_W = SWA_W // SWA_KV_HEADS
LANES = 128
SUBLANES = 8

R_QKV = 0
R_Z = R_QKV + 3 * DN_W
R_B = R_Z + DN_W
R_POOL = R_B + 2 * DN_HEADS
R_SQ = R_POOL + POOL_W
R_SK = R_SQ + SWA_W
R_SV = R_SK + SWA_KV_W
R_W = R_SV + SWA_KV_W
P_Z = 0
P_SQ = P_Z + DN_W
P_POOL = P_SQ + SWA_W
P_SK = P_POOL + POOL_W
P_SV = P_SK + SWA_KV_W
P_BA = P_SV + SWA_KV_W
P_QKV = P_BA + LANES
P_W = P_QKV + 3 * DN_W
assert P_Z % DN_W == 0 and P_SQ % SWA_GROUP_W == 0 and P_POOL % POOL_W == 0
assert P_SK % HEAD_DIM == 0 and P_SV % HEAD_DIM == 0 and P_BA % LANES == 0 and P_QKV % LANES == 0

VMEM_LIMIT = 56 * 1024 * 1024

F32 = jnp.float32
BF16 = jnp.bfloat16


def _cparams(sem):
    return pltpu.CompilerParams(dimension_semantics=sem, vmem_limit_bytes=VMEM_LIMIT)


def _rms(v, gain):
    return v * lax.rsqrt(jnp.mean(v * v, axis=-1, keepdims=True) + NORM_EPS) * gain


def _sigmoid(v):
    return 1.0 / (1.0 + jnp.exp(-v))


def _bdot(a, b):
    return jnp.einsum("gik,gkj->gij", a.astype(BF16), b.astype(BF16), preferred_element_type=F32)


def _bdot_nt(a, b):
    return jnp.einsum("gik,gjk->gij", a.astype(BF16), b.astype(BF16), preferred_element_type=F32)


def _cast_kernel(w_ref, o_ref):
    o_ref[...] = w_ref[...].astype(o_ref.dtype)


def _cast_bf16(w, tr):
    L, R, C = w.shape
    return pl.pallas_call(
        _cast_kernel,
        grid=(L, R // tr),
        in_specs=[pl.BlockSpec((None, tr, C), lambda l, i: (l, i, 0))],
        out_specs=pl.BlockSpec((None, tr, C), lambda l, i: (l, i, 0)),
        out_shape=jax.ShapeDtypeStruct((L, R, C), BF16),
        compiler_params=_cparams(("parallel", "parallel")),
        name="cast_bf16",
    )(w)


def _win_kernel(src_ref, w_ref, o_ref):
    L = w_ref.shape[1]
    row = lax.broadcasted_iota(jnp.int32, (w_ref.shape[0], w_ref.shape[2]), 0)
    keep = jnp.logical_or(pl.program_id(0) != P_BA // LANES, row < 2 * DN_HEADS)
    for l in range(L):
        o_ref[l] = jnp.where(keep, w_ref[:, l, :], 0.0).astype(o_ref.dtype)


def _prep_w_in(w_in):
    L, K, _ = w_in.shape
    wt = jnp.transpose(w_in, (2, 0, 1))
    sections = ((P_Z, R_Z, DN_W), (P_SQ, R_SQ, SWA_W), (P_POOL, R_POOL, POOL_W), (P_SK, R_SK, SWA_KV_W),
                (P_SV, R_SV, SWA_KV_W), (P_BA, R_B, LANES), (P_QKV, R_QKV, 3 * DN_W))
    src = np.zeros((P_W // LANES,), np.int32)
    for p0, r0, width in sections:
        for k in range(width // LANES):
            src[p0 // LANES + k] = r0 + k * LANES
    return pl.pallas_call(
        _win_kernel,
        grid_spec=pltpu.PrefetchScalarGridSpec(
            num_scalar_prefetch=1,
            grid=(P_W // LANES,),
            in_specs=[pl.BlockSpec((pl.Element(LANES), pl.Element(L), pl.Element(K)),
                                   lambda c, src_ref: (src_ref[c], 0, 0))],
            out_specs=pl.BlockSpec((L, LANES, K), lambda c, src_ref: (0, c, 0)),
        ),
        out_shape=jax.ShapeDtypeStruct((L, P_W, K), BF16),
        compiler_params=_cparams(("parallel",)),
        name="prep_w_in",
    )(jnp.asarray(src), wt)


def _prenorm_kernel(x_ref, g_ref, h_ref):
    h_ref[...] = _rms(x_ref[...], g_ref[...]).astype(h_ref.dtype)


def _prenorm(x2, gains, tm):
    M, D = x2.shape
    return pl.pallas_call(
        _prenorm_kernel,
        grid=(M // tm,),
        in_specs=[pl.BlockSpec((tm, D), lambda i: (i, 0)), pl.BlockSpec((None, 1, D), lambda i: (0, 0, 0))],
        out_specs=pl.BlockSpec((tm, D), lambda i: (i, 0)),
        out_shape=jax.ShapeDtypeStruct((M, D), BF16),
        compiler_params=_cparams(("parallel",)),
        name="prenorm",
    )(x2, gains)


def _rope_kernel(pos_ref, freq_ref, cos_ref, sin_ref):
    ang = pos_ref[...] * freq_ref[...]
    lane = lax.broadcasted_iota(jnp.int32, ang.shape, 1)
    cos_ref[...] = jnp.cos(ang)
    sin_ref[...] = jnp.where(lane < HEAD_DIM // 2, -jnp.sin(ang), jnp.sin(ang))


def _rope_tables(positions, tm):
    M = positions.size
    pos = positions.astype(F32).reshape(M, 1)
    inv = (1.0 / (ROPE_THETA ** (np.arange(0, HEAD_DIM, 2, dtype=np.float32) / HEAD_DIM))).astype(np.float32)
    freq = jnp.asarray(np.concatenate([inv, inv])[None, :])
    return pl.pallas_call(
        _rope_kernel,
        grid=(M // tm,),
        in_specs=[pl.BlockSpec((tm, 1), lambda i: (i, 0)), pl.BlockSpec((1, HEAD_DIM), lambda i: (0, 0))],
        out_specs=[pl.BlockSpec((tm, HEAD_DIM), lambda i: (i, 0))] * 2,
        out_shape=[jax.ShapeDtypeStruct((M, HEAD_DIM), F32)] * 2,
        compiler_params=_cparams(("parallel",)),
        name="rope_tables",
    )(pos, freq)


def _mm_kernel(a_ref, wt_ref, o_ref):
    o_ref[...] = lax.dot_general(a_ref[...], wt_ref[...], (((1,), (1,)), ((), ())), preferred_element_type=F32)


def _in_proj(h, wt, l, tm, tn):
    M, K = h.shape
    N = wt.shape[1]
    return pl.pallas_call(
        _mm_kernel,
        grid=(N // tn, M // tm),
        in_specs=[pl.BlockSpec((tm, K), lambda j, i: (i, 0)), pl.BlockSpec((None, tn, K), lambda j, i: (l, j, 0))],
        out_specs=pl.BlockSpec((tm, tn), lambda j, i: (i, j)),
        out_shape=jax.ShapeDtypeStruct((M, N), F32),
        compiler_params=_cparams(("parallel", "parallel")),
        name="in_proj",
    )(h, wt)


def _gate_kernel(ba_ref, alog_ref, dt_ref, beta_ref, gc_ref):
    p = ba_ref[...]
    beta_ref[...] = _sigmoid(p)
    sp = p + dt_ref[...]
    softplus = jnp.maximum(sp, 0.0) + jnp.log(1.0 + jnp.exp(-jnp.abs(sp)))
    g = -jnp.exp(alog_ref[...]) * softplus
    row = lax.broadcasted_iota(jnp.int32, g.shape, 0) % DN_CHUNK
    s = 1
    while s < DN_CHUNK:
        g = g + jnp.where(row >= s, pltpu.roll(g, s, axis=0), 0.0)
        s *= 2
    gc_ref[...] = g


def _gates(p, alog_rows, dt_rows, l, B, T):
    M = p.shape[0]
    cb = P_BA // LANES
    vec = pl.BlockSpec((None, 1, LANES), lambda b: (l, 0, 0))
    return pl.pallas_call(
        _gate_kernel,
        grid=(B,),
        in_specs=[pl.BlockSpec((T, LANES), lambda b: (b, cb)), vec, vec],
        out_specs=[pl.BlockSpec((T, LANES), lambda b: (b, 0))] * 2,
        out_shape=[jax.ShapeDtypeStruct((M, LANES), F32)] * 2,
        compiler_params=_cparams(("parallel",)),
        name="dn_gates",
    )(p, alog_rows, dt_rows)


def _dn_prep_kernel(q_ref, k_ref, v_ref, qh_ref, kh_ref, vh_ref, beta_ref, gc_ref, gcrow_ref,
                    cwq_ref, cwk_ref, cwv_ref, wq_ref, u_ref, kd_ref, a_ref, c_ref, *, tiles_per_seq):
    h = pl.program_id(1)
    first = (pl.program_id(2) % tiles_per_seq) == 0
    tt = q_ref.shape[0]
    C = DN_CHUNK
    G = tt // C
    D = HEAD_DIM
    pad = SUBLANES

    def conv_silu(x_ref, halo_ref, cw_ref):
        halo = jnp.where(first, 0.0, halo_ref[...])
        ext = jnp.concatenate([halo, x_ref[...]], axis=0)
        acc = ext[pad:pad + tt] * cw_ref[DN_CONV - 1:DN_CONV, :]
        for s in range(1, DN_CONV):
            acc = acc + ext[pad - s:pad - s + tt] * cw_ref[DN_CONV - 1 - s:DN_CONV - s, :]
        return acc * _sigmoid(acc)

    q = conv_silu(q_ref, qh_ref, cwq_ref)
    k = conv_silu(k_ref, kh_ref, cwk_ref)
    v = conv_silu(v_ref, vh_ref, cwv_ref)
    q = q * lax.rsqrt(jnp.sum(q * q, axis=-1, keepdims=True) + NORM_EPS) * (D ** -0.5)
    k = k * lax.rsqrt(jnp.sum(k * k, axis=-1, keepdims=True) + NORM_EPS)
    lane = lax.broadcasted_iota(jnp.int32, (tt, LANES), 1)
    beta = jnp.sum(jnp.where(lane == h, beta_ref[...], 0.0), axis=-1, keepdims=True)
    gcc = jnp.sum(jnp.where(lane == h + DN_HEADS, gc_ref[...], 0.0), axis=-1, keepdims=True)
    eg = jnp.exp(gcc)
    kb = k * beta
    vb = v * beta

    q3 = q.reshape(G, C, D)
    k3 = k.reshape(G, C, D)
    kb3 = kb.reshape(G, C, D)
    gcc3 = gcc.reshape(G, C, 1)
    gcr3 = gcrow_ref[...]
    ii = lax.broadcasted_iota(jnp.int32, (G, C, C), 1)
    jj = lax.broadcasted_iota(jnp.int32, (G, C, C), 2)
    decay = jnp.exp(jnp.where(ii >= jj, gcc3 - gcr3, -jnp.inf))
    eye = (ii == jj).astype(F32)
    nil = jnp.where(ii > jj, -_bdot_nt(kb3, k3) * decay, 0.0)
    inv = eye + nil
    powk = nil
    for _ in range(int(math.log2(C)) - 1):
        powk = _bdot(powk, powk)
        inv = _bdot(inv, eye + powk)
    rhs = jnp.concatenate([vb, kb * eg], axis=-1).reshape(G, C, 2 * D)
    uw = _bdot(inv, rhs)
    glast = gcr3[:, :, C - 1:C]
    u_ref[...] = uw[:, :, :D]
    wq_ref[:, 0:C, :] = uw[:, :, D:].astype(wq_ref.dtype)
    wq_ref[:, C:2 * C, :] = (q * eg).reshape(G, C, D).astype(wq_ref.dtype)
    kd_ref[...] = (k3 * jnp.exp(glast - gcc3)).astype(kd_ref.dtype)
    a_ref[...] = (_bdot_nt(q3, k3) * decay).astype(a_ref.dtype)
    c_ref[...] = jnp.broadcast_to(jnp.exp(glast), c_ref.shape)


def _dn_prep(p, beta, gc, gcrow, conv_w, l, B, T, tt):
    H = DN_HEADS
    C = DN_CHUNK
    N = T // C
    G = tt // C
    nt = T // tt
    hb = tt // SUBLANES
    qb, kb, vb = P_QKV // LANES, (P_QKV + DN_W) // LANES, (P_QKV + 2 * DN_W) // LANES
    tok = lambda off: pl.BlockSpec((tt, HEAD_DIM), lambda b, h, i: (b * nt + i, off + h))
    halo = lambda off: pl.BlockSpec((SUBLANES, HEAD_DIM),
                                    lambda b, h, i: (jnp.maximum((b * nt + i) * hb - 1, 0), off + h))
    gate = pl.BlockSpec((tt, LANES), lambda b, h, i: (b * nt + i, 0))
    cw = lambda off: pl.BlockSpec((None, DN_CONV, HEAD_DIM), lambda b, h, i: (l, 0, off + h))
    chunked = lambda r, c: pl.BlockSpec((None, None, G, r, c), lambda b, h, i: (b, h, i, 0, 0))
    return pl.pallas_call(
        functools.partial(_dn_prep_kernel, tiles_per_seq=nt),
        grid=(B, H, nt),
        in_specs=[tok(qb), tok(kb), tok(vb), halo(qb), halo(kb), halo(vb), gate, gate, chunked(1, C),
                  cw(0), cw(H), cw(2 * H)],
        out_specs=[chunked(2 * C, HEAD_DIM), chunked(C, HEAD_DIM), chunked(C, HEAD_DIM), chunked(C, C),
                   chunked(1, LANES)],
        out_shape=[jax.ShapeDtypeStruct((B, H, N, 2 * C, HEAD_DIM), BF16),
                   jax.ShapeDtypeStruct((B, H, N, C, HEAD_DIM), F32),
                   jax.ShapeDtypeStruct((B, H, N, C, HEAD_DIM), BF16),
                   jax.ShapeDtypeStruct((B, H, N, C, C), BF16),
                   jax.ShapeDtypeStruct((B, H, N, 1, LANES), F32)],
        compiler_params=_cparams(("parallel", "parallel", "parallel")),
        name="dn_prep",
    )(p, p, p, p, p, p, beta, gc, gcrow, conv_w, conv_w, conv_w)


def _dn_scan_kernel(wq_ref, u_ref, kd_ref, a_ref, c_ref, z_ref, nw_ref, y_ref, s_ref):
    B, H, G = wq_ref.shape[0], wq_ref.shape[1], wq_ref.shape[2]
    C = DN_CHUNK
    D = HEAD_DIM

    @pl.when(pl.program_id(0) == 0)
    def _():
        s_ref[...] = jnp.zeros_like(s_ref)

    nw = nw_ref[...]
    chains = [(b, h) for b in range(B) for h in range(H)]

    def body(n, carry):
        r0 = pl.multiple_of(n * C, C)
        S = [s_ref[b * H + h] for b, h in chains]
        m1 = [jnp.dot(wq_ref[b, h, n], S[i].astype(BF16), preferred_element_type=F32)
              for i, (b, h) in enumerate(chains)]
        v_new = [(u_ref[b, h, n] - m1[i][0:C]).astype(BF16) for i, (b, h) in enumerate(chains)]
        o = [m1[i][C:2 * C] + jnp.dot(a_ref[b, h, n], v_new[i], preferred_element_type=F32)
             for i, (b, h) in enumerate(chains)]
        kv = [lax.dot_general(kd_ref[b, h, n], v_new[i], (((0,), (0,)), ((), ())), preferred_element_type=F32)
              for i, (b, h) in enumerate(chains)]
        for i, (b, h) in enumerate(chains):
            s_ref[b * H + h] = S[i] * c_ref[b, h, n] + kv[i]
            zz = z_ref[b, pl.ds(r0, C), h * D:(h + 1) * D]
            y_ref[b, pl.ds(r0, C), h * D:(h + 1) * D] = (_rms(o[i], nw) * (zz * _sigmoid(zz))).astype(y_ref.dtype)
        return carry

    lax.fori_loop(0, G, body, 0)


def _dn_scan(wq, u, kd, a, c, p, norm_w, l, B, T, tt):
    H = DN_HEADS
    C = DN_CHUNK
    G = tt // C
    nt = T // tt
    chunked = lambda r, cc: pl.BlockSpec((B, H, G, r, cc), lambda i: (0, 0, i, 0, 0))
    y = pl.pallas_call(
        _dn_scan_kernel,
        grid=(nt,),
        in_specs=[chunked(2 * C, HEAD_DIM), chunked(C, HEAD_DIM), chunked(C, HEAD_DIM), chunked(C, C),
                  chunked(1, LANES),
                  pl.BlockSpec((B, tt, DN_W), lambda i: (0, i, P_Z // DN_W)),
                  pl.BlockSpec((None, 1, HEAD_DIM), lambda i: (l, 0, 0))],
        out_specs=pl.BlockSpec((B, tt, DN_W), lambda i: (0, i, 0)),
        out_shape=jax.ShapeDtypeStruct((B, T, DN_W), BF16),
        scratch_shapes=[pltpu.VMEM((B * H, HEAD_DIM, HEAD_DIM), F32)],
        compiler_params=_cparams(("arbitrary",)),
        name="dn_scan",
    )(wq, u, kd, a, c, p.reshape(B, T, p.shape[1]), norm_w)
    return y.reshape(B * T, DN_W)


def _dn_layer(p, conv_w, alog_rows, dt_rows, norm_w, l, B, T):
    N = T // DN_CHUNK
    tt = min(1024, T)
    beta, gc = _gates(p, alog_rows, dt_rows, l, B, T)
    gcrow = gc[:, DN_HEADS:2 * DN_HEADS].reshape(B, N, DN_CHUNK, DN_HEADS).transpose(0, 3, 1, 2)
    gcrow = gcrow.reshape(B, DN_HEADS, N, 1, DN_CHUNK)
    wq, u, kd, a, c = _dn_prep(p, beta, gc, gcrow, conv_w, l, B, T, tt)
    return _dn_scan(wq, u, kd, a, c, p, norm_w, l, B, T, tt // 2)


POOL_HALO = 16


def _pool_kernel(x_ref, halo_ref, w_ref, scale_ref, o_ref, *, tiles_per_seq):
    i = pl.program_id(0)
    tm = x_ref.shape[0]
    t0 = (i % tiles_per_seq) * tm
    first = (i % tiles_per_seq) == 0
    pos = t0 + 1 + lax.broadcasted_iota(jnp.int32, (tm, POOL_GROUP_DIM), 0)
    for gi, win in enumerate(POOL_WINDOWS):
        cols = slice(gi * POOL_GROUP_DIM, (gi + 1) * POOL_GROUP_DIM)
        halo = jnp.where(first, 0.0, halo_ref[:, cols])
        xg = x_ref[:, cols]
        ext = jnp.concatenate([halo, xg], axis=0)
        acc = ext
        s = 1
        while s < win:
            acc = acc + pltpu.roll(acc, s, axis=0)
            s *= 2
        cnt = jnp.minimum(pos, win).astype(F32)
        y = acc[POOL_HALO:] / cnt - xg
        y = jnp.dot(y.astype(BF16), w_ref[gi].astype(BF16), preferred_element_type=F32)
        o_ref[:, cols] = (y * scale_ref[:, cols]).astype(o_ref.dtype)


def _pool(p, pool_w, pool_scale, l, T, tm):
    M = p.shape[0]
    cb = P_POOL // POOL_W
    hb = tm // POOL_HALO
    return pl.pallas_call(
        functools.partial(_pool_kernel, tiles_per_seq=T // tm),
        grid=(M // tm,),
        in_specs=[pl.BlockSpec((tm, POOL_W), lambda i: (i, cb)),
                  pl.BlockSpec((POOL_HALO, POOL_W), lambda i: (jnp.maximum(i * hb - 1, 0), cb)),
                  pl.BlockSpec((None, POOL_GROUPS, POOL_GROUP_DIM, POOL_GROUP_DIM), lambda i: (l, 0, 0, 0)),
                  pl.BlockSpec((None, 1, POOL_W), lambda i: (l, 0, 0))],
        out_specs=pl.BlockSpec((tm, POOL_W), lambda i: (i, 0)),
        out_shape=jax.ShapeDtypeStruct((M, POOL_W), BF16),
        compiler_params=_cparams(("parallel",)),
        name="pool",
    )(p, p, pool_w, pool_scale)


def _swa_kernel(q_ref, k_ref, kp_ref, v_ref, vp_ref, cos_ref, sin_ref, cosp_ref, sinp_ref, sink_ref, o_ref,
                *, tiles_per_seq):
    i = pl.program_id(1)
    first = (i % tiles_per_seq) == 0
    tm = q_ref.shape[0]
    blk = SWA_BLOCK
    G = SWA_HEADS // SWA_KV_HEADS
    D = HEAD_DIM

    def rope(x, cos, sin):
        return x * cos + pltpu.roll(x, D // 2, axis=1) * sin

    cos = cos_ref[...]
    sin = sin_ref[...]
    cosx = jnp.concatenate([cosp_ref[...], cos], axis=0)
    sinx = jnp.concatenate([sinp_ref[...], sin], axis=0)
    kx = rope(jnp.concatenate([kp_ref[...], k_ref[...]], axis=0), cosx, sinx).astype(BF16)
    vx = jnp.concatenate([vp_ref[...], v_ref[...]], axis=0).astype(BF16)
    vx = jnp.concatenate([vx, jnp.ones_like(vx)], axis=1)
    scale = D ** -0.5
    qh = [(rope(q_ref[:, g * D:(g + 1) * D], cos, sin) * scale).astype(BF16) for g in range(G)]

    ri = lax.broadcasted_iota(jnp.int32, (G * blk, 2 * blk), 0)
    ii = ri % blk
    jj = lax.broadcasted_iota(jnp.int32, (G * blk, 2 * blk), 1)
    band = jnp.logical_or(jnp.logical_and(jj < blk, jj > ii), jnp.logical_and(jj >= blk, jj - blk <= ii))
    no_prev = jnp.where(first, blk, 0)
    band_first = jnp.logical_and(band, jj >= no_prev)
    sink = sink_ref[:, 0:1]
    for g in range(1, G):
        sink = jnp.where(ri[:, 0:1] >= g * blk, sink_ref[:, g:g + 1], sink)
    for b in range(tm // blk):
        qs = jnp.concatenate([q[b * blk:(b + 1) * blk] for q in qh], axis=0)
        keys = kx[b * blk:(b + 2) * blk]
        s = lax.dot_general(qs, keys, (((1,), (1,)), ((), ())), preferred_element_type=F32)
        s = jnp.where(band_first if b == 0 else band, s, -jnp.inf)
        m = jnp.maximum(jnp.max(s, axis=-1, keepdims=True), sink)
        e = jnp.exp(s - m).astype(BF16)
        pv = jnp.dot(e, vx[b * blk:(b + 2) * blk], preferred_element_type=F32)
        o = pv[:, 0:D] / (pv[:, D:2 * D] + jnp.exp(sink - m))
        for g in range(G):
            o_ref[b * blk:(b + 1) * blk, g * D:(g + 1) * D] = o[g * blk:(g + 1) * blk].astype(o_ref.dtype)


def _swa(p, cos, sin, sink_rows, l, T, tm):
    M = p.shape[0]
    blk = SWA_BLOCK
    r = tm // blk
    qb, kb, vb = P_SQ // SWA_GROUP_W, P_SK // HEAD_DIM, P_SV // HEAD_DIM
    prev = lambda i: jnp.maximum(i * r - 1, 0)
    return pl.pallas_call(
        functools.partial(_swa_kernel, tiles_per_seq=T // tm),
        grid=(SWA_KV_HEADS, M // tm),
        in_specs=[pl.BlockSpec((tm, SWA_GROUP_W), lambda kv, i: (i, qb + kv)),
                  pl.BlockSpec((tm, HEAD_DIM), lambda kv, i: (i, kb + kv)),
                  pl.BlockSpec((blk, HEAD_DIM), lambda kv, i: (prev(i), kb + kv)),
                  pl.BlockSpec((tm, HEAD_DIM), lambda kv, i: (i, vb + kv)),
                  pl.BlockSpec((blk, HEAD_DIM), lambda kv, i: (prev(i), vb + kv)),
                  pl.BlockSpec((tm, HEAD_DIM), lambda kv, i: (i, 0)),
                  pl.BlockSpec((tm, HEAD_DIM), lambda kv, i: (i, 0)),
                  pl.BlockSpec((blk, HEAD_DIM), lambda kv, i: (prev(i), 0)),
                  pl.BlockSpec((blk, HEAD_DIM), lambda kv, i: (prev(i), 0)),
                  pl.BlockSpec((None, None, 1, LANES), lambda kv, i: (l, kv, 0, 0))],
        out_specs=pl.BlockSpec((tm, SWA_GROUP_W), lambda kv, i: (i, kv)),
        out_shape=jax.ShapeDtypeStruct((M, SWA_W), BF16),
        compiler_params=_cparams(("parallel", "parallel")),
        name="swa",
    )(p, p, p, p, p, cos, sin, cos, sin, sink_rows)


OUT_ROWS = 256


def _outproj_kernel(ydn_ref, ypool_ref, yswa_ref, w_ref, x_ref, gpost_ref, gffn_ref, xo_ref, h_ref):
    for r0 in range(0, x_ref.shape[0], OUT_ROWS):
        rows = slice(r0, r0 + OUT_ROWS)
        mix = jnp.dot(ydn_ref[rows, :], w_ref[0:DN_W, :], preferred_element_type=F32)
        mix = mix + jnp.dot(ypool_ref[rows, :], w_ref[DN_W:DN_W + POOL_W, :], preferred_element_type=F32)
        mix = mix + jnp.dot(yswa_ref[rows, :], w_ref[DN_W + POOL_W:, :], preferred_element_type=F32)
        xn = x_ref[rows, :] + _rms(mix, gpost_ref[...])
        xo_ref[rows, :] = xn
        h_ref[rows, :] = _rms(xn, gffn_ref[...]).astype(h_ref.dtype)


def _out_proj(ydn, ypool, yswa, w_out, x2, gpost, gffn, l, tm):
    M, D = x2.shape
    row = lambda w: pl.BlockSpec((tm, w), lambda i: (i, 0))
    vec = pl.BlockSpec((None, 1, D), lambda i: (l, 0, 0))
    return pl.pallas_call(
        _outproj_kernel,
        grid=(M // tm,),
        in_specs=[row(DN_W), row(POOL_W), row(SWA_W),
                  pl.BlockSpec((None, w_out.shape[1], D), lambda i: (l, 0, 0)), row(D), vec, vec],
        out_specs=[row(D), row(D)],
        out_shape=[jax.ShapeDtypeStruct((M, D), F32), jax.ShapeDtypeStruct((M, D), BF16)],
        compiler_params=_cparams(("parallel",)),
        name="out_proj",
    )(ydn, ypool, yswa, w_out, x2, gpost, gffn)


FFN_ROWS = 256


def _ffn_up_kernel(h_ref, wa_ref, wb_ref, cwa_ref, cwb_ref, cba_ref, cbb_ref, g_ref, wa16, wb16, ua, ub,
                   *, tiles_per_seq):
    i = pl.program_id(1)
    tm = h_ref.shape[0]
    pad = SUBLANES

    @pl.when(i == 0)
    def _():
        wa16[...] = wa_ref[...].astype(BF16)
        wb16[...] = wb_ref[...].astype(BF16)

    @pl.when(i % tiles_per_seq == 0)
    def _():
        ua[...] = jnp.zeros_like(ua)
        ub[...] = jnp.zeros_like(ub)

    def conv(u, tail_ref, cw_ref, cb_ref):
        ext = jnp.concatenate([tail_ref[...], u], axis=0)
        tail_ref[...] = u[FFN_ROWS - pad:FFN_ROWS]
        acc = cb_ref[...] + u * cw_ref[FFN_CONV - 1:FFN_CONV, :]
        for s in range(1, FFN_CONV):
            acc = acc + ext[pad - s:pad - s + FFN_ROWS] * cw_ref[FFN_CONV - 1 - s:FFN_CONV - s, :]
        return acc

    for r0 in range(0, tm, FFN_ROWS):
        hb = h_ref[r0:r0 + FFN_ROWS, :]
        a = conv(jnp.dot(hb, wa16[...], preferred_element_type=F32), ua, cwa_ref, cba_ref)
        b = conv(jnp.dot(hb, wb16[...], preferred_element_type=F32), ub, cwb_ref, cbb_ref)
        g_ref[r0:r0 + FFN_ROWS, :] = (a * _sigmoid(a) * b).astype(g_ref.dtype)


def _ffn_up(h, w_up, conv_w, conv_b, l, T, tm, tn):
    M, D = h.shape
    F = w_up.shape[2] // 2
    nj = F // tn
    return pl.pallas_call(
        functools.partial(_ffn_up_kernel, tiles_per_seq=T // tm),
        grid=(nj, M // tm),
        in_specs=[pl.BlockSpec((tm, D), lambda j, i: (i, 0)),
                  pl.BlockSpec((None, D, tn), lambda j, i: (l, 0, j)),
                  pl.BlockSpec((None, D, tn), lambda j, i: (l, 0, nj + j)),
                  pl.BlockSpec((None, FFN_CONV, tn), lambda j, i: (l, 0, j)),
                  pl.BlockSpec((None, FFN_CONV, tn), lambda j, i: (l, 0, nj + j)),
                  pl.BlockSpec((None, 1, tn), lambda j, i: (l, 0, j)),
                  pl.BlockSpec((None, 1, tn), lambda j, i: (l, 0, nj + j))],
        out_specs=pl.BlockSpec((tm, tn), lambda j, i: (i, j)),
        out_shape=jax.ShapeDtypeStruct((M, F), BF16),
        scratch_shapes=[pltpu.VMEM((D, tn), BF16)] * 2 + [pltpu.VMEM((SUBLANES, tn), F32)] * 2,
        compiler_params=_cparams(("parallel", "arbitrary")),
        name="ffn_up",
    )(h, w_up, w_up, conv_w, conv_w, conv_b, conv_b)


def _ffn_down_kernel(g_ref, w_ref, x_ref, gpost_ref, gnext_ref, xo_ref, h_ref):
    f = jnp.dot(g_ref[...], w_ref[...], preferred_element_type=F32)
    xn = x_ref[...] + _rms(f, gpost_ref[...])
    xo_ref[...] = xn
    h_ref[...] = _rms(xn, gnext_ref[...]).astype(h_ref.dtype)


def _ffn_down(g, w_down, x2, gpost, gnext, l, lnext, tm):
    M, D = x2.shape
    F = g.shape[1]
    return pl.pallas_call(
        _ffn_down_kernel,
        grid=(M // tm,),
        in_specs=[pl.BlockSpec((tm, F), lambda i: (i, 0)),
                  pl.BlockSpec((None, F, D), lambda i: (l, 0, 0), pipeline_mode=pl.Buffered(1)),
                  pl.BlockSpec((tm, D), lambda i: (i, 0)),
                  pl.BlockSpec((None, 1, D), lambda i: (l, 0, 0)),
                  pl.BlockSpec((None, 1, D), lambda i: (lnext, 0, 0))],
        out_specs=[pl.BlockSpec((tm, D), lambda i: (i, 0))] * 2,
        out_shape=[jax.ShapeDtypeStruct((M, D), F32), jax.ShapeDtypeStruct((M, D), BF16)],
        compiler_params=_cparams(("parallel",)),
        name="ffn_down",
    )(g, w_down, x2, gpost, gnext)


def _rows(v):
    return v.astype(F32).reshape(v.shape[0], 1, v.shape[1])


def kernel(x, positions, norm_mix_pre, w_in, dn_conv_w, dn_a_log, dn_dt_bias, dn_norm_w, pool_w, pool_scale,
           swa_sinks, w_out, norm_mix_post, norm_ffn_pre, ffn_w_up, ffn_conv_w, ffn_conv_b, ffn_w_down,
           norm_ffn_post):
    B, T, D = x.shape
    depth = w_in.shape[0]
    M = B * T
    tm = min(512, T)
    G = SWA_HEADS // SWA_KV_HEADS
    x2 = x.reshape(M, D).astype(F32)

    w_in16 = _prep_w_in(w_in)
    w_out16 = _cast_bf16(w_out, 512)
    w_down16 = _cast_bf16(ffn_w_down, 512)
    gate_pad = ((0, 0), (DN_HEADS, LANES - 2 * DN_HEADS))
    alog_rows = _rows(jnp.pad(dn_a_log, gate_pad))
    dt_rows = _rows(jnp.pad(dn_dt_bias, gate_pad))
    sink_rows = jnp.pad(swa_sinks.astype(F32).reshape(depth, SWA_KV_HEADS, 1, G),
                        ((0, 0), (0, 0), (0, 0), (0, LANES - G)))
    g_mix_pre, g_mix_post = _rows(norm_mix_pre), _rows(norm_mix_post)
    g_ffn_pre, g_ffn_post = _rows(norm_ffn_pre), _rows(norm_ffn_post)
    dn_nw, pscale, conv_b = _rows(dn_norm_w), _rows(pool_scale), _rows(ffn_conv_b)

    cos, sin = _rope_tables(positions, tm)
    h = _prenorm(x2, g_mix_pre, tm)
    for l in range(depth):
        p = _in_proj(h, w_in16, l, min(2 * tm, T), P_W // 3)
        y_dn = _dn_layer(p, dn_conv_w, alog_rows, dt_rows, dn_nw, l, B, T)
        y_pool = _pool(p, pool_w, pscale, l, T, tm)
        y_swa = _swa(p, cos, sin, sink_rows, l, T, tm)
        x2, h = _out_proj(y_dn, y_pool, y_swa, w_out16, x2, g_mix_post, g_ffn_pre, l, tm)
        g = _ffn_up(h, ffn_w_up, ffn_conv_w, conv_b, l, T, min(4 * FFN_ROWS, T), 512)
        x2, h = _ffn_down(g, w_down16, x2, g_ffn_post, g_mix_pre, l, min(l + 1, depth - 1), tm // 2)
    return x2.reshape(B, T, D).astype(x.dtype)
```

```python
import functools
import math

import numpy as np
import jax
import jax.numpy as jnp
from jax import lax
from jax.experimental import pallas as pl
from jax.experimental.pallas import tpu as pltpu

HEAD_DIM = 128
DN_HEADS = 6
DN_CONV = 4
DN_CHUNK = 64
POOL_WINDOWS = (2, 4, 8, 16)
POOL_GROUPS = 4
POOL_GROUP_DIM = 128
SWA_HEADS = 6
SWA_KV_HEADS = 2
SWA_WINDOW = 128
SWA_BLOCK = 128
ROPE_THETA = 10000.0
FFN_CONV = 3
NORM_EPS = 1e-6

DN_W = DN_HEADS * HEAD_DIM
POOL_W = POOL_GROUPS * POOL_GROUP_DIM
SWA_W = SWA_HEADS * HEAD_DIM
SWA_KV_W = SWA_KV_HEADS * HEAD_DIM
SWA_GROUP_W = SWA_W // SWA_KV_HEADS
LANES = 128
SUBLANES = 8

R_QKV = 0
R_Z = R_QKV + 3 * DN_W
R_B = R_Z + DN_W
R_POOL = R_B + 2 * DN_HEADS
R_SQ = R_POOL + POOL_W
R_SK = R_SQ + SWA_W
R_SV = R_SK + SWA_KV_W
R_W = R_SV + SWA_KV_W
P_Z = 0
P_SQ = P_Z + DN_W
P_POOL = P_SQ + SWA_W
P_SK = P_POOL + POOL_W
P_SV = P_SK + SWA_KV_W
P_BA = P_SV + SWA_KV_W
P_QKV = P_BA + LANES
P_W = P_QKV + 3 * DN_W
assert P_Z % DN_W == 0 and P_SQ % SWA_GROUP_W == 0 and P_POOL % POOL_W == 0
assert P_SK % HEAD_DIM == 0 and P_SV % HEAD_DIM == 0 and P_BA % LANES == 0 and P_QKV % LANES == 0

VMEM_LIMIT = 56 * 1024 * 1024

F32 = jnp.float32
BF16 = jnp.bfloat16


def _cparams(sem):
    return pltpu.CompilerParams(dimension_semantics=sem, vmem_limit_bytes=VMEM_LIMIT)


def _rms(v, gain):
    return v * lax.rsqrt(jnp.mean(v * v, axis=-1, keepdims=True) + NORM_EPS) * gain


def _sigmoid(v):
    return 1.0 / (1.0 + jnp.exp(-v))


def _bdot(a, b):
    return jnp.einsum("gik,gkj->gij", a.astype(BF16), b.astype(BF16), preferred_element_type=F32)


def _bdot_nt(a, b):
    return jnp.einsum("gik,gjk->gij", a.astype(BF16), b.astype(BF16), preferred_element_type=F32)


def _win_kernel(src_ref, w_ref, o_ref):
    L = w_ref.shape[1]
    row = lax.broadcasted_iota(jnp.int32, (w_ref.shape[0], w_ref.shape[2]), 0)
    keep = jnp.logical_or(pl.program_id(0) != P_BA // LANES, row < 2 * DN_HEADS)
    for l in range(L):
        o_ref[l] = jnp.where(keep, w_ref[:, l, :], 0.0).astype(o_ref.dtype)


def _prep_w_in(w_in):
    L, K, _ = w_in.shape
    wt = jnp.transpose(w_in, (2, 0, 1))
    sections = ((P_Z, R_Z, DN_W), (P_SQ, R_SQ, SWA_W), (P_POOL, R_POOL, POOL_W), (P_SK, R_SK, SWA_KV_W),
                (P_SV, R_SV, SWA_KV_W), (P_BA, R_B, LANES), (P_QKV, R_QKV, 3 * DN_W))
    src = np.zeros((P_W // LANES,), np.int32)
    for p0, r0, width in sections:
        for k in range(width // LANES):
            src[p0 // LANES + k] = r0 + k * LANES
    return pl.pallas_call(
        _win_kernel,
        grid_spec=pltpu.PrefetchScalarGridSpec(
            num_scalar_prefetch=1,
            grid=(P_W // LANES,),
            in_specs=[pl.BlockSpec((pl.Element(LANES), pl.Element(L), pl.Element(K)),
                                   lambda c, src_ref: (src_ref[c], 0, 0))],
            out_specs=pl.BlockSpec((L, LANES, K), lambda c, src_ref: (0, c, 0)),
        ),
        out_shape=jax.ShapeDtypeStruct((L, P_W, K), BF16),
        compiler_params=_cparams(("parallel",)),
        name="prep_w_in",
    )(jnp.asarray(src), wt)


def _prenorm_kernel(x_ref, g_ref, h_ref):
    h_ref[...] = _rms(x_ref[...], g_ref[...]).astype(h_ref.dtype)


def _prenorm(x2, gains, tm):
    M, D = x2.shape
    return pl.pallas_call(
        _prenorm_kernel,
        grid=(M // tm,),
        in_specs=[pl.BlockSpec((tm, D), lambda i: (i, 0)), pl.BlockSpec((None, 1, D), lambda i: (0, 0, 0))],
        out_specs=pl.BlockSpec((tm, D), lambda i: (i, 0)),
        out_shape=jax.ShapeDtypeStruct((M, D), BF16),
        compiler_params=_cparams(("parallel",)),
        name="prenorm",
    )(x2, gains)


def _rope_kernel(pos_ref, freq_ref, cos_ref, sin_ref):
    ang = pos_ref[...] * freq_ref[...]
    lane = lax.broadcasted_iota(jnp.int32, ang.shape, 1)
    cos_ref[...] = jnp.cos(ang)
    sin_ref[...] = jnp.where(lane < HEAD_DIM // 2, -jnp.sin(ang), jnp.sin(ang))


def _rope_tables(positions, tm):
    M = positions.size
    pos = positions.astype(F32).reshape(M, 1)
    inv = (1.0 / (ROPE_THETA ** (np.arange(0, HEAD_DIM, 2, dtype=np.float32) / HEAD_DIM))).astype(np.float32)
    freq = jnp.asarray(np.concatenate([inv, inv])[None, :])
    return pl.pallas_call(
        _rope_kernel,
        grid=(M // tm,),
        in_specs=[pl.BlockSpec((tm, 1), lambda i: (i, 0)), pl.BlockSpec((1, HEAD_DIM), lambda i: (0, 0))],
        out_specs=[pl.BlockSpec((tm, HEAD_DIM), lambda i: (i, 0))] * 2,
        out_shape=[jax.ShapeDtypeStruct((M, HEAD_DIM), F32)] * 2,
        compiler_params=_cparams(("parallel",)),
        name="rope_tables",
    )(pos, freq)


def _mm_kernel(a_ref, wt_ref, wo_ref, wd_ref, o_ref, wo16_ref, wd16_ref):
    o_ref[...] = lax.dot_general(a_ref[...], wt_ref[...], (((1,), (1,)), ((), ())), preferred_element_type=F32)
    wo16_ref[...] = wo_ref[...].astype(wo16_ref.dtype)
    wd16_ref[...] = wd_ref[...].astype(wd16_ref.dtype)


def _cast_slices(rows, steps):
    return max(n for n in range(1, steps + 1) if rows % n == 0 and (rows // n) % (2 * SUBLANES) == 0)


def _in_proj(h, wt, w_out, w_down, l, tm, tn):
    M, K = h.shape
    N = wt.shape[1]
    nj, nt = N // tn, M // tm
    specs, shapes = [], []
    for w in (w_out, w_down):
        rows, cols = w.shape[1], w.shape[2]
        n = _cast_slices(rows, nj * nt)
        idx = lambda j, i, n=n: jnp.minimum(j * nt + i, n - 1)
        specs.append((pl.BlockSpec((None, rows // n, cols), lambda j, i, idx=idx: (l, idx(j, i), 0)),
                      pl.BlockSpec((rows // n, cols), lambda j, i, idx=idx: (idx(j, i), 0))))
        shapes.append(jax.ShapeDtypeStruct((rows, cols), BF16))
    return pl.pallas_call(
        _mm_kernel,
        grid=(nj, nt),
        in_specs=[pl.BlockSpec((tm, K), lambda j, i: (i, 0)), pl.BlockSpec((None, tn, K), lambda j, i: (l, j, 0)),
                  specs[0][0], specs[1][0]],
        out_specs=[pl.BlockSpec((tm, tn), lambda j, i: (i, j)), specs[0][1], specs[1][1]],
        out_shape=[jax.ShapeDtypeStruct((M, N), F32)] + shapes,
        compiler_params=_cparams(("arbitrary", "arbitrary")),
        name="in_proj",
    )(h, wt, w_out, w_down)


def _gate_kernel(ba_ref, alog_ref, dt_ref, beta_ref, gc_ref):
    p = ba_ref[...]
    beta_ref[...] = _sigmoid(p)
    sp = p + dt_ref[...]
    softplus = jnp.maximum(sp, 0.0) + jnp.log(1.0 + jnp.exp(-jnp.abs(sp)))
    g = -jnp.exp(alog_ref[...]) * softplus
    row = lax.broadcasted_iota(jnp.int32, g.shape, 0) % DN_CHUNK
    s = 1
    while s < DN_CHUNK:
        g = g + jnp.where(row >= s, pltpu.roll(g, s, axis=0), 0.0)
        s *= 2
    gc_ref[...] = g


def _gates(p, alog_rows, dt_rows, l, B, T):
    M = p.shape[0]
    cb = P_BA // LANES
    vec = pl.BlockSpec((None, 1, LANES), lambda b: (l, 0, 0))
    return pl.pallas_call(
        _gate_kernel,
        grid=(B,),
        in_specs=[pl.BlockSpec((T, LANES), lambda b: (b, cb)), vec, vec],
        out_specs=[pl.BlockSpec((T, LANES), lambda b: (b, 0))] * 2,
        out_shape=[jax.ShapeDtypeStruct((M, LANES), F32)] * 2,
        compiler_params=_cparams(("parallel",)),
        name="dn_gates",
    )(p, alog_rows, dt_rows)


def _dn_prep_kernel(q_ref, k_ref, v_ref, qh_ref, kh_ref, vh_ref, beta_ref, gc_ref, gcrow_ref,
                    cwq_ref, cwk_ref, cwv_ref, wq_ref, u_ref, kd_ref, a_ref, c_ref, *, tiles_per_seq):
    h = pl.program_id(1)
    first = (pl.program_id(2) % tiles_per_seq) == 0
    tt = q_ref.shape[0]
    C = DN_CHUNK
    G = tt // C
    D = HEAD_DIM
    pad = SUBLANES

    def conv_silu(x_ref, halo_ref, cw_ref):
        halo = jnp.where(first, 0.0, halo_ref[...])
        ext = jnp.concatenate([halo, x_ref[...]], axis=0)
        c0, c1, c2, c3 = (cw_ref[j:j + 1, :] for j in range(DN_CONV))
        back1 = pltpu.roll(ext, 1, axis=0)
        z = ext * c1 + back1 * c0
        acc = (ext * c3 + back1 * c2 + pltpu.roll(z, 2, axis=0))[pad:pad + tt]
        return acc * _sigmoid(acc)

    q = conv_silu(q_ref, qh_ref, cwq_ref)
    k = conv_silu(k_ref, kh_ref, cwk_ref)
    v = conv_silu(v_ref, vh_ref, cwv_ref)
    q = q * lax.rsqrt(jnp.sum(q * q, axis=-1, keepdims=True) + NORM_EPS) * (D ** -0.5)
    k = k * lax.rsqrt(jnp.sum(k * k, axis=-1, keepdims=True) + NORM_EPS)
    lane = lax.broadcasted_iota(jnp.int32, (tt, LANES), 1)
    beta = jnp.sum(jnp.where(lane == h, beta_ref[...], 0.0), axis=-1, keepdims=True)
    gcc = jnp.sum(jnp.where(lane == h + DN_HEADS, gc_ref[...], 0.0), axis=-1, keepdims=True)
    eg = jnp.exp(gcc)
    kb = k * beta
    vb = v * beta

    q3 = q.reshape(G, C, D)
    k3 = k.reshape(G, C, D)
    kb3 = kb.reshape(G, C, D)
    gcc3 = gcc.reshape(G, C, 1)
    gcr3 = gcrow_ref[...]
    ii = lax.broadcasted_iota(jnp.int32, (G, C, C), 1)
    jj = lax.broadcasted_iota(jnp.int32, (G, C, C), 2)
    decay = jnp.exp(jnp.where(ii >= jj, gcc3 - gcr3, -jnp.inf))
    eye = (ii == jj).astype(F32)
    kq = _bdot_nt(jnp.concatenate([kb3, q3], axis=1), k3)
    nil = jnp.where(ii > jj, -kq[:, 0:C] * decay, 0.0)
    inv = eye + nil
    powk = _bdot(nil, nil)
    for _ in range(int(math.log2(C)) - 2):
        both = _bdot(jnp.concatenate([inv, powk], axis=1), powk)
        inv = inv + both[:, 0:C]
        powk = both[:, C:2 * C]
    inv = inv + _bdot(inv, powk)
    rhs = jnp.concatenate([vb, kb * eg], axis=-1).reshape(G, C, 2 * D)
    uw = _bdot(inv, rhs)
    glast = gcr3[:, :, C - 1:C]
    u_ref[...] = uw[:, :, :D]
    wq_ref[:, 0:C, :] = uw[:, :, D:].astype(wq_ref.dtype)
    wq_ref[:, C:2 * C, :] = (q * eg).reshape(G, C, D).astype(wq_ref.dtype)
    kd_ref[...] = (k3 * jnp.exp(glast - gcc3)).astype(kd_ref.dtype)
    a_ref[...] = (kq[:, C:2 * C] * decay).astype(a_ref.dtype)
    c_ref[...] = jnp.broadcast_to(jnp.exp(glast), c_ref.shape)


def _dn_prep(p, beta, gc, gcrow, conv_w, l, B, T, tt):
    H = DN_HEADS
    C = DN_CHUNK
    N = T // C
    G = tt // C
    nt = T // tt
    hb = tt // SUBLANES
    qb, kb, vb = P_QKV // LANES, (P_QKV + DN_W) // LANES, (P_QKV + 2 * DN_W) // LANES
    tok = lambda off: pl.BlockSpec((tt, HEAD_DIM), lambda b, h, i: (b * nt + i, off + h))
    halo = lambda off: pl.BlockSpec((SUBLANES, HEAD_DIM),
                                    lambda b, h, i: (jnp.maximum((b * nt + i) * hb - 1, 0), off + h))
    gate = pl.BlockSpec((tt, LANES), lambda b, h, i: (b * nt + i, 0))
    cw = lambda off: pl.BlockSpec((None, DN_CONV, HEAD_DIM), lambda b, h, i: (l, 0, off + h))
    chunked = lambda r, c: pl.BlockSpec((None, None, G, r, c), lambda b, h, i: (b, h, i, 0, 0))
    return pl.pallas_call(
        functools.partial(_dn_prep_kernel, tiles_per_seq=nt),
        grid=(B, H, nt),
        in_specs=[tok(qb), tok(kb), tok(vb), halo(qb), halo(kb), halo(vb), gate, gate, chunked(1, C),
                  cw(0), cw(H), cw(2 * H)],
        out_specs=[chunked(2 * C, HEAD_DIM), chunked(C, HEAD_DIM), chunked(C, HEAD_DIM), chunked(C, C),
                   chunked(1, LANES)],
        out_shape=[jax.ShapeDtypeStruct((B, H, N, 2 * C, HEAD_DIM), BF16),
                   jax.ShapeDtypeStruct((B, H, N, C, HEAD_DIM), F32),
                   jax.ShapeDtypeStruct((B, H, N, C, HEAD_DIM), BF16),
                   jax.ShapeDtypeStruct((B, H, N, C, C), BF16),
                   jax.ShapeDtypeStruct((B, H, N, 1, LANES), F32)],
        compiler_params=_cparams(("parallel", "parallel", "parallel")),
        name="dn_prep",
    )(p, p, p, p, p, p, beta, gc, gcrow, conv_w, conv_w, conv_w)


def _dn_scan_kernel(wq_ref, u_ref, kd_ref, a_ref, c_ref, z_ref, nw_ref, y_ref, s_ref):
    B, H, G = wq_ref.shape[0], wq_ref.shape[1], wq_ref.shape[2]
    C = DN_CHUNK
    D = HEAD_DIM

    @pl.when(pl.program_id(0) == 0)
    def _():
        s_ref[...] = jnp.zeros_like(s_ref)

    nw = nw_ref[...]
    chains = [(b, h) for b in range(B) for h in range(H)]

    def body(n, carry):
        r0 = pl.multiple_of(n * C, C)
        S = [s_ref[b * H + h] for b, h in chains]
        m1 = [jnp.dot(wq_ref[b, h, n], S[i].astype(BF16), preferred_element_type=F32)
              for i, (b, h) in enumerate(chains)]
        v_new = [(u_ref[b, h, n] - m1[i][0:C]).astype(BF16) for i, (b, h) in enumerate(chains)]
        o = [m1[i][C:2 * C] + jnp.dot(a_ref[b, h, n], v_new[i], preferred_element_type=F32)
             for i, (b, h) in enumerate(chains)]
        kv = [lax.dot_general(kd_ref[b, h, n], v_new[i], (((0,), (0,)), ((), ())), preferred_element_type=F32)
              for i, (b, h) in enumerate(chains)]
        for i, (b, h) in enumerate(chains):
            s_ref[b * H + h] = S[i] * c_ref[b, h, n] + kv[i]
            zz = z_ref[b, pl.ds(r0, C), h * D:(h + 1) * D]
            y_ref[b, pl.ds(r0, C), h * D:(h + 1) * D] = (_rms(o[i], nw) * (zz * _sigmoid(zz))).astype(y_ref.dtype)
        return carry

    lax.fori_loop(0, G, body, 0)


def _dn_scan(wq, u, kd, a, c, p, norm_w, l, B, T, tt):
    H = DN_HEADS
    C = DN_CHUNK
    G = tt // C
    nt = T // tt
    chunked = lambda r, cc: pl.BlockSpec((B, H, G, r, cc), lambda i: (0, 0, i, 0, 0))
    y = pl.pallas_call(
        _dn_scan_kernel,
        grid=(nt,),
        in_specs=[chunked(2 * C, HEAD_DIM), chunked(C, HEAD_DIM), chunked(C, HEAD_DIM), chunked(C, C),
                  chunked(1, LANES),
                  pl.BlockSpec((B, tt, DN_W), lambda i: (0, i, P_Z // DN_W)),
                  pl.BlockSpec((None, 1, HEAD_DIM), lambda i: (l, 0, 0))],
        out_specs=pl.BlockSpec((B, tt, DN_W), lambda i: (0, i, 0)),
        out_shape=jax.ShapeDtypeStruct((B, T, DN_W), BF16),
        scratch_shapes=[pltpu.VMEM((B * H, HEAD_DIM, HEAD_DIM), F32)],
        compiler_params=_cparams(("arbitrary",)),
        name="dn_scan",
    )(wq, u, kd, a, c, p.reshape(B, T, p.shape[1]), norm_w)
    return y.reshape(B * T, DN_W)


def _dn_layer(p, conv_w, alog_rows, dt_rows, norm_w, l, B, T):
    N = T // DN_CHUNK
    tt = min(1024, T)
    beta, gc = _gates(p, alog_rows, dt_rows, l, B, T)
    gcrow = gc[:, DN_HEADS:2 * DN_HEADS].reshape(B, N, DN_CHUNK, DN_HEADS).transpose(0, 3, 1, 2)
    gcrow = gcrow.reshape(B, DN_HEADS, N, 1, DN_CHUNK)
    wq, u, kd, a, c = _dn_prep(p, beta, gc, gcrow, conv_w, l, B, T, tt)
    return _dn_scan(wq, u, kd, a, c, p, norm_w, l, B, T, tt // 2)


POOL_HALO = 16


def _pool_kernel(x_ref, halo_ref, w_ref, scale_ref, o_ref, *, tiles_per_seq):
    i = pl.program_id(0)
    tm = x_ref.shape[0]
    t0 = (i % tiles_per_seq) * tm
    first = (i % tiles_per_seq) == 0
    pos = t0 + 1 + lax.broadcasted_iota(jnp.int32, (tm, POOL_GROUP_DIM), 0)
    for gi, win in enumerate(POOL_WINDOWS):
        cols = slice(gi * POOL_GROUP_DIM, (gi + 1) * POOL_GROUP_DIM)
        halo = jnp.where(first, 0.0, halo_ref[:, cols])
        xg = x_ref[:, cols]
        ext = jnp.concatenate([halo, xg], axis=0)
        acc = ext
        s = 1
        while s < win:
            acc = acc + pltpu.roll(acc, s, axis=0)
            s *= 2
        cnt = jnp.minimum(pos, win).astype(F32)
        y = acc[POOL_HALO:] / cnt - xg
        y = jnp.dot(y.astype(BF16), w_ref[gi].astype(BF16), preferred_element_type=F32)
        o_ref[:, cols] = (y * scale_ref[:, cols]).astype(o_ref.dtype)


def _pool(p, pool_w, pool_scale, l, T, tm):
    M = p.shape[0]
    cb = P_POOL // POOL_W
    hb = tm // POOL_HALO
    return pl.pallas_call(
        functools.partial(_pool_kernel, tiles_per_seq=T // tm),
        grid=(M // tm,),
        in_specs=[pl.BlockSpec((tm, POOL_W), lambda i: (i, cb)),
                  pl.BlockSpec((POOL_HALO, POOL_W), lambda i: (jnp.maximum(i * hb - 1, 0), cb)),
                  pl.BlockSpec((None, POOL_GROUPS, POOL_GROUP_DIM, POOL_GROUP_DIM), lambda i: (l, 0, 0, 0)),
                  pl.BlockSpec((None, 1, POOL_W), lambda i: (l, 0, 0))],
        out_specs=pl.BlockSpec((tm, POOL_W), lambda i: (i, 0)),
        out_shape=jax.ShapeDtypeStruct((M, POOL_W), BF16),
        compiler_params=_cparams(("parallel",)),
        name="pool",
    )(p, p, pool_w, pool_scale)


def _swa_kernel(q_ref, k_ref, kp_ref, v_ref, vp_ref, cos_ref, sin_ref, cosp_ref, sinp_ref, sink_ref, o_ref,
                *, tiles_per_seq):
    i = pl.program_id(1)
    first = (i % tiles_per_seq) == 0
    tm = q_ref.shape[0]
    blk = SWA_BLOCK
    G = SWA_HEADS // SWA_KV_HEADS
    D = HEAD_DIM

    def rope(x, cos, sin):
        return x * cos + pltpu.roll(x, D // 2, axis=1) * sin

    cos = cos_ref[...]
    sin = sin_ref[...]
    cosx = jnp.concatenate([cosp_ref[...], cos], axis=0)
    sinx = jnp.concatenate([sinp_ref[...], sin], axis=0)
    kx = rope(jnp.concatenate([kp_ref[...], k_ref[...]], axis=0), cosx, sinx).astype(BF16)
    vx = jnp.concatenate([vp_ref[...], v_ref[...]], axis=0).astype(BF16)
    vx = jnp.concatenate([vx, jnp.ones_like(vx)], axis=1)
    scale = D ** -0.5
    qh = [(rope(q_ref[:, g * D:(g + 1) * D], cos, sin) * scale).astype(BF16) for g in range(G)]

    ri = lax.broadcasted_iota(jnp.int32, (G * blk, 2 * blk), 0)
    ii = ri % blk
    jj = lax.broadcasted_iota(jnp.int32, (G * blk, 2 * blk), 1)
    band = jnp.logical_or(jnp.logical_and(jj < blk, jj > ii), jnp.logical_and(jj >= blk, jj - blk <= ii))
    no_prev = jnp.where(first, blk, 0)
    band_first = jnp.logical_and(band, jj >= no_prev)
    sink = sink_ref[:, 0:1]
    for g in range(1, G):
        sink = jnp.where(ri[:, 0:1] >= g * blk, sink_ref[:, g:g + 1], sink)
    for b in range(tm // blk):
        qs = jnp.concatenate([q[b * blk:(b + 1) * blk] for q in qh], axis=0)
        keys = kx[b * blk:(b + 2) * blk]
        s = lax.dot_general(qs, keys, (((1,), (1,)), ((), ())), preferred_element_type=F32)
        s = jnp.where(band_first if b == 0 else band, s, -jnp.inf)
        m = jnp.maximum(jnp.max(s, axis=-1, keepdims=True), sink)
        e = jnp.exp(s - m).astype(BF16)
        pv = jnp.dot(e, vx[b * blk:(b + 2) * blk], preferred_element_type=F32)
        o = pv[:, 0:D] / (pv[:, D:2 * D] + jnp.exp(sink - m))
        for g in range(G):
            o_ref[b * blk:(b + 1) * blk, g * D:(g + 1) * D] = o[g * blk:(g + 1) * blk].astype(o_ref.dtype)


def _swa(p, cos, sin, sink_rows, l, T, tm):
    M = p.shape[0]
    blk = SWA_BLOCK
    r = tm // blk
    qb, kb, vb = P_SQ // SWA_GROUP_W, P_SK // HEAD_DIM, P_SV // HEAD_DIM
    prev = lambda i: jnp.maximum(i * r - 1, 0)
    return pl.pallas_call(
        functools.partial(_swa_kernel, tiles_per_seq=T // tm),
        grid=(SWA_KV_HEADS, M // tm),
        in_specs=[pl.BlockSpec((tm, SWA_GROUP_W), lambda kv, i: (i, qb + kv)),
                  pl.BlockSpec((tm, HEAD_DIM), lambda kv, i: (i, kb + kv)),
                  pl.BlockSpec((blk, HEAD_DIM), lambda kv, i: (prev(i), kb + kv)),
                  pl.BlockSpec((tm, HEAD_DIM), lambda kv, i: (i, vb + kv)),
                  pl.BlockSpec((blk, HEAD_DIM), lambda kv, i: (prev(i), vb + kv)),
                  pl.BlockSpec((tm, HEAD_DIM), lambda kv, i: (i, 0)),
                  pl.BlockSpec((tm, HEAD_DIM), lambda kv, i: (i, 0)),
                  pl.BlockSpec((blk, HEAD_DIM), lambda kv, i: (prev(i), 0)),
                  pl.BlockSpec((blk, HEAD_DIM), lambda kv, i: (prev(i), 0)),
                  pl.BlockSpec((None, None, 1, LANES), lambda kv, i: (l, kv, 0, 0))],
        out_specs=pl.BlockSpec((tm, SWA_GROUP_W), lambda kv, i: (i, kv)),
        out_shape=jax.ShapeDtypeStruct((M, SWA_W), BF16),
        compiler_params=_cparams(("parallel", "parallel")),
        name="swa",
    )(p, p, p, p, p, cos, sin, cos, sin, sink_rows)


OUT_ROWS = 256


def _outproj_kernel(ydn_ref, ypool_ref, yswa_ref, w_ref, x_ref, gpost_ref, gffn_ref, xo_ref, h_ref):
    for r0 in range(0, x_ref.shape[0], OUT_ROWS):
        rows = slice(r0, r0 + OUT_ROWS)
        y = jnp.concatenate([ydn_ref[rows, :], ypool_ref[rows, :], yswa_ref[rows, :]], axis=1)
        mix = jnp.dot(y, w_ref[...], preferred_element_type=F32)
        xn = x_ref[rows, :] + _rms(mix, gpost_ref[...])
        xo_ref[rows, :] = xn
        h_ref[rows, :] = _rms(xn, gffn_ref[...]).astype(h_ref.dtype)


def _out_proj(ydn, ypool, yswa, w_out, x2, gpost, gffn, l, tm):
    M, D = x2.shape
    row = lambda w: pl.BlockSpec((tm, w), lambda i: (i, 0))
    vec = pl.BlockSpec((None, 1, D), lambda i: (l, 0, 0))
    return pl.pallas_call(
        _outproj_kernel,
        grid=(M // tm,),
        in_specs=[row(DN_W), row(POOL_W), row(SWA_W),
                  pl.BlockSpec(w_out.shape, lambda i: (0, 0)), row(D), vec, vec],
        out_specs=[row(D), row(D)],
        out_shape=[jax.ShapeDtypeStruct((M, D), F32), jax.ShapeDtypeStruct((M, D), BF16)],
        compiler_params=_cparams(("parallel",)),
        name="out_proj",
    )(ydn, ypool, yswa, w_out, x2, gpost, gffn)


FFN_TILE = 1024
FFN_ROWS = 256


def _ffn_chunks(tm):
    return (FFN_ROWS,) * (tm // FFN_ROWS) if tm % FFN_ROWS == 0 else (tm,)


def _ffn_up_kernel(h_ref, wa_ref, wb_ref, cwa_ref, cwb_ref, cba_ref, cbb_ref, g_ref, wa16, wb16, ua, ub,
                   *, tiles_per_seq):
    i = pl.program_id(1)
    tm = h_ref.shape[0]
    pad = SUBLANES

    @pl.when(i == 0)
    def _():
        wa16[...] = wa_ref[...].astype(BF16)
        wb16[...] = wb_ref[...].astype(BF16)

    @pl.when(i % tiles_per_seq == 0)
    def _():
        ua[...] = jnp.zeros_like(ua)
        ub[...] = jnp.zeros_like(ub)

    def conv(u, tail_ref, cw_ref, cb_ref):
        rows = u.shape[0]
        ext = jnp.concatenate([tail_ref[...], u], axis=0)
        tail_ref[...] = u[rows - pad:rows]
        acc = cb_ref[...] + u * cw_ref[FFN_CONV - 1:FFN_CONV, :]
        for s in range(1, FFN_CONV):
            acc = acc + ext[pad - s:pad - s + rows] * cw_ref[FFN_CONV - 1 - s:FFN_CONV - s, :]
        return acc

    r0 = 0
    for rows in _ffn_chunks(tm):
        hb = h_ref[r0:r0 + rows, :]
        a = conv(jnp.dot(hb, wa16[...], preferred_element_type=F32), ua, cwa_ref, cba_ref)
        b = conv(jnp.dot(hb, wb16[...], preferred_element_type=F32), ub, cwb_ref, cbb_ref)
        g_ref[r0:r0 + rows, :] = (a * _sigmoid(a) * b).astype(g_ref.dtype)
        r0 += rows


def _ffn_up(h, w_up, conv_w, conv_b, l, T, tm, tn):
    M, D = h.shape
    F = w_up.shape[2] // 2
    nj = F // tn
    return pl.pallas_call(
        functools.partial(_ffn_up_kernel, tiles_per_seq=T // tm),
        grid=(nj, M // tm),
        in_specs=[pl.BlockSpec((tm, D), lambda j, i: (i, 0)),
                  pl.BlockSpec((None, D, tn), lambda j, i: (l, 0, j)),
                  pl.BlockSpec((None, D, tn), lambda j, i: (l, 0, nj + j)),
                  pl.BlockSpec((None, FFN_CONV, tn), lambda j, i: (l, 0, j)),
                  pl.BlockSpec((None, FFN_CONV, tn), lambda j, i: (l, 0, nj + j)),
                  pl.BlockSpec((None, 1, tn), lambda j, i: (l, 0, j)),
                  pl.BlockSpec((None, 1, tn), lambda j, i: (l, 0, nj + j))],
        out_specs=pl.BlockSpec((tm, tn), lambda j, i: (i, j)),
        out_shape=jax.ShapeDtypeStruct((M, F), BF16),
        scratch_shapes=[pltpu.VMEM((D, tn), BF16)] * 2 + [pltpu.VMEM((SUBLANES, tn), F32)] * 2,
        compiler_params=_cparams(("parallel", "arbitrary")),
        name="ffn_up",
    )(h, w_up, w_up, conv_w, conv_w, conv_b, conv_b)


DOWN_ROWS = 128


def _ffn_down_kernel(g_ref, w_ref, x_ref, gpost_ref, gnext_ref, xo_ref, h_ref):
    for r0 in range(0, x_ref.shape[0], DOWN_ROWS):
        rows = slice(r0, r0 + DOWN_ROWS)
        f = jnp.dot(g_ref[rows, :], w_ref[...], preferred_element_type=F32)
        xn = x_ref[rows, :] + _rms(f, gpost_ref[...])
        xo_ref[rows, :] = xn
        h_ref[rows, :] = _rms(xn, gnext_ref[...]).astype(h_ref.dtype)


def _ffn_down(g, w_down, x2, gpost, gnext, l, lnext, tm):
    M, D = x2.shape
    F = g.shape[1]
    return pl.pallas_call(
        _ffn_down_kernel,
        grid=(M // tm,),
        in_specs=[pl.BlockSpec((tm, F), lambda i: (i, 0)),
                  pl.BlockSpec((F, D), lambda i: (0, 0), pipeline_mode=pl.Buffered(1)),
                  pl.BlockSpec((tm, D), lambda i: (i, 0)),
                  pl.BlockSpec((None, 1, D), lambda i: (l, 0, 0)),
                  pl.BlockSpec((None, 1, D), lambda i: (lnext, 0, 0))],
        out_specs=[pl.BlockSpec((tm, D), lambda i: (i, 0))] * 2,
        out_shape=[jax.ShapeDtypeStruct((M, D), F32), jax.ShapeDtypeStruct((M, D), BF16)],
        compiler_params=_cparams(("parallel",)),
        name="ffn_down",
    )(g, w_down, x2, gpost, gnext)


def _rows(v):
    return v.astype(F32).reshape(v.shape[0], 1, v.shape[1])


def kernel(x, positions, norm_mix_pre, w_in, dn_conv_w, dn_a_log, dn_dt_bias, dn_norm_w, pool_w, pool_scale,
           swa_sinks, w_out, norm_mix_post, norm_ffn_pre, ffn_w_up, ffn_conv_w, ffn_conv_b, ffn_w_down,
           norm_ffn_post):
    B, T, D = x.shape
    depth = w_in.shape[0]
    M = B * T
    tm = min(512, T)
    G = SWA_HEADS // SWA_KV_HEADS
    x2 = x.reshape(M, D).astype(F32)

    w_in16 = _prep_w_in(w_in)
    gate_pad = ((0, 0), (DN_HEADS, LANES - 2 * DN_HEADS))
    alog_rows = _rows(jnp.pad(dn_a_log, gate_pad))
    dt_rows = _rows(jnp.pad(dn_dt_bias, gate_pad))
    sink_rows = jnp.pad(swa_sinks.astype(F32).reshape(depth, SWA_KV_HEADS, 1, G),
                        ((0, 0), (0, 0), (0, 0), (0, LANES - G)))
    g_mix_pre, g_mix_post = _rows(norm_mix_pre), _rows(norm_mix_post)
    g_ffn_pre, g_ffn_post = _rows(norm_ffn_pre), _rows(norm_ffn_post)
    dn_nw, pscale, conv_b = _rows(dn_norm_w), _rows(pool_scale), _rows(ffn_conv_b)

    cos, sin = _rope_tables(positions, tm)
    h = _prenorm(x2, g_mix_pre, tm)
    for l in range(depth):
        p, w_out16, w_down16 = _in_proj(h, w_in16, w_out, ffn_w_down, l, min(2 * tm, T), P_W // 3)
        y_dn = _dn_layer(p, dn_conv_w, alog_rows, dt_rows, dn_nw, l, B, T)
        y_pool = _pool(p, pool_w, pscale, l, T, tm)
        y_swa = _swa(p, cos, sin, sink_rows, l, T, tm)
        x2, h = _out_proj(y_dn, y_pool, y_swa, w_out16, x2, g_mix_post, g_ffn_pre, l, tm)
        g = _ffn_up(h, ffn_w_up, ffn_conv_w, conv_b, l, T, min(FFN_TILE, T), 512)
        x2, h = _ffn_down(g, w_down16, x2, g_ffn_post, g_mix_pre, l, min(l + 1, depth - 1), tm // 2)
    return x2.reshape(B, T, D).astype(x.dtype)
```

```python
import functools
import math

import numpy as np
import jax
import jax.numpy as jnp
from jax import lax
from jax.experimental import pallas as pl
from jax.experimental.pallas import tpu as pltpu

HEAD_DIM = 128
DN_HEADS = 6
DN_CONV = 4
DN_CHUNK = 64
POOL_WINDOWS = (2, 4, 8, 16)
POOL_GROUPS = 4
POOL_GROUP_DIM = 128
SWA_HEADS = 6
SWA_KV_HEADS = 2
SWA_WINDOW = 128
SWA_BLOCK = 128
ROPE_THETA = 10000.0
FFN_CONV = 3
NORM_EPS = 1e-6

DN_W = DN_HEADS * HEAD_DIM
POOL_W = POOL_GROUPS * POOL_GROUP_DIM
SWA_W = SWA_HEADS * HEAD_DIM
SWA_KV_W = SWA_KV_HEADS * HEAD_DIM
SWA_GROUP_W = SWA_W // SWA_KV_HEADS
LANES = 128
SUBLANES = 8

R_QKV = 0
R_Z = R_QKV + 3 * DN_W
R_B = R_Z + DN_W
R_POOL = R_B + 2 * DN_HEADS
R_SQ = R_POOL + POOL_W
R_SK = R_SQ + SWA_W
R_SV = R_SK + SWA_KV_W
R_W = R_SV + SWA_KV_W
P_Z = 0
P_SQ = P_Z + DN_W
P_POOL = P_SQ + SWA_W
P_SK = P_POOL + POOL_W
P_SV = P_SK + SWA_KV_W
P_BA = P_SV + SWA_KV_W
P_QKV = P_BA + LANES
P_W = P_QKV + 3 * DN_W
assert P_Z % DN_W == 0 and P_SQ % SWA_GROUP_W == 0 and P_POOL % POOL_W == 0
assert P_SK % HEAD_DIM == 0 and P_SV % HEAD_DIM == 0 and P_BA % LANES == 0 and P_QKV % LANES == 0

VMEM_LIMIT = 56 * 1024 * 1024

ROW_TILE = 512
IN_PROJ_ROWS = 1024
IN_PROJ_PANELS = 3
FFN_PANEL = 512
FFN_DOWN_ROWS = 256
assert (P_W // IN_PROJ_PANELS) % LANES == 0 and P_W % IN_PROJ_PANELS == 0

F32 = jnp.float32
BF16 = jnp.bfloat16


def _cparams(sem):
    return pltpu.CompilerParams(dimension_semantics=sem, vmem_limit_bytes=VMEM_LIMIT)


def _rms(v, gain):
    return v * lax.rsqrt(jnp.mean(v * v, axis=-1, keepdims=True) + NORM_EPS) * gain


def _sigmoid(v):
    return 1.0 / (1.0 + jnp.exp(-v))


def _bdot(a, b):
    return jnp.einsum("gik,gkj->gij", a.astype(BF16), b.astype(BF16), preferred_element_type=F32)


def _bdot_nt(a, b):
    return jnp.einsum("gik,gjk->gij", a.astype(BF16), b.astype(BF16), preferred_element_type=F32)


def _win_kernel(src_ref, w_ref, o_ref):
    L = w_ref.shape[1]
    row = lax.broadcasted_iota(jnp.int32, (w_ref.shape[0], w_ref.shape[2]), 0)
    keep = jnp.logical_or(pl.program_id(0) != P_BA // LANES, row < 2 * DN_HEADS)
    for l in range(L):
        o_ref[l] = jnp.where(keep, w_ref[:, l, :], 0.0).astype(o_ref.dtype)


def _prep_w_in(w_in):
    L, K, _ = w_in.shape
    wt = jnp.transpose(w_in, (2, 0, 1))
    sections = ((P_Z, R_Z, DN_W), (P_SQ, R_SQ, SWA_W), (P_POOL, R_POOL, POOL_W), (P_SK, R_SK, SWA_KV_W),
                (P_SV, R_SV, SWA_KV_W), (P_BA, R_B, LANES), (P_QKV, R_QKV, 3 * DN_W))
    src = np.zeros((P_W // LANES,), np.int32)
    for p0, r0, width in sections:
        for k in range(width // LANES):
            src[p0 // LANES + k] = r0 + k * LANES
    return pl.pallas_call(
        _win_kernel,
        grid_spec=pltpu.PrefetchScalarGridSpec(
            num_scalar_prefetch=1,
            grid=(P_W // LANES,),
            in_specs=[pl.BlockSpec((pl.Element(LANES), pl.Element(L), pl.Element(K)),
                                   lambda c, src_ref: (src_ref[c], 0, 0))],
            out_specs=pl.BlockSpec((L, LANES, K), lambda c, src_ref: (0, c, 0)),
        ),
        out_shape=jax.ShapeDtypeStruct((L, P_W, K), BF16),
        compiler_params=_cparams(("parallel",)),
        name="prep_w_in",
    )(jnp.asarray(src), wt)


def _prenorm_kernel(x_ref, g_ref, h_ref):
    h_ref[...] = _rms(x_ref[...], g_ref[...]).astype(h_ref.dtype)


def _prenorm(x2, gains, tm):
    M, D = x2.shape
    return pl.pallas_call(
        _prenorm_kernel,
        grid=(M // tm,),
        in_specs=[pl.BlockSpec((tm, D), lambda i: (i, 0)), pl.BlockSpec((None, 1, D), lambda i: (0, 0, 0))],
        out_specs=pl.BlockSpec((tm, D), lambda i: (i, 0)),
        out_shape=jax.ShapeDtypeStruct((M, D), BF16),
        compiler_params=_cparams(("parallel",)),
        name="prenorm",
    )(x2, gains)


def _rope_kernel(pos_ref, freq_ref, cos_ref, sin_ref):
    ang = pos_ref[...] * freq_ref[...]
    lane = lax.broadcasted_iota(jnp.int32, ang.shape, 1)
    cos_ref[...] = jnp.cos(ang)
    sin_ref[...] = jnp.where(lane < HEAD_DIM // 2, -jnp.sin(ang), jnp.sin(ang))


def _rope_tables(positions, tm):
    M = positions.size
    pos = positions.astype(F32).reshape(M, 1)
    inv = (1.0 / (ROPE_THETA ** (np.arange(0, HEAD_DIM, 2, dtype=np.float32) / HEAD_DIM))).astype(np.float32)
    freq = jnp.asarray(np.concatenate([inv, inv])[None, :])
    return pl.pallas_call(
        _rope_kernel,
        grid=(M // tm,),
        in_specs=[pl.BlockSpec((tm, 1), lambda i: (i, 0)), pl.BlockSpec((1, HEAD_DIM), lambda i: (0, 0))],
        out_specs=[pl.BlockSpec((tm, HEAD_DIM), lambda i: (i, 0))] * 2,
        out_shape=[jax.ShapeDtypeStruct((M, HEAD_DIM), F32)] * 2,
        compiler_params=_cparams(("parallel",)),
        name="rope_tables",
    )(pos, freq)


def _mm_kernel(a_ref, wt_ref, wo_ref, wd_ref, o_ref, wo16_ref, wd16_ref):
    o_ref[...] = lax.dot_general(a_ref[...], wt_ref[...], (((1,), (1,)), ((), ())), preferred_element_type=F32)
    wo16_ref[...] = wo_ref[...].astype(wo16_ref.dtype)
    wd16_ref[...] = wd_ref[...].astype(wd16_ref.dtype)


def _cast_slices(rows, steps):
    return max(n for n in range(1, steps + 1) if rows % n == 0 and (rows // n) % (2 * SUBLANES) == 0)


def _in_proj(h, wt, w_out, w_down, l, tm, tn):
    M, K = h.shape
    N = wt.shape[1]
    nj, nt = N // tn, M // tm
    specs, shapes = [], []
    for w in (w_out, w_down):
        rows, cols = w.shape[1], w.shape[2]
        n = _cast_slices(rows, nj * nt)
        idx = lambda j, i, n=n: jnp.minimum(j * nt + i, n - 1)
        specs.append((pl.BlockSpec((None, rows // n, cols), lambda j, i, idx=idx: (l, idx(j, i), 0)),
                      pl.BlockSpec((rows // n, cols), lambda j, i, idx=idx: (idx(j, i), 0))))
        shapes.append(jax.ShapeDtypeStruct((rows, cols), BF16))
    return pl.pallas_call(
        _mm_kernel,
        grid=(nj, nt),
        in_specs=[pl.BlockSpec((tm, K), lambda j, i: (i, 0)), pl.BlockSpec((None, tn, K), lambda j, i: (l, j, 0)),
                  specs[0][0], specs[1][0]],
        out_specs=[pl.BlockSpec((tm, tn), lambda j, i: (i, j)), specs[0][1], specs[1][1]],
        out_shape=[jax.ShapeDtypeStruct((M, N), F32)] + shapes,
        compiler_params=_cparams(("arbitrary", "arbitrary")),
        name="in_proj",
    )(h, wt, w_out, w_down)


def _gate_kernel(ba_ref, alog_ref, dt_ref, beta_ref, gc_ref):
    p = ba_ref[...]
    beta_ref[...] = _sigmoid(p)
    sp = p + dt_ref[...]
    softplus = jnp.maximum(sp, 0.0) + jnp.log(1.0 + jnp.exp(-jnp.abs(sp)))
    g = -jnp.exp(alog_ref[...]) * softplus
    row = lax.broadcasted_iota(jnp.int32, g.shape, 0) % DN_CHUNK
    s = 1
    while s < DN_CHUNK:
        g = g + jnp.where(row >= s, pltpu.roll(g, s, axis=0), 0.0)
        s *= 2
    gc_ref[...] = g


def _gates(p, alog_rows, dt_rows, l, B, T):
    M = p.shape[0]
    cb = P_BA // LANES
    vec = pl.BlockSpec((None, 1, LANES), lambda b: (l, 0, 0))
    return pl.pallas_call(
        _gate_kernel,
        grid=(B,),
        in_specs=[pl.BlockSpec((T, LANES), lambda b: (b, cb)), vec, vec],
        out_specs=[pl.BlockSpec((T, LANES), lambda b: (b, 0))] * 2,
        out_shape=[jax.ShapeDtypeStruct((M, LANES), F32)] * 2,
        compiler_params=_cparams(("parallel",)),
        name="dn_gates",
    )(p, alog_rows, dt_rows)


def _dn_prep_kernel(q_ref, k_ref, v_ref, qh_ref, kh_ref, vh_ref, beta_ref, gc_ref, gcrow_ref,
                    cwq_ref, cwk_ref, cwv_ref, wq_ref, u_ref, kd_ref, a_ref, c_ref, *, tiles_per_seq):
    h = pl.program_id(1)
    first = (pl.program_id(2) % tiles_per_seq) == 0
    tt = q_ref.shape[0]
    C = DN_CHUNK
    G = tt // C
    D = HEAD_DIM
    pad = SUBLANES

    def conv_silu(x_ref, halo_ref, cw_ref):
        halo = jnp.where(first, 0.0, halo_ref[...])
        ext = jnp.concatenate([halo, x_ref[...]], axis=0)
        c0, c1, c2, c3 = (cw_ref[j:j + 1, :] for j in range(DN_CONV))
        back1 = pltpu.roll(ext, 1, axis=0)
        z = ext * c1 + back1 * c0
        acc = (ext * c3 + back1 * c2 + pltpu.roll(z, 2, axis=0))[pad:pad + tt]
        return acc * _sigmoid(acc)

    q = conv_silu(q_ref, qh_ref, cwq_ref)
    k = conv_silu(k_ref, kh_ref, cwk_ref)
    v = conv_silu(v_ref, vh_ref, cwv_ref)
    q = q * lax.rsqrt(jnp.sum(q * q, axis=-1, keepdims=True) + NORM_EPS) * (D ** -0.5)
    k = k * lax.rsqrt(jnp.sum(k * k, axis=-1, keepdims=True) + NORM_EPS)
    lane = lax.broadcasted_iota(jnp.int32, (tt, LANES), 1)
    beta = jnp.sum(jnp.where(lane == h, beta_ref[...], 0.0), axis=-1, keepdims=True)
    gcc = jnp.sum(jnp.where(lane == h + DN_HEADS, gc_ref[...], 0.0), axis=-1, keepdims=True)
    eg = jnp.exp(gcc)
    kb = k * beta
    vb = v * beta

    q3 = q.reshape(G, C, D)
    k3 = k.reshape(G, C, D)
    kb3 = kb.reshape(G, C, D)
    gcc3 = gcc.reshape(G, C, 1)
    gcr3 = gcrow_ref[...]
    ii = lax.broadcasted_iota(jnp.int32, (G, C, C), 1)
    jj = lax.broadcasted_iota(jnp.int32, (G, C, C), 2)
    decay = jnp.exp(jnp.where(ii >= jj, gcc3 - gcr3, -jnp.inf))
    eye = (ii == jj).astype(F32)
    kq = _bdot_nt(jnp.concatenate([kb3, q3], axis=1), k3)
    nil = jnp.where(ii > jj, -kq[:, 0:C] * decay, 0.0)
    inv = eye + nil
    powk = _bdot(nil, nil)
    for _ in range(int(math.log2(C)) - 2):
        both = _bdot(jnp.concatenate([inv, powk], axis=1), powk)
        inv = inv + both[:, 0:C]
        powk = both[:, C:2 * C]
    inv = inv + _bdot(inv, powk)
    rhs = jnp.concatenate([vb, kb * eg], axis=-1).reshape(G, C, 2 * D)
    uw = _bdot(inv, rhs)
    glast = gcr3[:, :, C - 1:C]
    u_ref[...] = uw[:, :, :D]
    wq_ref[:, 0:C, :] = uw[:, :, D:].astype(wq_ref.dtype)
    wq_ref[:, C:2 * C, :] = (q * eg).reshape(G, C, D).astype(wq_ref.dtype)
    kd_ref[...] = (k3 * jnp.exp(glast - gcc3)).astype(kd_ref.dtype)
    a_ref[...] = (kq[:, C:2 * C] * decay).astype(a_ref.dtype)
    c_ref[...] = jnp.broadcast_to(jnp.exp(glast), c_ref.shape)


def _dn_prep(p, beta, gc, gcrow, conv_w, l, B, T, tt):
    H = DN_HEADS
    C = DN_CHUNK
    N = T // C
    G = tt // C
    nt = T // tt
    hb = tt // SUBLANES
    qb, kb, vb = P_QKV // LANES, (P_QKV + DN_W) // LANES, (P_QKV + 2 * DN_W) // LANES
    tok = lambda off: pl.BlockSpec((tt, HEAD_DIM), lambda b, h, i: (b * nt + i, off + h))
    halo = lambda off: pl.BlockSpec((SUBLANES, HEAD_DIM),
                                    lambda b, h, i: (jnp.maximum((b * nt + i) * hb - 1, 0), off + h))
    gate = pl.BlockSpec((tt, LANES), lambda b, h, i: (b * nt + i, 0))
    cw = lambda off: pl.BlockSpec((None, DN_CONV, HEAD_DIM), lambda b, h, i: (l, 0, off + h))
    chunked = lambda r, c: pl.BlockSpec((None, None, G, r, c), lambda b, h, i: (b, h, i, 0, 0))
    return pl.pallas_call(
        functools.partial(_dn_prep_kernel, tiles_per_seq=nt),
        grid=(B, H, nt),
        in_specs=[tok(qb), tok(kb), tok(vb), halo(qb), halo(kb), halo(vb), gate, gate, chunked(1, C),
                  cw(0), cw(H), cw(2 * H)],
        out_specs=[chunked(2 * C, HEAD_DIM), chunked(C, HEAD_DIM), chunked(C, HEAD_DIM), chunked(C, C),
                   chunked(1, LANES)],
        out_shape=[jax.ShapeDtypeStruct((B, H, N, 2 * C, HEAD_DIM), BF16),
                   jax.ShapeDtypeStruct((B, H, N, C, HEAD_DIM), F32),
                   jax.ShapeDtypeStruct((B, H, N, C, HEAD_DIM), BF16),
                   jax.ShapeDtypeStruct((B, H, N, C, C), BF16),
                   jax.ShapeDtypeStruct((B, H, N, 1, LANES), F32)],
        compiler_params=_cparams(("parallel", "parallel", "parallel")),
        name="dn_prep",
    )(p, p, p, p, p, p, beta, gc, gcrow, conv_w, conv_w, conv_w)


def _dn_scan_kernel(wq_ref, u_ref, kd_ref, a_ref, c_ref, z_ref, nw_ref, y_ref, s_ref):
    B, H, G = wq_ref.shape[0], wq_ref.shape[1], wq_ref.shape[2]
    C = DN_CHUNK
    D = HEAD_DIM

    @pl.when(pl.program_id(0) == 0)
    def _():
        s_ref[...] = jnp.zeros_like(s_ref)

    nw = nw_ref[...]
    chains = [(b, h) for b in range(B) for h in range(H)]

    def body(n, carry):
        r0 = pl.multiple_of(n * C, C)
        S = [s_ref[b * H + h] for b, h in chains]
        m1 = [jnp.dot(wq_ref[b, h, n], S[i].astype(BF16), preferred_element_type=F32)
              for i, (b, h) in enumerate(chains)]
        v_new = [(u_ref[b, h, n] - m1[i][0:C]).astype(BF16) for i, (b, h) in enumerate(chains)]
        o = [m1[i][C:2 * C] + jnp.dot(a_ref[b, h, n], v_new[i], preferred_element_type=F32)
             for i, (b, h) in enumerate(chains)]
        kv = [lax.dot_general(kd_ref[b, h, n], v_new[i], (((0,), (0,)), ((), ())), preferred_element_type=F32)
              for i, (b, h) in enumerate(chains)]
        for i, (b, h) in enumerate(chains):
            s_ref[b * H + h] = S[i] * c_ref[b, h, n] + kv[i]
            zz = z_ref[b, pl.ds(r0, C), h * D:(h + 1) * D]
            y_ref[b, pl.ds(r0, C), h * D:(h + 1) * D] = (_rms(o[i], nw) * (zz * _sigmoid(zz))).astype(y_ref.dtype)
        return carry

    lax.fori_loop(0, G, body, 0)


def _dn_scan(wq, u, kd, a, c, p, norm_w, l, B, T, tt):
    H = DN_HEADS
    C = DN_CHUNK
    G = tt // C
    nt = T // tt
    chunked = lambda r, cc: pl.BlockSpec((B, H, G, r, cc), lambda i: (0, 0, i, 0, 0))
    y = pl.pallas_call(
        _dn_scan_kernel,
        grid=(nt,),
        in_specs=[chunked(2 * C, HEAD_DIM), chunked(C, HEAD_DIM), chunked(C, HEAD_DIM), chunked(C, C),
                  chunked(1, LANES),
                  pl.BlockSpec((B, tt, DN_W), lambda i: (0, i, P_Z // DN_W)),
                  pl.BlockSpec((None, 1, HEAD_DIM), lambda i: (l, 0, 0))],
        out_specs=pl.BlockSpec((B, tt, DN_W), lambda i: (0, i, 0)),
        out_shape=jax.ShapeDtypeStruct((B, T, DN_W), BF16),
        scratch_shapes=[pltpu.VMEM((B * H, HEAD_DIM, HEAD_DIM), F32)],
        compiler_params=_cparams(("arbitrary",)),
        name="dn_scan",
    )(wq, u, kd, a, c, p.reshape(B, T, p.shape[1]), norm_w)
    return y.reshape(B * T, DN_W)


DN_PREP_ROWS = 2048
DN_SCAN_ROWS = 512


def _dn_layer(p, conv_w, alog_rows, dt_rows, norm_w, l, B, T):
    N = T // DN_CHUNK
    beta, gc = _gates(p, alog_rows, dt_rows, l, B, T)
    gcrow = gc[:, DN_HEADS:2 * DN_HEADS].reshape(B, N, DN_CHUNK, DN_HEADS).transpose(0, 3, 1, 2)
    gcrow = gcrow.reshape(B, DN_HEADS, N, 1, DN_CHUNK)
    wq, u, kd, a, c = _dn_prep(p, beta, gc, gcrow, conv_w, l, B, T, min(DN_PREP_ROWS, T))
    return _dn_scan(wq, u, kd, a, c, p, norm_w, l, B, T, min(DN_SCAN_ROWS, T))


POOL_HALO = 16


def _pool_kernel(x_ref, halo_ref, w_ref, scale_ref, o_ref, *, tiles_per_seq):
    i = pl.program_id(0)
    tm = x_ref.shape[0]
    t0 = (i % tiles_per_seq) * tm
    first = (i % tiles_per_seq) == 0
    pos = t0 + 1 + lax.broadcasted_iota(jnp.int32, (tm, POOL_GROUP_DIM), 0)
    for gi, win in enumerate(POOL_WINDOWS):
        cols = slice(gi * POOL_GROUP_DIM, (gi + 1) * POOL_GROUP_DIM)
        halo = jnp.where(first, 0.0, halo_ref[:, cols])
        xg = x_ref[:, cols]
        ext = jnp.concatenate([halo, xg], axis=0)
        acc = ext
        s = 1
        while s < win:
            acc = acc + pltpu.roll(acc, s, axis=0)
            s *= 2
        cnt = jnp.minimum(pos, win).astype(F32)
        y = acc[POOL_HALO:] / cnt - xg
        y = jnp.dot(y.astype(BF16), w_ref[gi].astype(BF16), preferred_element_type=F32)
        o_ref[:, cols] = (y * scale_ref[:, cols]).astype(o_ref.dtype)


def _pool(p, pool_w, pool_scale, l, T, tm):
    M = p.shape[0]
    cb = P_POOL // POOL_W
    hb = tm // POOL_HALO
    return pl.pallas_call(
        functools.partial(_pool_kernel, tiles_per_seq=T // tm),
        grid=(M // tm,),
        in_specs=[pl.BlockSpec((tm, POOL_W), lambda i: (i, cb)),
                  pl.BlockSpec((POOL_HALO, POOL_W), lambda i: (jnp.maximum(i * hb - 1, 0), cb)),
                  pl.BlockSpec((None, POOL_GROUPS, POOL_GROUP_DIM, POOL_GROUP_DIM), lambda i: (l, 0, 0, 0)),
                  pl.BlockSpec((None, 1, POOL_W), lambda i: (l, 0, 0))],
        out_specs=pl.BlockSpec((tm, POOL_W), lambda i: (i, 0)),
        out_shape=jax.ShapeDtypeStruct((M, POOL_W), BF16),
        compiler_params=_cparams(("parallel",)),
        name="pool",
    )(p, p, pool_w, pool_scale)


def _swa_kernel(q_ref, k_ref, kp_ref, v_ref, vp_ref, cos_ref, sin_ref, cosp_ref, sinp_ref, sink_ref, o_ref,
                *, tiles_per_seq):
    i = pl.program_id(1)
    first = (i % tiles_per_seq) == 0
    tm = q_ref.shape[0]
    blk = SWA_BLOCK
    G = SWA_HEADS // SWA_KV_HEADS
    D = HEAD_DIM

    def rope(x, cos, sin):
        return x * cos + pltpu.roll(x, D // 2, axis=1) * sin

    cos = cos_ref[...]
    sin = sin_ref[...]
    cosx = jnp.concatenate([cosp_ref[...], cos], axis=0)
    sinx = jnp.concatenate([sinp_ref[...], sin], axis=0)
    kx = rope(jnp.concatenate([kp_ref[...], k_ref[...]], axis=0), cosx, sinx).astype(BF16)
    vx = jnp.concatenate([vp_ref[...], v_ref[...]], axis=0).astype(BF16)
    vx = jnp.concatenate([vx, jnp.ones_like(vx)], axis=1)
    scale = D ** -0.5
    qh = [(rope(q_ref[:, g * D:(g + 1) * D], cos, sin) * scale).astype(BF16) for g in range(G)]

    ri = lax.broadcasted_iota(jnp.int32, (G * blk, 2 * blk), 0)
    ii = ri % blk
    jj = lax.broadcasted_iota(jnp.int32, (G * blk, 2 * blk), 1)
    band = jnp.logical_or(jnp.logical_and(jj < blk, jj > ii), jnp.logical_and(jj >= blk, jj - blk <= ii))
    no_prev = jnp.where(first, blk, 0)
    band_first = jnp.logical_and(band, jj >= no_prev)
    sink = sink_ref[:, 0:1]
    for g in range(1, G):
        sink = jnp.where(ri[:, 0:1] >= g * blk, sink_ref[:, g:g + 1], sink)
    for b in range(tm // blk):
        qs = jnp.concatenate([q[b * blk:(b + 1) * blk] for q in qh], axis=0)
        keys = kx[b * blk:(b + 2) * blk]
        s = lax.dot_general(qs, keys, (((1,), (1,)), ((), ())), preferred_element_type=F32)
        s = jnp.where(band_first if b == 0 else band, s, -jnp.inf)
        m = jnp.maximum(jnp.max(s, axis=-1, keepdims=True), sink)
        e = jnp.exp(s - m).astype(BF16)
        pv = jnp.dot(e, vx[b * blk:(b + 2) * blk], preferred_element_type=F32)
        o = pv[:, 0:D] / (pv[:, D:2 * D] + jnp.exp(sink - m))
        for g in range(G):
            o_ref[b * blk:(b + 1) * blk, g * D:(g + 1) * D] = o[g * blk:(g + 1) * blk].astype(o_ref.dtype)


def _swa(p, cos, sin, sink_rows, l, T, tm):
    M = p.shape[0]
    blk = SWA_BLOCK
    r = tm // blk
    qb, kb, vb = P_SQ // SWA_GROUP_W, P_SK // HEAD_DIM, P_SV // HEAD_DIM
    prev = lambda i: jnp.maximum(i * r - 1, 0)
    return pl.pallas_call(
        functools.partial(_swa_kernel, tiles_per_seq=T // tm),
        grid=(SWA_KV_HEADS, M // tm),
        in_specs=[pl.BlockSpec((tm, SWA_GROUP_W), lambda kv, i: (i, qb + kv)),
                  pl.BlockSpec((tm, HEAD_DIM), lambda kv, i: (i, kb + kv)),
                  pl.BlockSpec((blk, HEAD_DIM), lambda kv, i: (prev(i), kb + kv)),
                  pl.BlockSpec((tm, HEAD_DIM), lambda kv, i: (i, vb + kv)),
                  pl.BlockSpec((blk, HEAD_DIM), lambda kv, i: (prev(i), vb + kv)),
                  pl.BlockSpec((tm, HEAD_DIM), lambda kv, i: (i, 0)),
                  pl.BlockSpec((tm, HEAD_DIM), lambda kv, i: (i, 0)),
                  pl.BlockSpec((blk, HEAD_DIM), lambda kv, i: (prev(i), 0)),
                  pl.BlockSpec((blk, HEAD_DIM), lambda kv, i: (prev(i), 0)),
                  pl.BlockSpec((None, None, 1, LANES), lambda kv, i: (l, kv, 0, 0))],
        out_specs=pl.BlockSpec((tm, SWA_GROUP_W), lambda kv, i: (i, kv)),
        out_shape=jax.ShapeDtypeStruct((M, SWA_W), BF16),
        compiler_params=_cparams(("parallel", "parallel")),
        name="swa",
    )(p, p, p, p, p, cos, sin, cos, sin, sink_rows)


OUT_ROWS = 256


def _outproj_kernel(ydn_ref, ypool_ref, yswa_ref, w_ref, x_ref, gpost_ref, gffn_ref, xo_ref, h_ref):
    for r0 in range(0, x_ref.shape[0], OUT_ROWS):
        rows = slice(r0, r0 + OUT_ROWS)
        y = jnp.concatenate([ydn_ref[rows, :], ypool_ref[rows, :], yswa_ref[rows, :]], axis=1)
        mix = jnp.dot(y, w_ref[...], preferred_element_type=F32)
        xn = x_ref[rows, :] + _rms(mix, gpost_ref[...])
        xo_ref[rows, :] = xn
        h_ref[rows, :] = _rms(xn, gffn_ref[...]).astype(h_ref.dtype)


def _out_proj(ydn, ypool, yswa, w_out, x2, gpost, gffn, l, tm):
    M, D = x2.shape
    row = lambda w: pl.BlockSpec((tm, w), lambda i: (i, 0))
    vec = pl.BlockSpec((None, 1, D), lambda i: (l, 0, 0))
    return pl.pallas_call(
        _outproj_kernel,
        grid=(M // tm,),
        in_specs=[row(DN_W), row(POOL_W), row(SWA_W),
                  pl.BlockSpec(w_out.shape, lambda i: (0, 0)), row(D), vec, vec],
        out_specs=[row(D), row(D)],
        out_shape=[jax.ShapeDtypeStruct((M, D), F32), jax.ShapeDtypeStruct((M, D), BF16)],
        compiler_params=_cparams(("parallel",)),
        name="out_proj",
    )(ydn, ypool, yswa, w_out, x2, gpost, gffn)


FFN_TILE = 2048
FFN_ROWS = 128


def _ffn_chunks(tm):
    return (FFN_ROWS,) * (tm // FFN_ROWS) if tm % FFN_ROWS == 0 else (tm,)


def _ffn_up_kernel(h_ref, wa_ref, wb_ref, cwa_ref, cwb_ref, cba_ref, cbb_ref, g_ref, wa16, wb16, ua, ub,
                   *, tiles_per_seq):
    i = pl.program_id(1)
    tm = h_ref.shape[0]
    pad = SUBLANES

    @pl.when(i == 0)
    def _():
        wa16[...] = wa_ref[...].astype(BF16)
        wb16[...] = wb_ref[...].astype(BF16)

    @pl.when(i % tiles_per_seq == 0)
    def _():
        ua[...] = jnp.zeros_like(ua)
        ub[...] = jnp.zeros_like(ub)

    def conv(u, tail_ref, cw_ref, cb_ref):
        rows = u.shape[0]
        ext = jnp.concatenate([tail_ref[...], u], axis=0)
        tail_ref[...] = u[rows - pad:rows]
        acc = cb_ref[...] + u * cw_ref[FFN_CONV - 1:FFN_CONV, :]
        for s in range(1, FFN_CONV):
            acc = acc + ext[pad - s:pad - s + rows] * cw_ref[FFN_CONV - 1 - s:FFN_CONV - s, :]
        return acc

    r0 = 0
    for rows in _ffn_chunks(tm):
        hb = h_ref[r0:r0 + rows, :]
        a = conv(jnp.dot(hb, wa16[...], preferred_element_type=F32), ua, cwa_ref, cba_ref)
        b = conv(jnp.dot(hb, wb16[...], preferred_element_type=F32), ub, cwb_ref, cbb_ref)
        g_ref[r0:r0 + rows, :] = (a * _sigmoid(a) * b).astype(g_ref.dtype)
        r0 += rows


def _ffn_up(h, w_up, conv_w, conv_b, l, T, tm, tn):
    M, D = h.shape
    F = w_up.shape[2] // 2
    nj = F // tn
    return pl.pallas_call(
        functools.partial(_ffn_up_kernel, tiles_per_seq=T // tm),
        grid=(nj, M // tm),
        in_specs=[pl.BlockSpec((tm, D), lambda j, i: (i, 0)),
                  pl.BlockSpec((None, D, tn), lambda j, i: (l, 0, j)),
                  pl.BlockSpec((None, D, tn), lambda j, i: (l, 0, nj + j)),
                  pl.BlockSpec((None, FFN_CONV, tn), lambda j, i: (l, 0, j)),
                  pl.BlockSpec((None, FFN_CONV, tn), lambda j, i: (l, 0, nj + j)),
                  pl.BlockSpec((None, 1, tn), lambda j, i: (l, 0, j)),
                  pl.BlockSpec((None, 1, tn), lambda j, i: (l, 0, nj + j))],
        out_specs=pl.BlockSpec((tm, tn), lambda j, i: (i, j)),
        out_shape=jax.ShapeDtypeStruct((M, F), BF16),
        scratch_shapes=[pltpu.VMEM((D, tn), BF16)] * 2 + [pltpu.VMEM((SUBLANES, tn), F32)] * 2,
        compiler_params=_cparams(("parallel", "arbitrary")),
        name="ffn_up",
    )(h, w_up, w_up, conv_w, conv_w, conv_b, conv_b)


DOWN_ROWS = 128


def _ffn_down_kernel(g_ref, w_ref, x_ref, gpost_ref, gnext_ref, xo_ref, h_ref):
    for r0 in range(0, x_ref.shape[0], DOWN_ROWS):
        rows = slice(r0, r0 + DOWN_ROWS)
        f = jnp.dot(g_ref[rows, :], w_ref[...], preferred_element_type=F32)
        xn = x_ref[rows, :] + _rms(f, gpost_ref[...])
        xo_ref[rows, :] = xn
        h_ref[rows, :] = _rms(xn, gnext_ref[...]).astype(h_ref.dtype)


def _ffn_down(g, w_down, x2, gpost, gnext, l, lnext, tm):
    M, D = x2.shape
    F = g.shape[1]
    return pl.pallas_call(
        _ffn_down_kernel,
        grid=(M // tm,),
        in_specs=[pl.BlockSpec((tm, F), lambda i: (i, 0)),
                  pl.BlockSpec((F, D), lambda i: (0, 0), pipeline_mode=pl.Buffered(1)),
                  pl.BlockSpec((tm, D), lambda i: (i, 0)),
                  pl.BlockSpec((None, 1, D), lambda i: (l, 0, 0)),
                  pl.BlockSpec((None, 1, D), lambda i: (lnext, 0, 0))],
        out_specs=[pl.BlockSpec((tm, D), lambda i: (i, 0))] * 2,
        out_shape=[jax.ShapeDtypeStruct((M, D), F32), jax.ShapeDtypeStruct((M, D), BF16)],
        compiler_params=_cparams(("parallel",)),
        name="ffn_down",
    )(g, w_down, x2, gpost, gnext)


def _rows(v):
    return v.astype(F32).reshape(v.shape[0], 1, v.shape[1])


def kernel(x, positions, norm_mix_pre, w_in, dn_conv_w, dn_a_log, dn_dt_bias, dn_norm_w, pool_w, pool_scale,
           swa_sinks, w_out, norm_mix_post, norm_ffn_pre, ffn_w_up, ffn_conv_w, ffn_conv_b, ffn_w_down,
           norm_ffn_post):
    B, T, D = x.shape
    depth = w_in.shape[0]
    M = B * T
    tm = min(ROW_TILE, T)
    G = SWA_HEADS // SWA_KV_HEADS
    x2 = x.reshape(M, D).astype(F32)

    w_in16 = _prep_w_in(w_in)
    gate_pad = ((0, 0), (DN_HEADS, LANES - 2 * DN_HEADS))
    alog_rows = _rows(jnp.pad(dn_a_log, gate_pad))
    dt_rows = _rows(jnp.pad(dn_dt_bias, gate_pad))
    sink_rows = jnp.pad(swa_sinks.astype(F32).reshape(depth, SWA_KV_HEADS, 1, G),
                        ((0, 0), (0, 0), (0, 0), (0, LANES - G)))
    g_mix_pre, g_mix_post = _rows(norm_mix_pre), _rows(norm_mix_post)
    g_ffn_pre, g_ffn_post = _rows(norm_ffn_pre), _rows(norm_ffn_post)
    dn_nw, pscale, conv_b = _rows(dn_norm_w), _rows(pool_scale), _rows(ffn_conv_b)

    cos, sin = _rope_tables(positions, tm)
    h = _prenorm(x2, g_mix_pre, tm)
    for l in range(depth):
        p, w_out16, w_down16 = _in_proj(h, w_in16, w_out, ffn_w_down, l, min(IN_PROJ_ROWS, T),
                                        P_W // IN_PROJ_PANELS)
        y_dn = _dn_layer(p, dn_conv_w, alog_rows, dt_rows, dn_nw, l, B, T)
        y_pool = _pool(p, pool_w, pscale, l, T, tm)
        y_swa = _swa(p, cos, sin, sink_rows, l, T, tm)
        x2, h = _out_proj(y_dn, y_pool, y_swa, w_out16, x2, g_mix_post, g_ffn_pre, l, tm)
        g = _ffn_up(h, ffn_w_up, ffn_conv_w, conv_b, l, T, min(FFN_TILE, T), FFN_PANEL)
        x2, h = _ffn_down(g, w_down16, x2, g_ffn_post, g_mix_pre, l, min(l + 1, depth - 1),
                          min(FFN_DOWN_ROWS, T))
    return x2.reshape(B, T, D).astype(x.dtype)
```

```python
import functools
import math

import numpy as np
import jax
import jax.numpy as jnp
from jax import lax
from jax.experimental import pallas as pl
from jax.experimental.pallas import tpu as pltpu

HEAD_DIM = 128
DN_HEADS = 6
DN_CONV = 4
DN_CHUNK = 64
POOL_WINDOWS = (2, 4, 8, 16)
POOL_GROUPS = 4
POOL_GROUP_DIM = 128
SWA_HEADS = 6
SWA_KV_HEADS = 2
SWA_WINDOW = 128
SWA_BLOCK = 128
ROPE_THETA = 10000.0
FFN_CONV = 3
NORM_EPS = 1e-6

DN_W = DN_HEADS * HEAD_DIM
POOL_W = POOL_GROUPS * POOL_GROUP_DIM
SWA_W = SWA_HEADS * HEAD_DIM
SWA_KV_W = SWA_KV_HEADS * HEAD_DIM
SWA_GROUP_W = SWA_W // SWA_KV_HEADS
LANES = 128
SUBLANES = 8

R_QKV = 0
R_Z = R_QKV + 3 * DN_W
R_B = R_Z + DN_W
R_POOL = R_B + 2 * DN_HEADS
R_SQ = R_POOL + POOL_W
R_SK = R_SQ + SWA_W
R_SV = R_SK + SWA_KV_W
R_W = R_SV + SWA_KV_W
P_Z = 0
P_SQ = P_Z + DN_W
P_POOL = P_SQ + SWA_W
P_SK = P_POOL + POOL_W
P_SV = P_SK + SWA_KV_W
P_BA = P_SV + SWA_KV_W
P_QKV = P_BA + LANES
P_W = P_QKV + 3 * DN_W
assert P_Z % DN_W == 0 and P_SQ % SWA_GROUP_W == 0 and P_POOL % POOL_W == 0
assert P_SK % HEAD_DIM == 0 and P_SV % HEAD_DIM == 0 and P_BA % LANES == 0 and P_QKV % LANES == 0

VMEM_LIMIT = 56 * 1024 * 1024

ROW_TILE = 512
MIXER_ROWS = 1024
IN_PROJ_ROWS = 1024
IN_PROJ_PANELS = 3
FFN_PANEL = 512
FFN_DOWN_ROWS = 256
assert (P_W // IN_PROJ_PANELS) % LANES == 0 and P_W % IN_PROJ_PANELS == 0

F32 = jnp.float32
BF16 = jnp.bfloat16


def _cparams(sem):
    return pltpu.CompilerParams(dimension_semantics=sem, vmem_limit_bytes=VMEM_LIMIT)


def _rms(v, gain):
    return v * lax.rsqrt(jnp.mean(v * v, axis=-1, keepdims=True) + NORM_EPS) * gain


def _sigmoid(v):
    return 1.0 / (1.0 + jnp.exp(-v))


def _bdot(a, b):
    return jnp.einsum("gik,gkj->gij", a.astype(BF16), b.astype(BF16), preferred_element_type=F32)


def _bdot_nt(a, b):
    return jnp.einsum("gik,gjk->gij", a.astype(BF16), b.astype(BF16), preferred_element_type=F32)


def _win_kernel(src_ref, w_ref, o_ref):
    L = w_ref.shape[1]
    row = lax.broadcasted_iota(jnp.int32, (w_ref.shape[0], w_ref.shape[2]), 0)
    keep = jnp.logical_or(pl.program_id(0) != P_BA // LANES, row < 2 * DN_HEADS)
    for l in range(L):
        o_ref[l] = jnp.where(keep, w_ref[:, l, :], 0.0).astype(o_ref.dtype)


def _prep_w_in(w_in):
    L, K, _ = w_in.shape
    wt = jnp.transpose(w_in, (2, 0, 1))
    sections = ((P_Z, R_Z, DN_W), (P_SQ, R_SQ, SWA_W), (P_POOL, R_POOL, POOL_W), (P_SK, R_SK, SWA_KV_W),
                (P_SV, R_SV, SWA_KV_W), (P_BA, R_B, LANES), (P_QKV, R_QKV, 3 * DN_W))
    src = np.zeros((P_W // LANES,), np.int32)
    for p0, r0, width in sections:
        for k in range(width // LANES):
            src[p0 // LANES + k] = r0 + k * LANES
    return pl.pallas_call(
        _win_kernel,
        grid_spec=pltpu.PrefetchScalarGridSpec(
            num_scalar_prefetch=1,
            grid=(P_W // LANES,),
            in_specs=[pl.BlockSpec((pl.Element(LANES), pl.Element(L), pl.Element(K)),
                                   lambda c, src_ref: (src_ref[c], 0, 0))],
            out_specs=pl.BlockSpec((L, LANES, K), lambda c, src_ref: (0, c, 0)),
        ),
        out_shape=jax.ShapeDtypeStruct((L, P_W, K), BF16),
        compiler_params=_cparams(("parallel",)),
        name="prep_w_in",
    )(jnp.asarray(src), wt)


def _prenorm_kernel(x_ref, g_ref, h_ref):
    h_ref[...] = _rms(x_ref[...], g_ref[...]).astype(h_ref.dtype)


def _prenorm(x2, gains, tm):
    M, D = x2.shape
    return pl.pallas_call(
        _prenorm_kernel,
        grid=(M // tm,),
        in_specs=[pl.BlockSpec((tm, D), lambda i: (i, 0)), pl.BlockSpec((None, 1, D), lambda i: (0, 0, 0))],
        out_specs=pl.BlockSpec((tm, D), lambda i: (i, 0)),
        out_shape=jax.ShapeDtypeStruct((M, D), BF16),
        compiler_params=_cparams(("parallel",)),
        name="prenorm",
    )(x2, gains)


def _rope_kernel(pos_ref, freq_ref, cos_ref, sin_ref):
    ang = pos_ref[...] * freq_ref[...]
    lane = lax.broadcasted_iota(jnp.int32, ang.shape, 1)
    cos_ref[...] = jnp.cos(ang)
    sin_ref[...] = jnp.where(lane < HEAD_DIM // 2, -jnp.sin(ang), jnp.sin(ang))


def _rope_tables(positions, tm):
    M = positions.size
    pos = positions.astype(F32).reshape(M, 1)
    inv = (1.0 / (ROPE_THETA ** (np.arange(0, HEAD_DIM, 2, dtype=np.float32) / HEAD_DIM))).astype(np.float32)
    freq = jnp.asarray(np.concatenate([inv, inv])[None, :])
    return pl.pallas_call(
        _rope_kernel,
        grid=(M // tm,),
        in_specs=[pl.BlockSpec((tm, 1), lambda i: (i, 0)), pl.BlockSpec((1, HEAD_DIM), lambda i: (0, 0))],
        out_specs=[pl.BlockSpec((tm, HEAD_DIM), lambda i: (i, 0))] * 2,
        out_shape=[jax.ShapeDtypeStruct((M, HEAD_DIM), F32)] * 2,
        compiler_params=_cparams(("parallel",)),
        name="rope_tables",
    )(pos, freq)


def _mm_kernel(a_ref, wt_ref, wo_ref, wd_ref, o_ref, wo16_ref, wd16_ref):
    o_ref[...] = lax.dot_general(a_ref[...], wt_ref[...], (((1,), (1,)), ((), ())), preferred_element_type=F32)
    wo16_ref[...] = wo_ref[...].astype(wo16_ref.dtype)
    wd16_ref[...] = wd_ref[...].astype(wd16_ref.dtype)


def _cast_slices(rows, steps):
    return max(n for n in range(1, steps + 1) if rows % n == 0 and (rows // n) % (2 * SUBLANES) == 0)


def _in_proj(h, wt, w_out, w_down, l, tm, tn):
    M, K = h.shape
    N = wt.shape[1]
    nj, nt = N // tn, M // tm
    specs, shapes = [], []
    for w in (w_out, w_down):
        rows, cols = w.shape[1], w.shape[2]
        n = _cast_slices(rows, nj * nt)
        idx = lambda j, i, n=n: jnp.minimum(j * nt + i, n - 1)
        specs.append((pl.BlockSpec((None, rows // n, cols), lambda j, i, idx=idx: (l, idx(j, i), 0)),
                      pl.BlockSpec((rows // n, cols), lambda j, i, idx=idx: (idx(j, i), 0))))
        shapes.append(jax.ShapeDtypeStruct((rows, cols), BF16))
    return pl.pallas_call(
        _mm_kernel,
        grid=(nj, nt),
        in_specs=[pl.BlockSpec((tm, K), lambda j, i: (i, 0)), pl.BlockSpec((None, tn, K), lambda j, i: (l, j, 0)),
                  specs[0][0], specs[1][0]],
        out_specs=[pl.BlockSpec((tm, tn), lambda j, i: (i, j)), specs[0][1], specs[1][1]],
        out_shape=[jax.ShapeDtypeStruct((M, N), F32)] + shapes,
        compiler_params=_cparams(("arbitrary", "arbitrary")),
        name="in_proj",
    )(h, wt, w_out, w_down)


def _gate_kernel(ba_ref, alog_ref, dt_ref, beta_ref, gc_ref):
    p = ba_ref[...]
    beta_ref[...] = _sigmoid(p)
    sp = p + dt_ref[...]
    softplus = jnp.maximum(sp, 0.0) + jnp.log(1.0 + jnp.exp(-jnp.abs(sp)))
    g = -jnp.exp(alog_ref[...]) * softplus
    row = lax.broadcasted_iota(jnp.int32, g.shape, 0) % DN_CHUNK
    s = 1
    while s < DN_CHUNK:
        g = g + jnp.where(row >= s, pltpu.roll(g, s, axis=0), 0.0)
        s *= 2
    gc_ref[...] = g


def _gates(p, alog_rows, dt_rows, l, B, T):
    M = p.shape[0]
    cb = P_BA // LANES
    vec = pl.BlockSpec((None, 1, LANES), lambda b: (l, 0, 0))
    return pl.pallas_call(
        _gate_kernel,
        grid=(B,),
        in_specs=[pl.BlockSpec((T, LANES), lambda b: (b, cb)), vec, vec],
        out_specs=[pl.BlockSpec((T, LANES), lambda b: (b, 0))] * 2,
        out_shape=[jax.ShapeDtypeStruct((M, LANES), F32)] * 2,
        compiler_params=_cparams(("parallel",)),
        name="dn_gates",
    )(p, alog_rows, dt_rows)


def _dn_prep_kernel(q_ref, k_ref, v_ref, qh_ref, kh_ref, vh_ref, beta_ref, gc_ref, gcrow_ref,
                    cwq_ref, cwk_ref, cwv_ref, wq_ref, u_ref, kd_ref, a_ref, c_ref, *, tiles_per_seq):
    h = pl.program_id(1)
    first = (pl.program_id(2) % tiles_per_seq) == 0
    tt = q_ref.shape[0]
    C = DN_CHUNK
    G = tt // C
    D = HEAD_DIM
    pad = SUBLANES

    def conv_silu(x_ref, halo_ref, cw_ref):
        halo = jnp.where(first, 0.0, halo_ref[...])
        ext = jnp.concatenate([halo, x_ref[...]], axis=0)
        c0, c1, c2, c3 = (cw_ref[j:j + 1, :] for j in range(DN_CONV))
        back1 = pltpu.roll(ext, 1, axis=0)
        z = ext * c1 + back1 * c0
        acc = (ext * c3 + back1 * c2 + pltpu.roll(z, 2, axis=0))[pad:pad + tt]
        return acc * _sigmoid(acc)

    q = conv_silu(q_ref, qh_ref, cwq_ref)
    k = conv_silu(k_ref, kh_ref, cwk_ref)
    v = conv_silu(v_ref, vh_ref, cwv_ref)
    q = q * lax.rsqrt(jnp.sum(q * q, axis=-1, keepdims=True) + NORM_EPS) * (D ** -0.5)
    k = k * lax.rsqrt(jnp.sum(k * k, axis=-1, keepdims=True) + NORM_EPS)
    lane = lax.broadcasted_iota(jnp.int32, (tt, LANES), 1)
    beta = jnp.sum(jnp.where(lane == h, beta_ref[...], 0.0), axis=-1, keepdims=True)
    gcc = jnp.sum(jnp.where(lane == h + DN_HEADS, gc_ref[...], 0.0), axis=-1, keepdims=True)
    eg = jnp.exp(gcc)
    kb = k * beta
    vb = v * beta

    q3 = q.reshape(G, C, D)
    k3 = k.reshape(G, C, D)
    kb3 = kb.reshape(G, C, D)
    gcc3 = gcc.reshape(G, C, 1)
    gcr3 = gcrow_ref[...]
    ii = lax.broadcasted_iota(jnp.int32, (G, C, C), 1)
    jj = lax.broadcasted_iota(jnp.int32, (G, C, C), 2)
    decay = jnp.exp(jnp.where(ii >= jj, gcc3 - gcr3, -jnp.inf))
    eye = (ii == jj).astype(F32)
    kq = _bdot_nt(jnp.concatenate([kb3, q3], axis=1), k3)
    nil = jnp.where(ii > jj, -kq[:, 0:C] * decay, 0.0)
    inv = eye + nil
    powk = _bdot(nil, nil)
    for _ in range(int(math.log2(C)) - 2):
        both = _bdot(jnp.concatenate([inv, powk], axis=1), powk)
        inv = inv + both[:, 0:C]
        powk = both[:, C:2 * C]
    inv = inv + _bdot(inv, powk)
    rhs = jnp.concatenate([vb, kb * eg], axis=-1).reshape(G, C, 2 * D)
    uw = _bdot(inv, rhs)
    glast = gcr3[:, :, C - 1:C]
    u_ref[...] = uw[:, :, :D]
    wq_ref[:, 0:C, :] = uw[:, :, D:].astype(wq_ref.dtype)
    wq_ref[:, C:2 * C, :] = (q * eg).reshape(G, C, D).astype(wq_ref.dtype)
    kd_ref[...] = (k3 * jnp.exp(glast - gcc3)).astype(kd_ref.dtype)
    a_ref[...] = (kq[:, C:2 * C] * decay).astype(a_ref.dtype)
    c_ref[...] = jnp.broadcast_to(jnp.exp(glast), c_ref.shape)


def _dn_prep(p, beta, gc, gcrow, conv_w, l, B, T, tt):
    H = DN_HEADS
    C = DN_CHUNK
    N = T // C
    G = tt // C
    nt = T // tt
    hb = tt // SUBLANES
    qb, kb, vb = P_QKV // LANES, (P_QKV + DN_W) // LANES, (P_QKV + 2 * DN_W) // LANES
    tok = lambda off: pl.BlockSpec((tt, HEAD_DIM), lambda b, h, i: (b * nt + i, off + h))
    halo = lambda off: pl.BlockSpec((SUBLANES, HEAD_DIM),
                                    lambda b, h, i: (jnp.maximum((b * nt + i) * hb - 1, 0), off + h))
    gate = pl.BlockSpec((tt, LANES), lambda b, h, i: (b * nt + i, 0))
    cw = lambda off: pl.BlockSpec((None, DN_CONV, HEAD_DIM), lambda b, h, i: (l, 0, off + h))
    chunked = lambda r, c: pl.BlockSpec((None, None, G, r, c), lambda b, h, i: (b, h, i, 0, 0))
    return pl.pallas_call(
        functools.partial(_dn_prep_kernel, tiles_per_seq=nt),
        grid=(B, H, nt),
        in_specs=[tok(qb), tok(kb), tok(vb), halo(qb), halo(kb), halo(vb), gate, gate, chunked(1, C),
                  cw(0), cw(H), cw(2 * H)],
        out_specs=[chunked(2 * C, HEAD_DIM), chunked(C, HEAD_DIM), chunked(C, HEAD_DIM), chunked(C, C),
                   chunked(1, LANES)],
        out_shape=[jax.ShapeDtypeStruct((B, H, N, 2 * C, HEAD_DIM), BF16),
                   jax.ShapeDtypeStruct((B, H, N, C, HEAD_DIM), F32),
                   jax.ShapeDtypeStruct((B, H, N, C, HEAD_DIM), BF16),
                   jax.ShapeDtypeStruct((B, H, N, C, C), BF16),
                   jax.ShapeDtypeStruct((B, H, N, 1, LANES), F32)],
        compiler_params=_cparams(("parallel", "parallel", "parallel")),
        name="dn_prep",
    )(p, p, p, p, p, p, beta, gc, gcrow, conv_w, conv_w, conv_w)


def _dn_scan_kernel(wq_ref, u_ref, kd_ref, a_ref, c_ref, z_ref, nw_ref, y_ref, s_ref):
    B, H, G = wq_ref.shape[0], wq_ref.shape[1], wq_ref.shape[2]
    C = DN_CHUNK
    D = HEAD_DIM

    @pl.when(pl.program_id(0) == 0)
    def _():
        s_ref[...] = jnp.zeros_like(s_ref)

    nw = nw_ref[...]
    chains = [(b, h) for b in range(B) for h in range(H)]

    def body(n, carry):
        r0 = pl.multiple_of(n * C, C)
        S = [s_ref[b * H + h] for b, h in chains]
        m1 = [jnp.dot(wq_ref[b, h, n], S[i].astype(BF16), preferred_element_type=F32)
              for i, (b, h) in enumerate(chains)]
        v_new = [(u_ref[b, h, n] - m1[i][0:C]).astype(BF16) for i, (b, h) in enumerate(chains)]
        o = [m1[i][C:2 * C] + jnp.dot(a_ref[b, h, n], v_new[i], preferred_element_type=F32)
             for i, (b, h) in enumerate(chains)]
        kv = [lax.dot_general(kd_ref[b, h, n], v_new[i], (((0,), (0,)), ((), ())), preferred_element_type=F32)
              for i, (b, h) in enumerate(chains)]
        for i, (b, h) in enumerate(chains):
            s_ref[b * H + h] = S[i] * c_ref[b, h, n] + kv[i]
            zz = z_ref[b, pl.ds(r0, C), h * D:(h + 1) * D]
            y_ref[b, pl.ds(r0, C), h * D:(h + 1) * D] = (_rms(o[i], nw) * (zz * _sigmoid(zz))).astype(y_ref.dtype)
        return carry

    lax.fori_loop(0, G, body, 0)


def _dn_scan(wq, u, kd, a, c, p, norm_w, l, B, T, tt):
    H = DN_HEADS
    C = DN_CHUNK
    G = tt // C
    nt = T // tt
    chunked = lambda r, cc: pl.BlockSpec((B, H, G, r, cc), lambda i: (0, 0, i, 0, 0))
    y = pl.pallas_call(
        _dn_scan_kernel,
        grid=(nt,),
        in_specs=[chunked(2 * C, HEAD_DIM), chunked(C, HEAD_DIM), chunked(C, HEAD_DIM), chunked(C, C),
                  chunked(1, LANES),
                  pl.BlockSpec((B, tt, DN_W), lambda i: (0, i, P_Z // DN_W)),
                  pl.BlockSpec((None, 1, HEAD_DIM), lambda i: (l, 0, 0))],
        out_specs=pl.BlockSpec((B, tt, DN_W), lambda i: (0, i, 0)),
        out_shape=jax.ShapeDtypeStruct((B, T, DN_W), BF16),
        scratch_shapes=[pltpu.VMEM((B * H, HEAD_DIM, HEAD_DIM), F32)],
        compiler_params=_cparams(("arbitrary",)),
        name="dn_scan",
    )(wq, u, kd, a, c, p.reshape(B, T, p.shape[1]), norm_w)
    return y.reshape(B * T, DN_W)


DN_PREP_ROWS = 2048
DN_SCAN_ROWS = 512


def _dn_layer(p, conv_w, alog_rows, dt_rows, norm_w, l, B, T):
    N = T // DN_CHUNK
    beta, gc = _gates(p, alog_rows, dt_rows, l, B, T)
    gcrow = gc[:, DN_HEADS:2 * DN_HEADS].reshape(B, N, DN_CHUNK, DN_HEADS).transpose(0, 3, 1, 2)
    gcrow = gcrow.reshape(B, DN_HEADS, N, 1, DN_CHUNK)
    wq, u, kd, a, c = _dn_prep(p, beta, gc, gcrow, conv_w, l, B, T, min(DN_PREP_ROWS, T))
    return _dn_scan(wq, u, kd, a, c, p, norm_w, l, B, T, min(DN_SCAN_ROWS, T))


POOL_HALO = 16


def _pool_kernel(x_ref, halo_ref, w_ref, scale_ref, o_ref, *, tiles_per_seq):
    i = pl.program_id(0)
    tm = x_ref.shape[0]
    t0 = (i % tiles_per_seq) * tm
    first = (i % tiles_per_seq) == 0
    pos = t0 + 1 + lax.broadcasted_iota(jnp.int32, (tm, POOL_GROUP_DIM), 0)
    for gi, win in enumerate(POOL_WINDOWS):
        cols = slice(gi * POOL_GROUP_DIM, (gi + 1) * POOL_GROUP_DIM)
        halo = jnp.where(first, 0.0, halo_ref[:, cols])
        xg = x_ref[:, cols]
        ext = jnp.concatenate([halo, xg], axis=0)
        acc = ext
        s = 1
        while s < win:
            acc = acc + pltpu.roll(acc, s, axis=0)
            s *= 2
        cnt = jnp.minimum(pos, win).astype(F32)
        y = acc[POOL_HALO:] / cnt - xg
        y = jnp.dot(y.astype(BF16), w_ref[gi].astype(BF16), preferred_element_type=F32)
        o_ref[:, cols] = (y * scale_ref[:, cols]).astype(o_ref.dtype)


def _pool(p, pool_w, pool_scale, l, T, tm):
    M = p.shape[0]
    cb = P_POOL // POOL_W
    hb = tm // POOL_HALO
    return pl.pallas_call(
        functools.partial(_pool_kernel, tiles_per_seq=T // tm),
        grid=(M // tm,),
        in_specs=[pl.BlockSpec((tm, POOL_W), lambda i: (i, cb)),
                  pl.BlockSpec((POOL_HALO, POOL_W), lambda i: (jnp.maximum(i * hb - 1, 0), cb)),
                  pl.BlockSpec((None, POOL_GROUPS, POOL_GROUP_DIM, POOL_GROUP_DIM), lambda i: (l, 0, 0, 0)),
                  pl.BlockSpec((None, 1, POOL_W), lambda i: (l, 0, 0))],
        out_specs=pl.BlockSpec((tm, POOL_W), lambda i: (i, 0)),
        out_shape=jax.ShapeDtypeStruct((M, POOL_W), BF16),
        compiler_params=_cparams(("parallel",)),
        name="pool",
    )(p, p, pool_w, pool_scale)


def _swa_kernel(q_ref, k_ref, kp_ref, v_ref, vp_ref, cos_ref, sin_ref, cosp_ref, sinp_ref, sink_ref, o_ref,
                *, tiles_per_seq):
    i = pl.program_id(1)
    first = (i % tiles_per_seq) == 0
    tm = q_ref.shape[0]
    blk = SWA_BLOCK
    G = SWA_HEADS // SWA_KV_HEADS
    D = HEAD_DIM

    def rope(x, cos, sin):
        return x * cos + pltpu.roll(x, D // 2, axis=1) * sin

    cos = cos_ref[...]
    sin = sin_ref[...]
    cosx = jnp.concatenate([cosp_ref[...], cos], axis=0)
    sinx = jnp.concatenate([sinp_ref[...], sin], axis=0)
    kx = rope(jnp.concatenate([kp_ref[...], k_ref[...]], axis=0), cosx, sinx).astype(BF16)
    vx = jnp.concatenate([vp_ref[...], v_ref[...]], axis=0).astype(BF16)
    vx = jnp.concatenate([vx, jnp.ones_like(vx)], axis=1)
    scale = D ** -0.5
    qh = [(rope(q_ref[:, g * D:(g + 1) * D], cos, sin) * scale).astype(BF16) for g in range(G)]

    ri = lax.broadcasted_iota(jnp.int32, (G * blk, 2 * blk), 0)
    ii = ri % blk
    jj = lax.broadcasted_iota(jnp.int32, (G * blk, 2 * blk), 1)
    band = jnp.logical_or(jnp.logical_and(jj < blk, jj > ii), jnp.logical_and(jj >= blk, jj - blk <= ii))
    no_prev = jnp.where(first, blk, 0)
    band_first = jnp.logical_and(band, jj >= no_prev)
    sink = sink_ref[:, 0:1]
    for g in range(1, G):
        sink = jnp.where(ri[:, 0:1] >= g * blk, sink_ref[:, g:g + 1], sink)
    for b in range(tm // blk):
        qs = jnp.concatenate([q[b * blk:(b + 1) * blk] for q in qh], axis=0)
        keys = kx[b * blk:(b + 2) * blk]
        s = lax.dot_general(qs, keys, (((1,), (1,)), ((), ())), preferred_element_type=F32)
        s = jnp.where(band_first if b == 0 else band, s, -jnp.inf)
        m = jnp.maximum(jnp.max(s, axis=-1, keepdims=True), sink)
        e = jnp.exp(s - m).astype(BF16)
        pv = jnp.dot(e, vx[b * blk:(b + 2) * blk], preferred_element_type=F32)
        o = pv[:, 0:D] / (pv[:, D:2 * D] + jnp.exp(sink - m))
        for g in range(G):
            o_ref[b * blk:(b + 1) * blk, g * D:(g + 1) * D] = o[g * blk:(g + 1) * blk].astype(o_ref.dtype)


def _swa(p, cos, sin, sink_rows, l, T, tm):
    M = p.shape[0]
    blk = SWA_BLOCK
    r = tm // blk
    qb, kb, vb = P_SQ // SWA_GROUP_W, P_SK // HEAD_DIM, P_SV // HEAD_DIM
    prev = lambda i: jnp.maximum(i * r - 1, 0)
    return pl.pallas_call(
        functools.partial(_swa_kernel, tiles_per_seq=T // tm),
        grid=(SWA_KV_HEADS, M // tm),
        in_specs=[pl.BlockSpec((tm, SWA_GROUP_W), lambda kv, i: (i, qb + kv)),
                  pl.BlockSpec((tm, HEAD_DIM), lambda kv, i: (i, kb + kv)),
                  pl.BlockSpec((blk, HEAD_DIM), lambda kv, i: (prev(i), kb + kv)),
                  pl.BlockSpec((tm, HEAD_DIM), lambda kv, i: (i, vb + kv)),
                  pl.BlockSpec((blk, HEAD_DIM), lambda kv, i: (prev(i), vb + kv)),
                  pl.BlockSpec((tm, HEAD_DIM), lambda kv, i: (i, 0)),
                  pl.BlockSpec((tm, HEAD_DIM), lambda kv, i: (i, 0)),
                  pl.BlockSpec((blk, HEAD_DIM), lambda kv, i: (prev(i), 0)),
                  pl.BlockSpec((blk, HEAD_DIM), lambda kv, i: (prev(i), 0)),
                  pl.BlockSpec((None, None, 1, LANES), lambda kv, i: (l, kv, 0, 0))],
        out_specs=pl.BlockSpec((tm, SWA_GROUP_W), lambda kv, i: (i, kv)),
        out_shape=jax.ShapeDtypeStruct((M, SWA_W), BF16),
        compiler_params=_cparams(("parallel", "parallel")),
        name="swa",
    )(p, p, p, p, p, cos, sin, cos, sin, sink_rows)


OUT_ROWS = 256


def _outproj_kernel(ydn_ref, ypool_ref, yswa_ref, w_ref, x_ref, gpost_ref, gffn_ref, xo_ref, h_ref):
    for r0 in range(0, x_ref.shape[0], OUT_ROWS):
        rows = slice(r0, r0 + OUT_ROWS)
        y = jnp.concatenate([ydn_ref[rows, :], ypool_ref[rows, :], yswa_ref[rows, :]], axis=1)
        mix = jnp.dot(y, w_ref[...], preferred_element_type=F32)
        xn = x_ref[rows, :] + _rms(mix, gpost_ref[...])
        xo_ref[rows, :] = xn
        h_ref[rows, :] = _rms(xn, gffn_ref[...]).astype(h_ref.dtype)


def _out_proj(ydn, ypool, yswa, w_out, x2, gpost, gffn, l, tm):
    M, D = x2.shape
    row = lambda w: pl.BlockSpec((tm, w), lambda i: (i, 0))
    vec = pl.BlockSpec((None, 1, D), lambda i: (l, 0, 0))
    return pl.pallas_call(
        _outproj_kernel,
        grid=(M // tm,),
        in_specs=[row(DN_W), row(POOL_W), row(SWA_W),
                  pl.BlockSpec(w_out.shape, lambda i: (0, 0)), row(D), vec, vec],
        out_specs=[row(D), row(D)],
        out_shape=[jax.ShapeDtypeStruct((M, D), F32), jax.ShapeDtypeStruct((M, D), BF16)],
        compiler_params=_cparams(("parallel",)),
        name="out_proj",
    )(ydn, ypool, yswa, w_out, x2, gpost, gffn)


FFN_TILE = 1024
FFN_ROWS = 256


def _ffn_chunks(tm):
    return (FFN_ROWS,) * (tm // FFN_ROWS) if tm % FFN_ROWS == 0 else (tm,)


def _ffn_up_kernel(h_ref, wa_ref, wb_ref, cwa_ref, cwb_ref, cba_ref, cbb_ref, g_ref, wa16, wb16, ua, ub,
                   *, tiles_per_seq):
    i = pl.program_id(1)
    tm = h_ref.shape[0]
    pad = SUBLANES

    @pl.when(i == 0)
    def _():
        wa16[...] = wa_ref[...].astype(BF16)
        wb16[...] = wb_ref[...].astype(BF16)

    @pl.when(i % tiles_per_seq == 0)
    def _():
        ua[0:pad, :] = jnp.zeros((pad, ua.shape[1]), F32)
        ub[0:pad, :] = jnp.zeros((pad, ub.shape[1]), F32)

    hb = h_ref[...]
    ua[pad:pad + tm, :] = jnp.dot(hb, wa16[...], preferred_element_type=F32)
    ub[pad:pad + tm, :] = jnp.dot(hb, wb16[...], preferred_element_type=F32)

    def conv(scr, cw_ref, cb_ref, r0, rows):
        acc = cb_ref[...] + scr[pad + r0:pad + r0 + rows, :] * cw_ref[FFN_CONV - 1:FFN_CONV, :]
        for s in range(1, FFN_CONV):
            acc = acc + scr[pad + r0 - s:pad + r0 - s + rows, :] * cw_ref[FFN_CONV - 1 - s:FFN_CONV - s, :]
        return acc

    r0 = 0
    for rows in _ffn_chunks(tm):
        a = conv(ua, cwa_ref, cba_ref, r0, rows)
        b = conv(ub, cwb_ref, cbb_ref, r0, rows)
        g_ref[r0:r0 + rows, :] = (a * _sigmoid(a) * b).astype(g_ref.dtype)
        r0 += rows
    ua[0:pad, :] = ua[tm:tm + pad, :]
    ub[0:pad, :] = ub[tm:tm + pad, :]


def _ffn_up(h, w_up, conv_w, conv_b, l, T, tm, tn):
    M, D = h.shape
    F = w_up.shape[2] // 2
    nj = F // tn
    return pl.pallas_call(
        functools.partial(_ffn_up_kernel, tiles_per_seq=T // tm),
        grid=(nj, M // tm),
        in_specs=[pl.BlockSpec((tm, D), lambda j, i: (i, 0)),
                  pl.BlockSpec((None, D, tn), lambda j, i: (l, 0, j)),
                  pl.BlockSpec((None, D, tn), lambda j, i: (l, 0, nj + j)),
                  pl.BlockSpec((None, FFN_CONV, tn), lambda j, i: (l, 0, j)),
                  pl.BlockSpec((None, FFN_CONV, tn), lambda j, i: (l, 0, nj + j)),
                  pl.BlockSpec((None, 1, tn), lambda j, i: (l, 0, j)),
                  pl.BlockSpec((None, 1, tn), lambda j, i: (l, 0, nj + j))],
        out_specs=pl.BlockSpec((tm, tn), lambda j, i: (i, j)),
        out_shape=jax.ShapeDtypeStruct((M, F), BF16),
        scratch_shapes=[pltpu.VMEM((D, tn), BF16)] * 2 + [pltpu.VMEM((tm + SUBLANES, tn), F32)] * 2,
        compiler_params=_cparams(("parallel", "arbitrary")),
        name="ffn_up",
    )(h, w_up, w_up, conv_w, conv_w, conv_b, conv_b)


DOWN_ROWS = 128


def _ffn_down_kernel(g_ref, w_ref, x_ref, gpost_ref, gnext_ref, xo_ref, h_ref):
    for r0 in range(0, x_ref.shape[0], DOWN_ROWS):
        rows = slice(r0, r0 + DOWN_ROWS)
        f = jnp.dot(g_ref[rows, :], w_ref[...], preferred_element_type=F32)
        xn = x_ref[rows, :] + _rms(f, gpost_ref[...])
        xo_ref[rows, :] = xn
        h_ref[rows, :] = _rms(xn, gnext_ref[...]).astype(h_ref.dtype)


def _ffn_down(g, w_down, x2, gpost, gnext, l, lnext, tm):
    M, D = x2.shape
    F = g.shape[1]
    return pl.pallas_call(
        _ffn_down_kernel,
        grid=(M // tm,),
        in_specs=[pl.BlockSpec((tm, F), lambda i: (i, 0)),
                  pl.BlockSpec((F, D), lambda i: (0, 0), pipeline_mode=pl.Buffered(1)),
                  pl.BlockSpec((tm, D), lambda i: (i, 0)),
                  pl.BlockSpec((None, 1, D), lambda i: (l, 0, 0)),
                  pl.BlockSpec((None, 1, D), lambda i: (lnext, 0, 0))],
        out_specs=[pl.BlockSpec((tm, D), lambda i: (i, 0))] * 2,
        out_shape=[jax.ShapeDtypeStruct((M, D), F32), jax.ShapeDtypeStruct((M, D), BF16)],
        compiler_params=_cparams(("parallel",)),
        name="ffn_down",
    )(g, w_down, x2, gpost, gnext)


def _rows(v):
    return v.astype(F32).reshape(v.shape[0], 1, v.shape[1])


def kernel(x, positions, norm_mix_pre, w_in, dn_conv_w, dn_a_log, dn_dt_bias, dn_norm_w, pool_w, pool_scale,
           swa_sinks, w_out, norm_mix_post, norm_ffn_pre, ffn_w_up, ffn_conv_w, ffn_conv_b, ffn_w_down,
           norm_ffn_post):
    B, T, D = x.shape
    depth = w_in.shape[0]
    M = B * T
    tm = min(ROW_TILE, T)
    G = SWA_HEADS // SWA_KV_HEADS
    x2 = x.reshape(M, D).astype(F32)

    w_in16 = _prep_w_in(w_in)
    gate_pad = ((0, 0), (DN_HEADS, LANES - 2 * DN_HEADS))
    alog_rows = _rows(jnp.pad(dn_a_log, gate_pad))
    dt_rows = _rows(jnp.pad(dn_dt_bias, gate_pad))
    sink_rows = jnp.pad(swa_sinks.astype(F32).reshape(depth, SWA_KV_HEADS, 1, G),
                        ((0, 0), (0, 0), (0, 0), (0, LANES - G)))
    g_mix_pre, g_mix_post = _rows(norm_mix_pre), _rows(norm_mix_post)
    g_ffn_pre, g_ffn_post = _rows(norm_ffn_pre), _rows(norm_ffn_post)
    dn_nw, pscale, conv_b = _rows(dn_norm_w), _rows(pool_scale), _rows(ffn_conv_b)

    cos, sin = _rope_tables(positions, tm)
    h = _prenorm(x2, g_mix_pre, tm)
    for l in range(depth):
        p, w_out16, w_down16 = _in_proj(h, w_in16, w_out, ffn_w_down, l, min(IN_PROJ_ROWS, T),
                                        P_W // IN_PROJ_PANELS)
        y_dn = _dn_layer(p, dn_conv_w, alog_rows, dt_rows, dn_nw, l, B, T)
        y_pool = _pool(p, pool_w, pscale, l, T, min(MIXER_ROWS, T))
        y_swa = _swa(p, cos, sin, sink_rows, l, T, min(MIXER_ROWS, T))
        x2, h = _out_proj(y_dn, y_pool, y_swa, w_out16, x2, g_mix_post, g_ffn_pre, l, tm)
        g = _ffn_up(h, ffn_w_up, ffn_conv_w, conv_b, l, T, min(FFN_TILE, T), FFN_PANEL)
        x2, h = _ffn_down(g, w_down16, x2, g_ffn_post, g_mix_pre, l, min(l + 1, depth - 1),
                          min(FFN_DOWN_ROWS, T))
    return x2.reshape(B, T, D).astype(x.dtype)
```

```python
import functools
import math

import numpy as np
import jax
import jax.numpy as jnp
from jax import lax
from jax.experimental import pallas as pl
from jax.experimental.pallas import tpu as pltpu

HEAD_DIM = 128
DN_HEADS = 6
DN_CONV = 4
DN_CHUNK = 64
POOL_WINDOWS = (2, 4, 8, 16)
POOL_GROUPS = 4
POOL_GROUP_DIM = 128
SWA_HEADS = 6
SWA_KV_HEADS = 2
SWA_WINDOW = 128
SWA_BLOCK = 128
ROPE_THETA = 10000.0
FFN_CONV = 3
NORM_EPS = 1e-6

DN_W = DN_HEADS * HEAD_DIM
POOL_W = POOL_GROUPS * POOL_GROUP_DIM
SWA_W = SWA_HEADS * HEAD_DIM
SWA_KV_W = SWA_KV_HEADS * HEAD_DIM
SWA_GROUP_W = SWA_W // SWA_KV_HEADS
LANES = 128
SUBLANES = 8

R_QKV = 0
R_Z = R_QKV + 3 * DN_W
R_B = R_Z + DN_W
R_POOL = R_B + 2 * DN_HEADS
R_SQ = R_POOL + POOL_W
R_SK = R_SQ + SWA_W
R_SV = R_SK + SWA_KV_W
R_W = R_SV + SWA_KV_W
P_Z = 0
P_SQ = P_Z + DN_W
P_POOL = P_SQ + SWA_W
P_SK = P_POOL + POOL_W
P_SV = P_SK + SWA_KV_W
P_BA = P_SV + SWA_KV_W
P_QKV = P_BA + LANES
P_W = P_QKV + 3 * DN_W
assert P_Z % DN_W == 0 and P_SQ % SWA_GROUP_W == 0 and P_POOL % POOL_W == 0
assert P_SK % HEAD_DIM == 0 and P_SV % HEAD_DIM == 0 and P_BA % LANES == 0 and P_QKV % LANES == 0

VMEM_LIMIT = 56 * 1024 * 1024

ROW_TILE = 512
MIXER_ROWS = 2048
IN_PROJ_ROWS = 1024
IN_PROJ_PANELS = 3
FFN_PANEL = 512
FFN_DOWN_ROWS = 256
assert (P_W // IN_PROJ_PANELS) % LANES == 0 and P_W % IN_PROJ_PANELS == 0

F32 = jnp.float32
BF16 = jnp.bfloat16


def _cparams(sem):
    return pltpu.CompilerParams(dimension_semantics=sem, vmem_limit_bytes=VMEM_LIMIT)


def _rms(v, gain):
    return v * lax.rsqrt(jnp.mean(v * v, axis=-1, keepdims=True) + NORM_EPS) * gain


def _sigmoid(v):
    return 1.0 / (1.0 + jnp.exp(-v))


def _bdot(a, b):
    return jnp.einsum("gik,gkj->gij", a.astype(BF16), b.astype(BF16), preferred_element_type=F32)


def _bdot_nt(a, b):
    return jnp.einsum("gik,gjk->gij", a.astype(BF16), b.astype(BF16), preferred_element_type=F32)


def _win_kernel(src_ref, w_ref, o_ref):
    L = w_ref.shape[1]
    row = lax.broadcasted_iota(jnp.int32, (w_ref.shape[0], w_ref.shape[2]), 0)
    keep = jnp.logical_or(pl.program_id(0) != P_BA // LANES, row < 2 * DN_HEADS)
    for l in range(L):
        o_ref[l] = jnp.where(keep, w_ref[:, l, :], 0.0).astype(o_ref.dtype)


def _prep_w_in(w_in):
    L, K, _ = w_in.shape
    wt = jnp.transpose(w_in, (2, 0, 1))
    sections = ((P_Z, R_Z, DN_W), (P_SQ, R_SQ, SWA_W), (P_POOL, R_POOL, POOL_W), (P_SK, R_SK, SWA_KV_W),
                (P_SV, R_SV, SWA_KV_W), (P_BA, R_B, LANES), (P_QKV, R_QKV, 3 * DN_W))
    src = np.zeros((P_W // LANES,), np.int32)
    for p0, r0, width in sections:
        for k in range(width // LANES):
            src[p0 // LANES + k] = r0 + k * LANES
    return pl.pallas_call(
        _win_kernel,
        grid_spec=pltpu.PrefetchScalarGridSpec(
            num_scalar_prefetch=1,
            grid=(P_W // LANES,),
            in_specs=[pl.BlockSpec((pl.Element(LANES), pl.Element(L), pl.Element(K)),
                                   lambda c, src_ref: (src_ref[c], 0, 0))],
            out_specs=pl.BlockSpec((L, LANES, K), lambda c, src_ref: (0, c, 0)),
        ),
        out_shape=jax.ShapeDtypeStruct((L, P_W, K), BF16),
        compiler_params=_cparams(("parallel",)),
        name="prep_w_in",
    )(jnp.asarray(src), wt)


def _prenorm_kernel(x_ref, g_ref, h_ref):
    h_ref[...] = _rms(x_ref[...], g_ref[...]).astype(h_ref.dtype)


def _prenorm(x2, gains, tm):
    M, D = x2.shape
    return pl.pallas_call(
        _prenorm_kernel,
        grid=(M // tm,),
        in_specs=[pl.BlockSpec((tm, D), lambda i: (i, 0)), pl.BlockSpec((None, 1, D), lambda i: (0, 0, 0))],
        out_specs=pl.BlockSpec((tm, D), lambda i: (i, 0)),
        out_shape=jax.ShapeDtypeStruct((M, D), BF16),
        compiler_params=_cparams(("parallel",)),
        name="prenorm",
    )(x2, gains)


def _rope_kernel(pos_ref, freq_ref, cos_ref, sin_ref):
    ang = pos_ref[...] * freq_ref[...]
    lane = lax.broadcasted_iota(jnp.int32, ang.shape, 1)
    cos_ref[...] = jnp.cos(ang)
    sin_ref[...] = jnp.where(lane < HEAD_DIM // 2, -jnp.sin(ang), jnp.sin(ang))


def _rope_tables(positions, tm):
    M = positions.size
    pos = positions.astype(F32).reshape(M, 1)
    inv = (1.0 / (ROPE_THETA ** (np.arange(0, HEAD_DIM, 2, dtype=np.float32) / HEAD_DIM))).astype(np.float32)
    freq = jnp.asarray(np.concatenate([inv, inv])[None, :])
    return pl.pallas_call(
        _rope_kernel,
        grid=(M // tm,),
        in_specs=[pl.BlockSpec((tm, 1), lambda i: (i, 0)), pl.BlockSpec((1, HEAD_DIM), lambda i: (0, 0))],
        out_specs=[pl.BlockSpec((tm, HEAD_DIM), lambda i: (i, 0))] * 2,
        out_shape=[jax.ShapeDtypeStruct((M, HEAD_DIM), F32)] * 2,
        compiler_params=_cparams(("parallel",)),
        name="rope_tables",
    )(pos, freq)


def _mm_kernel(a_ref, wt_ref, wo_ref, wd_ref, o_ref, wo16_ref, wd16_ref):
    o_ref[...] = lax.dot_general(a_ref[...], wt_ref[...], (((1,), (1,)), ((), ())), preferred_element_type=F32)
    wo16_ref[...] = wo_ref[...].astype(wo16_ref.dtype)
    wd16_ref[...] = wd_ref[...].astype(wd16_ref.dtype)


def _cast_slices(rows, steps):
    return max(n for n in range(1, steps + 1) if rows % n == 0 and (rows // n) % (2 * SUBLANES) == 0)


def _in_proj(h, wt, w_out, w_down, l, tm, tn):
    M, K = h.shape
    N = wt.shape[1]
    nj, nt = N // tn, M // tm
    specs, shapes = [], []
    for w in (w_out, w_down):
        rows, cols = w.shape[1], w.shape[2]
        n = _cast_slices(rows, nj * nt)
        idx = lambda j, i, n=n: jnp.minimum(j * nt + i, n - 1)
        specs.append((pl.BlockSpec((None, rows // n, cols), lambda j, i, idx=idx: (l, idx(j, i), 0)),
                      pl.BlockSpec((rows // n, cols), lambda j, i, idx=idx: (idx(j, i), 0))))
        shapes.append(jax.ShapeDtypeStruct((rows, cols), BF16))
    return pl.pallas_call(
        _mm_kernel,
        grid=(nj, nt),
        in_specs=[pl.BlockSpec((tm, K), lambda j, i: (i, 0)), pl.BlockSpec((None, tn, K), lambda j, i: (l, j, 0)),
                  specs[0][0], specs[1][0]],
        out_specs=[pl.BlockSpec((tm, tn), lambda j, i: (i, j)), specs[0][1], specs[1][1]],
        out_shape=[jax.ShapeDtypeStruct((M, N), F32)] + shapes,
        compiler_params=_cparams(("arbitrary", "arbitrary")),
        name="in_proj",
    )(h, wt, w_out, w_down)


def _gate_kernel(ba_ref, alog_ref, dt_ref, beta_ref, gc_ref):
    p = ba_ref[...]
    beta_ref[...] = _sigmoid(p)
    sp = p + dt_ref[...]
    softplus = jnp.maximum(sp, 0.0) + jnp.log(1.0 + jnp.exp(-jnp.abs(sp)))
    g = -jnp.exp(alog_ref[...]) * softplus
    row = lax.broadcasted_iota(jnp.int32, g.shape, 0) % DN_CHUNK
    s = 1
    while s < DN_CHUNK:
        g = g + jnp.where(row >= s, pltpu.roll(g, s, axis=0), 0.0)
        s *= 2
    gc_ref[...] = g


def _gates(p, alog_rows, dt_rows, l, B, T):
    M = p.shape[0]
    cb = P_BA // LANES
    vec = pl.BlockSpec((None, 1, LANES), lambda b: (l, 0, 0))
    return pl.pallas_call(
        _gate_kernel,
        grid=(B,),
        in_specs=[pl.BlockSpec((T, LANES), lambda b: (b, cb)), vec, vec],
        out_specs=[pl.BlockSpec((T, LANES), lambda b: (b, 0))] * 2,
        out_shape=[jax.ShapeDtypeStruct((M, LANES), F32)] * 2,
        compiler_params=_cparams(("parallel",)),
        name="dn_gates",
    )(p, alog_rows, dt_rows)


def _dn_prep_kernel(q_ref, k_ref, v_ref, qh_ref, kh_ref, vh_ref, beta_ref, gc_ref, gcrow_ref,
                    cwq_ref, cwk_ref, cwv_ref, wq_ref, u_ref, kd_ref, a_ref, c_ref, *, tiles_per_seq):
    h = pl.program_id(1)
    first = (pl.program_id(2) % tiles_per_seq) == 0
    tt = q_ref.shape[0]
    C = DN_CHUNK
    G = tt // C
    D = HEAD_DIM
    pad = SUBLANES

    def conv_silu(x_ref, halo_ref, cw_ref):
        halo = jnp.where(first, 0.0, halo_ref[...])
        ext = jnp.concatenate([halo, x_ref[...]], axis=0)
        c0, c1, c2, c3 = (cw_ref[j:j + 1, :] for j in range(DN_CONV))
        back1 = pltpu.roll(ext, 1, axis=0)
        z = ext * c1 + back1 * c0
        acc = (ext * c3 + back1 * c2 + pltpu.roll(z, 2, axis=0))[pad:pad + tt]
        return acc * _sigmoid(acc)

    q = conv_silu(q_ref, qh_ref, cwq_ref)
    k = conv_silu(k_ref, kh_ref, cwk_ref)
    v = conv_silu(v_ref, vh_ref, cwv_ref)
    q = q * lax.rsqrt(jnp.sum(q * q, axis=-1, keepdims=True) + NORM_EPS) * (D ** -0.5)
    k = k * lax.rsqrt(jnp.sum(k * k, axis=-1, keepdims=True) + NORM_EPS)
    lane = lax.broadcasted_iota(jnp.int32, (tt, LANES), 1)
    beta = jnp.sum(jnp.where(lane == h, beta_ref[...], 0.0), axis=-1, keepdims=True)
    gcc = jnp.sum(jnp.where(lane == h + DN_HEADS, gc_ref[...], 0.0), axis=-1, keepdims=True)
    eg = jnp.exp(gcc)
    kb = k * beta
    vb = v * beta

    q3 = q.reshape(G, C, D)
    k3 = k.reshape(G, C, D)
    kb3 = kb.reshape(G, C, D)
    gcc3 = gcc.reshape(G, C, 1)
    gcr3 = gcrow_ref[...]
    ii = lax.broadcasted_iota(jnp.int32, (G, C, C), 1)
    jj = lax.broadcasted_iota(jnp.int32, (G, C, C), 2)
    decay = jnp.exp(jnp.where(ii >= jj, gcc3 - gcr3, -jnp.inf))
    eye = (ii == jj).astype(F32)
    kq = _bdot_nt(jnp.concatenate([kb3, q3], axis=1), k3)
    nil = jnp.where(ii > jj, -kq[:, 0:C] * decay, 0.0)
    inv = eye + nil
    powk = _bdot(nil, nil)
    for _ in range(int(math.log2(C)) - 2):
        both = _bdot(jnp.concatenate([inv, powk], axis=1), powk)
        inv = inv + both[:, 0:C]
        powk = both[:, C:2 * C]
    inv = inv + _bdot(inv, powk)
    rhs = jnp.concatenate([vb, kb * eg], axis=-1).reshape(G, C, 2 * D)
    uw = _bdot(inv, rhs)
    glast = gcr3[:, :, C - 1:C]
    u_ref[...] = uw[:, :, :D]
    wq_ref[:, 0:C, :] = uw[:, :, D:].astype(wq_ref.dtype)
    wq_ref[:, C:2 * C, :] = (q * eg).reshape(G, C, D).astype(wq_ref.dtype)
    kd_ref[...] = (k3 * jnp.exp(glast - gcc3)).astype(kd_ref.dtype)
    a_ref[...] = (kq[:, C:2 * C] * decay).astype(a_ref.dtype)
    c_ref[...] = jnp.broadcast_to(jnp.exp(glast), c_ref.shape)


def _dn_prep(p, beta, gc, gcrow, conv_w, l, B, T, tt):
    H = DN_HEADS
    C = DN_CHUNK
    N = T // C
    G = tt // C
    nt = T // tt
    hb = tt // SUBLANES
    qb, kb, vb = P_QKV // LANES, (P_QKV + DN_W) // LANES, (P_QKV + 2 * DN_W) // LANES
    tok = lambda off: pl.BlockSpec((tt, HEAD_DIM), lambda b, h, i: (b * nt + i, off + h))
    halo = lambda off: pl.BlockSpec((SUBLANES, HEAD_DIM),
                                    lambda b, h, i: (jnp.maximum((b * nt + i) * hb - 1, 0), off + h))
    gate = pl.BlockSpec((tt, LANES), lambda b, h, i: (b * nt + i, 0))
    cw = lambda off: pl.BlockSpec((None, DN_CONV, HEAD_DIM), lambda b, h, i: (l, 0, off + h))
    chunked = lambda r, c: pl.BlockSpec((None, None, G, r, c), lambda b, h, i: (b, h, i, 0, 0))
    return pl.pallas_call(
        functools.partial(_dn_prep_kernel, tiles_per_seq=nt),
        grid=(B, H, nt),
        in_specs=[tok(qb), tok(kb), tok(vb), halo(qb), halo(kb), halo(vb), gate, gate, chunked(1, C),
                  cw(0), cw(H), cw(2 * H)],
        out_specs=[chunked(2 * C, HEAD_DIM), chunked(C, HEAD_DIM), chunked(C, HEAD_DIM), chunked(C, C),
                   chunked(1, LANES)],
        out_shape=[jax.ShapeDtypeStruct((B, H, N, 2 * C, HEAD_DIM), BF16),
                   jax.ShapeDtypeStruct((B, H, N, C, HEAD_DIM), F32),
                   jax.ShapeDtypeStruct((B, H, N, C, HEAD_DIM), BF16),
                   jax.ShapeDtypeStruct((B, H, N, C, C), BF16),
                   jax.ShapeDtypeStruct((B, H, N, 1, LANES), F32)],
        compiler_params=_cparams(("parallel", "parallel", "parallel")),
        name="dn_prep",
    )(p, p, p, p, p, p, beta, gc, gcrow, conv_w, conv_w, conv_w)


def _dn_scan_kernel(wq_ref, u_ref, kd_ref, a_ref, c_ref, z_ref, nw_ref, y_ref, s_ref):
    B, H, G = wq_ref.shape[0], wq_ref.shape[1], wq_ref.shape[2]
    C = DN_CHUNK
    D = HEAD_DIM

    @pl.when(pl.program_id(0) == 0)
    def _():
        s_ref[...] = jnp.zeros_like(s_ref)

    nw = nw_ref[...]
    chains = [(b, h) for b in range(B) for h in range(H)]

    def body(n, carry):
        r0 = pl.multiple_of(n * C, C)
        S = [s_ref[b * H + h] for b, h in chains]
        m1 = [jnp.dot(wq_ref[b, h, n], S[i].astype(BF16), preferred_element_type=F32)
              for i, (b, h) in enumerate(chains)]
        v_new = [(u_ref[b, h, n] - m1[i][0:C]).astype(BF16) for i, (b, h) in enumerate(chains)]
        o = [m1[i][C:2 * C] + jnp.dot(a_ref[b, h, n], v_new[i], preferred_element_type=F32)
             for i, (b, h) in enumerate(chains)]
        kv = [lax.dot_general(kd_ref[b, h, n], v_new[i], (((0,), (0,)), ((), ())), preferred_element_type=F32)
              for i, (b, h) in enumerate(chains)]
        for i, (b, h) in enumerate(chains):
            s_ref[b * H + h] = S[i] * c_ref[b, h, n] + kv[i]
            zz = z_ref[b, pl.ds(r0, C), h * D:(h + 1) * D]
            y_ref[b, pl.ds(r0, C), h * D:(h + 1) * D] = (_rms(o[i], nw) * (zz * _sigmoid(zz))).astype(y_ref.dtype)
        return carry

    lax.fori_loop(0, G, body, 0)


def _dn_scan(wq, u, kd, a, c, p, norm_w, l, B, T, tt):
    H = DN_HEADS
    C = DN_CHUNK
    G = tt // C
    nt = T // tt
    chunked = lambda r, cc: pl.BlockSpec((B, H, G, r, cc), lambda i: (0, 0, i, 0, 0))
    y = pl.pallas_call(
        _dn_scan_kernel,
        grid=(nt,),
        in_specs=[chunked(2 * C, HEAD_DIM), chunked(C, HEAD_DIM), chunked(C, HEAD_DIM), chunked(C, C),
                  chunked(1, LANES),
                  pl.BlockSpec((B, tt, DN_W), lambda i: (0, i, P_Z // DN_W)),
                  pl.BlockSpec((None, 1, HEAD_DIM), lambda i: (l, 0, 0))],
        out_specs=pl.BlockSpec((B, tt, DN_W), lambda i: (0, i, 0)),
        out_shape=jax.ShapeDtypeStruct((B, T, DN_W), BF16),
        scratch_shapes=[pltpu.VMEM((B * H, HEAD_DIM, HEAD_DIM), F32)],
        compiler_params=_cparams(("arbitrary",)),
        name="dn_scan",
    )(wq, u, kd, a, c, p.reshape(B, T, p.shape[1]), norm_w)
    return y.reshape(B * T, DN_W)


DN_PREP_ROWS = 4096
DN_SCAN_ROWS = 512


def _dn_layer(p, conv_w, alog_rows, dt_rows, norm_w, l, B, T):
    N = T // DN_CHUNK
    beta, gc = _gates(p, alog_rows, dt_rows, l, B, T)
    gcrow = gc[:, DN_HEADS:2 * DN_HEADS].reshape(B, N, DN_CHUNK, DN_HEADS).transpose(0, 3, 1, 2)
    gcrow = gcrow.reshape(B, DN_HEADS, N, 1, DN_CHUNK)
    wq, u, kd, a, c = _dn_prep(p, beta, gc, gcrow, conv_w, l, B, T, min(DN_PREP_ROWS, T))
    return _dn_scan(wq, u, kd, a, c, p, norm_w, l, B, T, min(DN_SCAN_ROWS, T))


POOL_HALO = 16


def _pool_kernel(x_ref, halo_ref, w_ref, scale_ref, o_ref, *, tiles_per_seq):
    i = pl.program_id(0)
    tm = x_ref.shape[0]
    t0 = (i % tiles_per_seq) * tm
    first = (i % tiles_per_seq) == 0
    pos = t0 + 1 + lax.broadcasted_iota(jnp.int32, (tm, POOL_GROUP_DIM), 0)
    for gi, win in enumerate(POOL_WINDOWS):
        cols = slice(gi * POOL_GROUP_DIM, (gi + 1) * POOL_GROUP_DIM)
        halo = jnp.where(first, 0.0, halo_ref[:, cols])
        xg = x_ref[:, cols]
        ext = jnp.concatenate([halo, xg], axis=0)
        acc = ext
        s = 1
        while s < win:
            acc = acc + pltpu.roll(acc, s, axis=0)
            s *= 2
        cnt = jnp.minimum(pos, win).astype(F32)
        y = acc[POOL_HALO:] / cnt - xg
        y = jnp.dot(y.astype(BF16), w_ref[gi].astype(BF16), preferred_element_type=F32)
        o_ref[:, cols] = (y * scale_ref[:, cols]).astype(o_ref.dtype)


def _pool(p, pool_w, pool_scale, l, T, tm):
    M = p.shape[0]
    cb = P_POOL // POOL_W
    hb = tm // POOL_HALO
    return pl.pallas_call(
        functools.partial(_pool_kernel, tiles_per_seq=T // tm),
        grid=(M // tm,),
        in_specs=[pl.BlockSpec((tm, POOL_W), lambda i: (i, cb)),
                  pl.BlockSpec((POOL_HALO, POOL_W), lambda i: (jnp.maximum(i * hb - 1, 0), cb)),
                  pl.BlockSpec((None, POOL_GROUPS, POOL_GROUP_DIM, POOL_GROUP_DIM), lambda i: (l, 0, 0, 0)),
                  pl.BlockSpec((None, 1, POOL_W), lambda i: (l, 0, 0))],
        out_specs=pl.BlockSpec((tm, POOL_W), lambda i: (i, 0)),
        out_shape=jax.ShapeDtypeStruct((M, POOL_W), BF16),
        compiler_params=_cparams(("parallel",)),
        name="pool",
    )(p, p, pool_w, pool_scale)


def _swa_kernel(q_ref, k_ref, kp_ref, v_ref, vp_ref, cos_ref, sin_ref, cosp_ref, sinp_ref, sink_ref, o_ref,
                *, tiles_per_seq):
    i = pl.program_id(1)
    first = (i % tiles_per_seq) == 0
    tm = q_ref.shape[0]
    blk = SWA_BLOCK
    G = SWA_HEADS // SWA_KV_HEADS
    D = HEAD_DIM

    def rope(x, cos, sin):
        return x * cos + pltpu.roll(x, D // 2, axis=1) * sin

    cos = cos_ref[...]
    sin = sin_ref[...]
    cosx = jnp.concatenate([cosp_ref[...], cos], axis=0)
    sinx = jnp.concatenate([sinp_ref[...], sin], axis=0)
    kx = rope(jnp.concatenate([kp_ref[...], k_ref[...]], axis=0), cosx, sinx).astype(BF16)
    vx = jnp.concatenate([vp_ref[...], v_ref[...]], axis=0).astype(BF16)
    vx = jnp.concatenate([vx, jnp.ones_like(vx)], axis=1)
    scale = D ** -0.5
    qh = [(rope(q_ref[:, g * D:(g + 1) * D], cos, sin) * scale).astype(BF16) for g in range(G)]

    ri = lax.broadcasted_iota(jnp.int32, (G * blk, 2 * blk), 0)
    ii = ri % blk
    jj = lax.broadcasted_iota(jnp.int32, (G * blk, 2 * blk), 1)
    band = jnp.logical_or(jnp.logical_and(jj < blk, jj > ii), jnp.logical_and(jj >= blk, jj - blk <= ii))
    no_prev = jnp.where(first, blk, 0)
    band_first = jnp.logical_and(band, jj >= no_prev)
    sink = sink_ref[:, 0:1]
    for g in range(1, G):
        sink = jnp.where(ri[:, 0:1] >= g * blk, sink_ref[:, g:g + 1], sink)
    for b in range(tm // blk):
        qs = jnp.concatenate([q[b * blk:(b + 1) * blk] for q in qh], axis=0)
        keys = kx[b * blk:(b + 2) * blk]
        s = lax.dot_general(qs, keys, (((1,), (1,)), ((), ())), preferred_element_type=F32)
        s = jnp.where(band_first if b == 0 else band, s, -jnp.inf)
        m = jnp.maximum(jnp.max(s, axis=-1, keepdims=True), sink)
        e = jnp.exp(s - m).astype(BF16)
        pv = jnp.dot(e, vx[b * blk:(b + 2) * blk], preferred_element_type=F32)
        o = pv[:, 0:D] / (pv[:, D:2 * D] + jnp.exp(sink - m))
        for g in range(G):
            o_ref[b * blk:(b + 1) * blk, g * D:(g + 1) * D] = o[g * blk:(g + 1) * blk].astype(o_ref.dtype)


def _swa(p, cos, sin, sink_rows, l, T, tm):
    M = p.shape[0]
    blk = SWA_BLOCK
    r = tm // blk
    qb, kb, vb = P_SQ // SWA_GROUP_W, P_SK // HEAD_DIM, P_SV // HEAD_DIM
    prev = lambda i: jnp.maximum(i * r - 1, 0)
    return pl.pallas_call(
        functools.partial(_swa_kernel, tiles_per_seq=T // tm),
        grid=(SWA_KV_HEADS, M // tm),
        in_specs=[pl.BlockSpec((tm, SWA_GROUP_W), lambda kv, i: (i, qb + kv)),
                  pl.BlockSpec((tm, HEAD_DIM), lambda kv, i: (i, kb + kv)),
                  pl.BlockSpec((blk, HEAD_DIM), lambda kv, i: (prev(i), kb + kv)),
                  pl.BlockSpec((tm, HEAD_DIM), lambda kv, i: (i, vb + kv)),
                  pl.BlockSpec((blk, HEAD_DIM), lambda kv, i: (prev(i), vb + kv)),
                  pl.BlockSpec((tm, HEAD_DIM), lambda kv, i: (i, 0)),
                  pl.BlockSpec((tm, HEAD_DIM), lambda kv, i: (i, 0)),
                  pl.BlockSpec((blk, HEAD_DIM), lambda kv, i: (prev(i), 0)),
                  pl.BlockSpec((blk, HEAD_DIM), lambda kv, i: (prev(i), 0)),
                  pl.BlockSpec((None, None, 1, LANES), lambda kv, i: (l, kv, 0, 0))],
        out_specs=pl.BlockSpec((tm, SWA_GROUP_W), lambda kv, i: (i, kv)),
        out_shape=jax.ShapeDtypeStruct((M, SWA_W), BF16),
        compiler_params=_cparams(("parallel", "parallel")),
        name="swa",
    )(p, p, p, p, p, cos, sin, cos, sin, sink_rows)


OUT_ROWS = 256


def _outproj_kernel(ydn_ref, ypool_ref, yswa_ref, w_ref, x_ref, gpost_ref, gffn_ref, xo_ref, h_ref):
    for r0 in range(0, x_ref.shape[0], OUT_ROWS):
        rows = slice(r0, r0 + OUT_ROWS)
        y = jnp.concatenate([ydn_ref[rows, :], ypool_ref[rows, :], yswa_ref[rows, :]], axis=1)
        mix = jnp.dot(y, w_ref[...], preferred_element_type=F32)
        xn = x_ref[rows, :] + _rms(mix, gpost_ref[...])
        xo_ref[rows, :] = xn
        h_ref[rows, :] = _rms(xn, gffn_ref[...]).astype(h_ref.dtype)


def _out_proj(ydn, ypool, yswa, w_out, x2, gpost, gffn, l, tm):
    M, D = x2.shape
    row = lambda w: pl.BlockSpec((tm, w), lambda i: (i, 0))
    vec = pl.BlockSpec((None, 1, D), lambda i: (l, 0, 0))
    return pl.pallas_call(
        _outproj_kernel,
        grid=(M // tm,),
        in_specs=[row(DN_W), row(POOL_W), row(SWA_W),
                  pl.BlockSpec(w_out.shape, lambda i: (0, 0)), row(D), vec, vec],
        out_specs=[row(D), row(D)],
        out_shape=[jax.ShapeDtypeStruct((M, D), F32), jax.ShapeDtypeStruct((M, D), BF16)],
        compiler_params=_cparams(("parallel",)),
        name="out_proj",
    )(ydn, ypool, yswa, w_out, x2, gpost, gffn)


FFN_TILE = 1024
FFN_ROWS = 256


def _ffn_chunks(tm):
    return (FFN_ROWS,) * (tm // FFN_ROWS) if tm % FFN_ROWS == 0 else (tm,)


def _ffn_up_kernel(h_ref, wa_ref, wb_ref, cwa_ref, cwb_ref, cba_ref, cbb_ref, g_ref, wa16, wb16, ua, ub,
                   *, tiles_per_seq):
    i = pl.program_id(1)
    tm = h_ref.shape[0]
    pad = SUBLANES

    @pl.when(i == 0)
    def _():
        wa16[...] = wa_ref[...].astype(BF16)
        wb16[...] = wb_ref[...].astype(BF16)

    @pl.when(i % tiles_per_seq == 0)
    def _():
        ua[0:pad, :] = jnp.zeros((pad, ua.shape[1]), F32)
        ub[0:pad, :] = jnp.zeros((pad, ub.shape[1]), F32)

    hb = h_ref[...]
    ua[pad:pad + tm, :] = jnp.dot(hb, wa16[...], preferred_element_type=F32)
    ub[pad:pad + tm, :] = jnp.dot(hb, wb16[...], preferred_element_type=F32)

    def conv(scr, cw_ref, cb_ref, r0, rows):
        acc = cb_ref[...] + scr[pad + r0:pad + r0 + rows, :] * cw_ref[FFN_CONV - 1:FFN_CONV, :]
        for s in range(1, FFN_CONV):
            acc = acc + scr[pad + r0 - s:pad + r0 - s + rows, :] * cw_ref[FFN_CONV - 1 - s:FFN_CONV - s, :]
        return acc

    r0 = 0
    for rows in _ffn_chunks(tm):
        a = conv(ua, cwa_ref, cba_ref, r0, rows)
        b = conv(ub, cwb_ref, cbb_ref, r0, rows)
        g_ref[r0:r0 + rows, :] = (a * _sigmoid(a) * b).astype(g_ref.dtype)
        r0 += rows
    ua[0:pad, :] = ua[tm:tm + pad, :]
    ub[0:pad, :] = ub[tm:tm + pad, :]


def _ffn_up(h, w_up, conv_w, conv_b, l, T, tm, tn):
    M, D = h.shape
    F = w_up.shape[2] // 2
    nj = F // tn
    return pl.pallas_call(
        functools.partial(_ffn_up_kernel, tiles_per_seq=T // tm),
        grid=(nj, M // tm),
        in_specs=[pl.BlockSpec((tm, D), lambda j, i: (i, 0)),
                  pl.BlockSpec((None, D, tn), lambda j, i: (l, 0, j)),
                  pl.BlockSpec((None, D, tn), lambda j, i: (l, 0, nj + j)),
                  pl.BlockSpec((None, FFN_CONV, tn), lambda j, i: (l, 0, j)),
                  pl.BlockSpec((None, FFN_CONV, tn), lambda j, i: (l, 0, nj + j)),
                  pl.BlockSpec((None, 1, tn), lambda j, i: (l, 0, j)),
                  pl.BlockSpec((None, 1, tn), lambda j, i: (l, 0, nj + j))],
        out_specs=pl.BlockSpec((tm, tn), lambda j, i: (i, j)),
        out_shape=jax.ShapeDtypeStruct((M, F), BF16),
        scratch_shapes=[pltpu.VMEM((D, tn), BF16)] * 2 + [pltpu.VMEM((tm + SUBLANES, tn), F32)] * 2,
        compiler_params=_cparams(("parallel", "arbitrary")),
        name="ffn_up",
    )(h, w_up, w_up, conv_w, conv_w, conv_b, conv_b)


DOWN_ROWS = 128


def _ffn_down_kernel(g_ref, w_ref, x_ref, gpost_ref, gnext_ref, xo_ref, h_ref):
    for r0 in range(0, x_ref.shape[0], DOWN_ROWS):
        rows = slice(r0, r0 + DOWN_ROWS)
        f = jnp.dot(g_ref[rows, :], w_ref[...], preferred_element_type=F32)
        xn = x_ref[rows, :] + _rms(f, gpost_ref[...])
        xo_ref[rows, :] = xn
        h_ref[rows, :] = _rms(xn, gnext_ref[...]).astype(h_ref.dtype)


def _ffn_down(g, w_down, x2, gpost, gnext, l, lnext, tm):
    M, D = x2.shape
    F = g.shape[1]
    return pl.pallas_call(
        _ffn_down_kernel,
        grid=(M // tm,),
        in_specs=[pl.BlockSpec((tm, F), lambda i: (i, 0)),
                  pl.BlockSpec((F, D), lambda i: (0, 0), pipeline_mode=pl.Buffered(1)),
                  pl.BlockSpec((tm, D), lambda i: (i, 0)),
                  pl.BlockSpec((None, 1, D), lambda i: (l, 0, 0)),
                  pl.BlockSpec((None, 1, D), lambda i: (lnext, 0, 0))],
        out_specs=[pl.BlockSpec((tm, D), lambda i: (i, 0))] * 2,
        out_shape=[jax.ShapeDtypeStruct((M, D), F32), jax.ShapeDtypeStruct((M, D), BF16)],
        compiler_params=_cparams(("parallel",)),
        name="ffn_down",
    )(g, w_down, x2, gpost, gnext)


def _rows(v):
    return v.astype(F32).reshape(v.shape[0], 1, v.shape[1])


def kernel(x, positions, norm_mix_pre, w_in, dn_conv_w, dn_a_log, dn_dt_bias, dn_norm_w, pool_w, pool_scale,
           swa_sinks, w_out, norm_mix_post, norm_ffn_pre, ffn_w_up, ffn_conv_w, ffn_conv_b, ffn_w_down,
           norm_ffn_post):
    B, T, D = x.shape
    depth = w_in.shape[0]
    M = B * T
    tm = min(ROW_TILE, T)
    G = SWA_HEADS // SWA_KV_HEADS
    x2 = x.reshape(M, D).astype(F32)

    w_in16 = _prep_w_in(w_in)
    gate_pad = ((0, 0), (DN_HEADS, LANES - 2 * DN_HEADS))
    alog_rows = _rows(jnp.pad(dn_a_log, gate_pad))
    dt_rows = _rows(jnp.pad(dn_dt_bias, gate_pad))
    sink_rows = jnp.pad(swa_sinks.astype(F32).reshape(depth, SWA_KV_HEADS, 1, G),
                        ((0, 0), (0, 0), (0, 0), (0, LANES - G)))
    g_mix_pre, g_mix_post = _rows(norm_mix_pre), _rows(norm_mix_post)
    g_ffn_pre, g_ffn_post = _rows(norm_ffn_pre), _rows(norm_ffn_post)
    dn_nw, pscale, conv_b = _rows(dn_norm_w), _rows(pool_scale), _rows(ffn_conv_b)

    cos, sin = _rope_tables(positions, tm)
    h = _prenorm(x2, g_mix_pre, tm)
    for l in range(depth):
        p, w_out16, w_down16 = _in_proj(h, w_in16, w_out, ffn_w_down, l, min(IN_PROJ_ROWS, T),
                                        P_W // IN_PROJ_PANELS)
        y_dn = _dn_layer(p, dn_conv_w, alog_rows, dt_rows, dn_nw, l, B, T)
        y_pool = _pool(p, pool_w, pscale, l, T, min(MIXER_ROWS, T))
        y_swa = _swa(p, cos, sin, sink_rows, l, T, min(MIXER_ROWS, T))
        x2, h = _out_proj(y_dn, y_pool, y_swa, w_out16, x2, g_mix_post, g_ffn_pre, l, tm)
        g = _ffn_up(h, ffn_w_up, ffn_conv_w, conv_b, l, T, min(FFN_TILE, T), FFN_PANEL)
        x2, h = _ffn_down(g, w_down16, x2, g_ffn_post, g_mix_pre, l, min(l + 1, depth - 1),
                          min(FFN_DOWN_ROWS, T))
    return x2.reshape(B, T, D).astype(x.dtype)
```

```python
import functools
import math

import numpy as np
import jax
import jax.numpy as jnp
from jax import lax
from jax.experimental import pallas as pl
from jax.experimental.pallas import tpu as pltpu

HEAD_DIM = 128
DN_HEADS = 6
DN_CONV = 4
DN_CHUNK = 64
POOL_WINDOWS = (2, 4, 8, 16)
POOL_GROUPS = 4
POOL_GROUP_DIM = 128
SWA_HEADS = 6
SWA_KV_HEADS = 2
SWA_WINDOW = 128
SWA_BLOCK = 128
ROPE_THETA = 10000.0
FFN_CONV = 3
NORM_EPS = 1e-6

DN_W = DN_HEADS * HEAD_DIM
POOL_W = POOL_GROUPS * POOL_GROUP_DIM
SWA_W = SWA_HEADS * HEAD_DIM
SWA_KV_W = SWA_KV_HEADS * HEAD_DIM
SWA_GROUP_W = SWA_W // SWA_KV_HEADS
LANES = 128
SUBLANES = 8

R_QKV = 0
R_Z = R_QKV + 3 * DN_W
R_B = R_Z + DN_W
R_POOL = R_B + 2 * DN_HEADS
R_SQ = R_POOL + POOL_W
R_SK = R_SQ + SWA_W
R_SV = R_SK + SWA_KV_W
R_W = R_SV + SWA_KV_W
P_Z = 0
P_SQ = P_Z + DN_W
P_POOL = P_SQ + SWA_W
P_SK = P_POOL + POOL_W
P_SV = P_SK + SWA_KV_W
P_BA = P_SV + SWA_KV_W
P_QKV = P_BA + LANES
P_W = P_QKV + 3 * DN_W
assert P_Z % DN_W == 0 and P_SQ % SWA_GROUP_W == 0 and P_POOL % POOL_W == 0
assert P_SK % HEAD_DIM == 0 and P_SV % HEAD_DIM == 0 and P_BA % LANES == 0 and P_QKV % LANES == 0

VMEM_LIMIT = 56 * 1024 * 1024

ROW_TILE = 512
MIXER_ROWS = 2048
IN_PROJ_ROWS = 1024
IN_PROJ_PANELS = 3
FFN_PANEL = 512
FFN_DOWN_ROWS = 256
assert (P_W // IN_PROJ_PANELS) % LANES == 0 and P_W % IN_PROJ_PANELS == 0

F32 = jnp.float32
BF16 = jnp.bfloat16


def _cparams(sem):
    return pltpu.CompilerParams(dimension_semantics=sem, vmem_limit_bytes=VMEM_LIMIT)


def _rms(v, gain):
    return v * lax.rsqrt(jnp.mean(v * v, axis=-1, keepdims=True) + NORM_EPS) * gain


def _sigmoid(v):
    return 1.0 / (1.0 + jnp.exp(-v))


def _bdot(a, b):
    return jnp.einsum("gik,gkj->gij", a.astype(BF16), b.astype(BF16), preferred_element_type=F32)


def _bdot_nt(a, b):
    return jnp.einsum("gik,gjk->gij", a.astype(BF16), b.astype(BF16), preferred_element_type=F32)


def _win_kernel(src_ref, w_ref, o_ref):
    L = w_ref.shape[1]
    row = lax.broadcasted_iota(jnp.int32, (w_ref.shape[0], w_ref.shape[2]), 0)
    keep = jnp.logical_or(pl.program_id(0) != P_BA // LANES, row < 2 * DN_HEADS)
    for l in range(L):
        o_ref[l] = jnp.where(keep, w_ref[:, l, :], 0.0).astype(o_ref.dtype)


def _prep_w_in(w_in):
    L, K, _ = w_in.shape
    wt = jnp.transpose(w_in, (2, 0, 1))
    sections = ((P_Z, R_Z, DN_W), (P_SQ, R_SQ, SWA_W), (P_POOL, R_POOL, POOL_W), (P_SK, R_SK, SWA_KV_W),
                (P_SV, R_SV, SWA_KV_W), (P_BA, R_B, LANES), (P_QKV, R_QKV, 3 * DN_W))
    src = np.zeros((P_W // LANES,), np.int32)
    for p0, r0, width in sections:
        for k in range(width // LANES):
            src[p0 // LANES + k] = r0 + k * LANES
    return pl.pallas_call(
        _win_kernel,
        grid_spec=pltpu.PrefetchScalarGridSpec(
            num_scalar_prefetch=1,
            grid=(P_W // LANES,),
            in_specs=[pl.BlockSpec((pl.Element(LANES), pl.Element(L), pl.Element(K)),
                                   lambda c, src_ref: (src_ref[c], 0, 0))],
            out_specs=pl.BlockSpec((L, LANES, K), lambda c, src_ref: (0, c, 0)),
        ),
        out_shape=jax.ShapeDtypeStruct((L, P_W, K), BF16),
        compiler_params=_cparams(("parallel",)),
        name="prep_w_in",
    )(jnp.asarray(src), wt)


def _prenorm_kernel(x_ref, g_ref, h_ref):
    h_ref[...] = _rms(x_ref[...], g_ref[...]).astype(h_ref.dtype)


def _prenorm(x2, gains, tm):
    M, D = x2.shape
    return pl.pallas_call(
        _prenorm_kernel,
        grid=(M // tm,),
        in_specs=[pl.BlockSpec((tm, D), lambda i: (i, 0)), pl.BlockSpec((None, 1, D), lambda i: (0, 0, 0))],
        out_specs=pl.BlockSpec((tm, D), lambda i: (i, 0)),
        out_shape=jax.ShapeDtypeStruct((M, D), BF16),
        compiler_params=_cparams(("parallel",)),
        name="prenorm",
    )(x2, gains)


def _rope_kernel(pos_ref, freq_ref, cos_ref, sin_ref):
    ang = pos_ref[...] * freq_ref[...]
    lane = lax.broadcasted_iota(jnp.int32, ang.shape, 1)
    cos_ref[...] = jnp.cos(ang)
    sin_ref[...] = jnp.where(lane < HEAD_DIM // 2, -jnp.sin(ang), jnp.sin(ang))


def _rope_tables(positions, tm):
    M = positions.size
    pos = positions.astype(F32).reshape(M, 1)
    inv = (1.0 / (ROPE_THETA ** (np.arange(0, HEAD_DIM, 2, dtype=np.float32) / HEAD_DIM))).astype(np.float32)
    freq = jnp.asarray(np.concatenate([inv, inv])[None, :])
    return pl.pallas_call(
        _rope_kernel,
        grid=(M // tm,),
        in_specs=[pl.BlockSpec((tm, 1), lambda i: (i, 0)), pl.BlockSpec((1, HEAD_DIM), lambda i: (0, 0))],
        out_specs=[pl.BlockSpec((tm, HEAD_DIM), lambda i: (i, 0))] * 2,
        out_shape=[jax.ShapeDtypeStruct((M, HEAD_DIM), F32)] * 2,
        compiler_params=_cparams(("parallel",)),
        name="rope_tables",
    )(pos, freq)


def _mm_kernel(a_ref, wt_ref, wo_ref, wd_ref, o_ref, wo16_ref, wd16_ref):
    o_ref[...] = lax.dot_general(a_ref[...], wt_ref[...], (((1,), (1,)), ((), ())), preferred_element_type=F32)
    wo16_ref[...] = wo_ref[...].astype(wo16_ref.dtype)
    wd16_ref[...] = wd_ref[...].astype(wd16_ref.dtype)


def _cast_slices(rows, steps):
    return max(n for n in range(1, steps + 1) if rows % n == 0 and (rows // n) % (2 * SUBLANES) == 0)


def _in_proj(h, wt, w_out, w_down, l, tm, tn):
    M, K = h.shape
    N = wt.shape[1]
    nj, nt = N // tn, M // tm
    specs, shapes = [], []
    for w in (w_out, w_down):
        rows, cols = w.shape[1], w.shape[2]
        n = _cast_slices(rows, nj * nt)
        idx = lambda j, i, n=n: jnp.minimum(j * nt + i, n - 1)
        specs.append((pl.BlockSpec((None, rows // n, cols), lambda j, i, idx=idx: (l, idx(j, i), 0)),
                      pl.BlockSpec((rows // n, cols), lambda j, i, idx=idx: (idx(j, i), 0))))
        shapes.append(jax.ShapeDtypeStruct((rows, cols), BF16))
    return pl.pallas_call(
        _mm_kernel,
        grid=(nj, nt),
        in_specs=[pl.BlockSpec((tm, K), lambda j, i: (i, 0)), pl.BlockSpec((None, tn, K), lambda j, i: (l, j, 0)),
                  specs[0][0], specs[1][0]],
        out_specs=[pl.BlockSpec((tm, tn), lambda j, i: (i, j)), specs[0][1], specs[1][1]],
        out_shape=[jax.ShapeDtypeStruct((M, N), F32)] + shapes,
        compiler_params=_cparams(("arbitrary", "arbitrary")),
        name="in_proj",
    )(h, wt, w_out, w_down)


def _gate_kernel(ba_ref, alog_ref, dt_ref, beta_ref, gc_ref):
    p = ba_ref[...]
    beta_ref[...] = _sigmoid(p)
    sp = p + dt_ref[...]
    softplus = jnp.maximum(sp, 0.0) + jnp.log(1.0 + jnp.exp(-jnp.abs(sp)))
    g = -jnp.exp(alog_ref[...]) * softplus
    row = lax.broadcasted_iota(jnp.int32, g.shape, 0) % DN_CHUNK
    s = 1
    while s < DN_CHUNK:
        g = g + jnp.where(row >= s, pltpu.roll(g, s, axis=0), 0.0)
        s *= 2
    gc_ref[...] = g


def _gates(p, alog_rows, dt_rows, l, B, T):
    M = p.shape[0]
    cb = P_BA // LANES
    vec = pl.BlockSpec((None, 1, LANES), lambda b: (l, 0, 0))
    return pl.pallas_call(
        _gate_kernel,
        grid=(B,),
        in_specs=[pl.BlockSpec((T, LANES), lambda b: (b, cb)), vec, vec],
        out_specs=[pl.BlockSpec((T, LANES), lambda b: (b, 0))] * 2,
        out_shape=[jax.ShapeDtypeStruct((M, LANES), F32)] * 2,
        compiler_params=_cparams(("parallel",)),
        name="dn_gates",
    )(p, alog_rows, dt_rows)


def _dn_prep_kernel(q_ref, k_ref, v_ref, qh_ref, kh_ref, vh_ref, beta_ref, gc_ref, gcrow_ref,
                    cwq_ref, cwk_ref, cwv_ref, wq_ref, u_ref, kd_ref, a_ref, c_ref, *, tiles_per_seq):
    h = pl.program_id(1)
    first = (pl.program_id(2) % tiles_per_seq) == 0
    tt = q_ref.shape[0]
    C = DN_CHUNK
    G = tt // C
    D = HEAD_DIM
    pad = SUBLANES

    def conv_silu(x_ref, halo_ref, cw_ref):
        halo = jnp.where(first, 0.0, halo_ref[...])
        ext = jnp.concatenate([halo, x_ref[...]], axis=0)
        c0, c1, c2, c3 = (cw_ref[j:j + 1, :] for j in range(DN_CONV))
        back1 = pltpu.roll(ext, 1, axis=0)
        z = ext * c1 + back1 * c0
        acc = (ext * c3 + back1 * c2 + pltpu.roll(z, 2, axis=0))[pad:pad + tt]
        return acc * _sigmoid(acc)

    q = conv_silu(q_ref, qh_ref, cwq_ref)
    k = conv_silu(k_ref, kh_ref, cwk_ref)
    v = conv_silu(v_ref, vh_ref, cwv_ref)
    q = q * lax.rsqrt(jnp.sum(q * q, axis=-1, keepdims=True) + NORM_EPS) * (D ** -0.5)
    k = k * lax.rsqrt(jnp.sum(k * k, axis=-1, keepdims=True) + NORM_EPS)
    lane = lax.broadcasted_iota(jnp.int32, (tt, LANES), 1)
    beta = jnp.sum(jnp.where(lane == h, beta_ref[...], 0.0), axis=-1, keepdims=True)
    gcc = jnp.sum(jnp.where(lane == h + DN_HEADS, gc_ref[...], 0.0), axis=-1, keepdims=True)
    eg = jnp.exp(gcc)
    kb = k * beta
    vb = v * beta

    q3 = q.reshape(G, C, D)
    k3 = k.reshape(G, C, D)
    kb3 = kb.reshape(G, C, D)
    gcc3 = gcc.reshape(G, C, 1)
    gcr3 = gcrow_ref[...]
    ii = lax.broadcasted_iota(jnp.int32, (G, C, C), 1)
    jj = lax.broadcasted_iota(jnp.int32, (G, C, C), 2)
    decay = jnp.exp(jnp.where(ii >= jj, gcc3 - gcr3, -jnp.inf))
    eye = (ii == jj).astype(F32)
    kq = _bdot_nt(jnp.concatenate([kb3, q3], axis=1), k3)
    nil = jnp.where(ii > jj, -kq[:, 0:C] * decay, 0.0)
    inv = eye + nil
    powk = _bdot(nil, nil)
    for _ in range(int(math.log2(C)) - 2):
        both = _bdot(jnp.concatenate([inv, powk], axis=1), powk)
        inv = inv + both[:, 0:C]
        powk = both[:, C:2 * C]
    inv = inv + _bdot(inv, powk)
    rhs = jnp.concatenate([vb, kb * eg], axis=-1).reshape(G, C, 2 * D)
    uw = _bdot(inv, rhs)
    glast = gcr3[:, :, C - 1:C]
    u_ref[...] = uw[:, :, :D]
    wq_ref[:, 0:C, :] = uw[:, :, D:].astype(wq_ref.dtype)
    wq_ref[:, C:2 * C, :] = (q * eg).reshape(G, C, D).astype(wq_ref.dtype)
    kd_ref[...] = (k3 * jnp.exp(glast - gcc3)).astype(kd_ref.dtype)
    a_ref[...] = (kq[:, C:2 * C] * decay).astype(a_ref.dtype)
    c_ref[...] = jnp.broadcast_to(jnp.exp(glast), c_ref.shape)


def _dn_prep(p, beta, gc, gcrow, conv_w, l, B, T, tt):
    H = DN_HEADS
    C = DN_CHUNK
    N = T // C
    G = tt // C
    nt = T // tt
    hb = tt // SUBLANES
    qb, kb, vb = P_QKV // LANES, (P_QKV + DN_W) // LANES, (P_QKV + 2 * DN_W) // LANES
    tok = lambda off: pl.BlockSpec((tt, HEAD_DIM), lambda b, h, i: (b * nt + i, off + h))
    halo = lambda off: pl.BlockSpec((SUBLANES, HEAD_DIM),
                                    lambda b, h, i: (jnp.maximum((b * nt + i) * hb - 1, 0), off + h))
    gate = pl.BlockSpec((tt, LANES), lambda b, h, i: (b * nt + i, 0))
    cw = lambda off: pl.BlockSpec((None, DN_CONV, HEAD_DIM), lambda b, h, i: (l, 0, off + h))
    chunked = lambda r, c: pl.BlockSpec((None, None, G, r, c), lambda b, h, i: (b, h, i, 0, 0))
    return pl.pallas_call(
        functools.partial(_dn_prep_kernel, tiles_per_seq=nt),
        grid=(B, H, nt),
        in_specs=[tok(qb), tok(kb), tok(vb), halo(qb), halo(kb), halo(vb), gate, gate, chunked(1, C),
                  cw(0), cw(H), cw(2 * H)],
        out_specs=[chunked(2 * C, HEAD_DIM), chunked(C, HEAD_DIM), chunked(C, HEAD_DIM), chunked(C, C),
                   chunked(1, LANES)],
        out_shape=[jax.ShapeDtypeStruct((B, H, N, 2 * C, HEAD_DIM), BF16),
                   jax.ShapeDtypeStruct((B, H, N, C, HEAD_DIM), F32),
                   jax.ShapeDtypeStruct((B, H, N, C, HEAD_DIM), BF16),
                   jax.ShapeDtypeStruct((B, H, N, C, C), BF16),
                   jax.ShapeDtypeStruct((B, H, N, 1, LANES), F32)],
        compiler_params=_cparams(("parallel", "parallel", "parallel")),
        name="dn_prep",
    )(p, p, p, p, p, p, beta, gc, gcrow, conv_w, conv_w, conv_w)


def _dn_scan_kernel(wq_ref, u_ref, kd_ref, a_ref, c_ref, z_ref, nw_ref, y_ref, s_ref):
    B, H, G = wq_ref.shape[0], wq_ref.shape[1], wq_ref.shape[2]
    C = DN_CHUNK
    D = HEAD_DIM

    @pl.when(pl.program_id(0) == 0)
    def _():
        s_ref[...] = jnp.zeros_like(s_ref)

    nw = nw_ref[...]
    chains = [(b, h) for b in range(B) for h in range(H)]

    def body(n, carry):
        r0 = pl.multiple_of(n * C, C)
        S = [s_ref[b * H + h] for b, h in chains]
        m1 = [jnp.dot(wq_ref[b, h, n], S[i].astype(BF16), preferred_element_type=F32)
              for i, (b, h) in enumerate(chains)]
        v_new = [(u_ref[b, h, n] - m1[i][0:C]).astype(BF16) for i, (b, h) in enumerate(chains)]
        o = [m1[i][C:2 * C] + jnp.dot(a_ref[b, h, n], v_new[i], preferred_element_type=F32)
             for i, (b, h) in enumerate(chains)]
        kv = [lax.dot_general(kd_ref[b, h, n], v_new[i], (((0,), (0,)), ((), ())), preferred_element_type=F32)
              for i, (b, h) in enumerate(chains)]
        for i, (b, h) in enumerate(chains):
            s_ref[b * H + h] = S[i] * c_ref[b, h, n] + kv[i]
            zz = z_ref[b, pl.ds(r0, C), h * D:(h + 1) * D]
            y_ref[b, pl.ds(r0, C), h * D:(h + 1) * D] = (_rms(o[i], nw) * (zz * _sigmoid(zz))).astype(y_ref.dtype)
        return carry

    lax.fori_loop(0, G, body, 0)


def _dn_scan(wq, u, kd, a, c, p, norm_w, l, B, T, tt):
    H = DN_HEADS
    C = DN_CHUNK
    G = tt // C
    nt = T // tt
    chunked = lambda r, cc: pl.BlockSpec((B, H, G, r, cc), lambda i: (0, 0, i, 0, 0))
    y = pl.pallas_call(
        _dn_scan_kernel,
        grid=(nt,),
        in_specs=[chunked(2 * C, HEAD_DIM), chunked(C, HEAD_DIM), chunked(C, HEAD_DIM), chunked(C, C),
                  chunked(1, LANES),
                  pl.BlockSpec((B, tt, DN_W), lambda i: (0, i, P_Z // DN_W)),
                  pl.BlockSpec((None, 1, HEAD_DIM), lambda i: (l, 0, 0))],
        out_specs=pl.BlockSpec((B, tt, DN_W), lambda i: (0, i, 0)),
        out_shape=jax.ShapeDtypeStruct((B, T, DN_W), BF16),
        scratch_shapes=[pltpu.VMEM((B * H, HEAD_DIM, HEAD_DIM), F32)],
        compiler_params=_cparams(("arbitrary",)),
        name="dn_scan",
    )(wq, u, kd, a, c, p.reshape(B, T, p.shape[1]), norm_w)
    return y.reshape(B * T, DN_W)


DN_PREP_ROWS = 4096
DN_SCAN_ROWS = 512


def _dn_layer(p, conv_w, alog_rows, dt_rows, norm_w, l, B, T):
    N = T // DN_CHUNK
    beta, gc = _gates(p, alog_rows, dt_rows, l, B, T)
    gcrow = gc[:, DN_HEADS:2 * DN_HEADS].reshape(B, N, DN_CHUNK, DN_HEADS).transpose(0, 3, 1, 2)
    gcrow = gcrow.reshape(B, DN_HEADS, N, 1, DN_CHUNK)
    wq, u, kd, a, c = _dn_prep(p, beta, gc, gcrow, conv_w, l, B, T, min(DN_PREP_ROWS, T))
    return _dn_scan(wq, u, kd, a, c, p, norm_w, l, B, T, min(DN_SCAN_ROWS, T))


POOL_HALO = 16


def _pool_kernel(x_ref, halo_ref, w_ref, scale_ref, o_ref, *, tiles_per_seq):
    i = pl.program_id(0)
    tm = x_ref.shape[0]
    t0 = (i % tiles_per_seq) * tm
    first = (i % tiles_per_seq) == 0
    pos = t0 + 1 + lax.broadcasted_iota(jnp.int32, (tm, POOL_GROUP_DIM), 0)
    for gi, win in enumerate(POOL_WINDOWS):
        cols = slice(gi * POOL_GROUP_DIM, (gi + 1) * POOL_GROUP_DIM)
        halo = jnp.where(first, 0.0, halo_ref[:, cols])
        xg = x_ref[:, cols]
        ext = jnp.concatenate([halo, xg], axis=0)
        acc = ext
        s = 1
        while s < win:
            acc = acc + pltpu.roll(acc, s, axis=0)
            s *= 2
        cnt = jnp.minimum(pos, win).astype(F32)
        y = acc[POOL_HALO:] / cnt - xg
        y = jnp.dot(y.astype(BF16), w_ref[gi].astype(BF16), preferred_element_type=F32)
        o_ref[:, cols] = (y * scale_ref[:, cols]).astype(o_ref.dtype)


def _pool(p, pool_w, pool_scale, l, T, tm):
    M = p.shape[0]
    cb = P_POOL // POOL_W
    hb = tm // POOL_HALO
    return pl.pallas_call(
        functools.partial(_pool_kernel, tiles_per_seq=T // tm),
        grid=(M // tm,),
        in_specs=[pl.BlockSpec((tm, POOL_W), lambda i: (i, cb)),
                  pl.BlockSpec((POOL_HALO, POOL_W), lambda i: (jnp.maximum(i * hb - 1, 0), cb)),
                  pl.BlockSpec((None, POOL_GROUPS, POOL_GROUP_DIM, POOL_GROUP_DIM), lambda i: (l, 0, 0, 0)),
                  pl.BlockSpec((None, 1, POOL_W), lambda i: (l, 0, 0))],
        out_specs=pl.BlockSpec((tm, POOL_W), lambda i: (i, 0)),
        out_shape=jax.ShapeDtypeStruct((M, POOL_W), BF16),
        compiler_params=_cparams(("parallel",)),
        name="pool",
    )(p, p, pool_w, pool_scale)


def _swa_kernel(q_ref, k_ref, kp_ref, v_ref, vp_ref, cos_ref, sin_ref, cosp_ref, sinp_ref, sink_ref, o_ref,
                *, tiles_per_seq):
    i = pl.program_id(1)
    first = (i % tiles_per_seq) == 0
    tm = q_ref.shape[0]
    blk = SWA_BLOCK
    G = SWA_HEADS // SWA_KV_HEADS
    D = HEAD_DIM

    def rope(x, cos, sin):
        return x * cos + pltpu.roll(x, D // 2, axis=1) * sin

    cos = cos_ref[...]
    sin = sin_ref[...]
    cosx = jnp.concatenate([cosp_ref[...], cos], axis=0)
    sinx = jnp.concatenate([sinp_ref[...], sin], axis=0)
    kx = rope(jnp.concatenate([kp_ref[...], k_ref[...]], axis=0), cosx, sinx).astype(BF16)
    vx = jnp.concatenate([vp_ref[...], v_ref[...]], axis=0).astype(BF16)
    vx = jnp.concatenate([vx, jnp.ones_like(vx)], axis=1)
    scale = D ** -0.5
    qh = [(rope(q_ref[:, g * D:(g + 1) * D], cos, sin) * scale).astype(BF16) for g in range(G)]

    ri = lax.broadcasted_iota(jnp.int32, (G * blk, 2 * blk), 0)
    ii = ri % blk
    jj = lax.broadcasted_iota(jnp.int32, (G * blk, 2 * blk), 1)
    band = jnp.logical_or(jnp.logical_and(jj < blk, jj > ii), jnp.logical_and(jj >= blk, jj - blk <= ii))
    no_prev = jnp.where(first, blk, 0)
    band_first = jnp.logical_and(band, jj >= no_prev)
    sink = sink_ref[:, 0:1]
    for g in range(1, G):
        sink = jnp.where(ri[:, 0:1] >= g * blk, sink_ref[:, g:g + 1], sink)
    for b in range(tm // blk):
        qs = jnp.concatenate([q[b * blk:(b + 1) * blk] for q in qh], axis=0)
        keys = kx[b * blk:(b + 2) * blk]
        s = lax.dot_general(qs, keys, (((1,), (1,)), ((), ())), preferred_element_type=F32)
        s = jnp.where(band_first if b == 0 else band, s, -jnp.inf)
        m = jnp.maximum(jnp.max(s, axis=-1, keepdims=True), sink)
        e = jnp.exp(s - m).astype(BF16)
        pv = jnp.dot(e, vx[b * blk:(b + 2) * blk], preferred_element_type=F32)
        o = pv[:, 0:D] / (pv[:, D:2 * D] + jnp.exp(sink - m))
        for g in range(G):
            o_ref[b * blk:(b + 1) * blk, g * D:(g + 1) * D] = o[g * blk:(g + 1) * blk].astype(o_ref.dtype)


def _swa(p, cos, sin, sink_rows, l, T, tm):
    M = p.shape[0]
    blk = SWA_BLOCK
    r = tm // blk
    qb, kb, vb = P_SQ // SWA_GROUP_W, P_SK // HEAD_DIM, P_SV // HEAD_DIM
    prev = lambda i: jnp.maximum(i * r - 1, 0)
    return pl.pallas_call(
        functools.partial(_swa_kernel, tiles_per_seq=T // tm),
        grid=(SWA_KV_HEADS, M // tm),
        in_specs=[pl.BlockSpec((tm, SWA_GROUP_W), lambda kv, i: (i, qb + kv)),
                  pl.BlockSpec((tm, HEAD_DIM), lambda kv, i: (i, kb + kv)),
                  pl.BlockSpec((blk, HEAD_DIM), lambda kv, i: (prev(i), kb + kv)),
                  pl.BlockSpec((tm, HEAD_DIM), lambda kv, i: (i, vb + kv)),
                  pl.BlockSpec((blk, HEAD_DIM), lambda kv, i: (prev(i), vb + kv)),
                  pl.BlockSpec((tm, HEAD_DIM), lambda kv, i: (i, 0)),
                  pl.BlockSpec((tm, HEAD_DIM), lambda kv, i: (i, 0)),
                  pl.BlockSpec((blk, HEAD_DIM), lambda kv, i: (prev(i), 0)),
                  pl.BlockSpec((blk, HEAD_DIM), lambda kv, i: (prev(i), 0)),
                  pl.BlockSpec((None, None, 1, LANES), lambda kv, i: (l, kv, 0, 0))],
        out_specs=pl.BlockSpec((tm, SWA_GROUP_W), lambda kv, i: (i, kv)),
        out_shape=jax.ShapeDtypeStruct((M, SWA_W), BF16),
        compiler_params=_cparams(("parallel", "parallel")),
        name="swa",
    )(p, p, p, p, p, cos, sin, cos, sin, sink_rows)


OUT_ROWS = 128


def _outproj_kernel(ydn_ref, ypool_ref, yswa_ref, w_ref, x_ref, gpost_ref, gffn_ref, xo_ref, h_ref):
    for r0 in range(0, x_ref.shape[0], OUT_ROWS):
        rows = slice(r0, r0 + OUT_ROWS)
        y = jnp.concatenate([ydn_ref[rows, :], ypool_ref[rows, :], yswa_ref[rows, :]], axis=1)
        mix = jnp.dot(y, w_ref[...], preferred_element_type=F32)
        xn = x_ref[rows, :] + _rms(mix, gpost_ref[...])
        xo_ref[rows, :] = xn
        h_ref[rows, :] = _rms(xn, gffn_ref[...]).astype(h_ref.dtype)


def _out_proj(ydn, ypool, yswa, w_out, x2, gpost, gffn, l, tm):
    M, D = x2.shape
    row = lambda w: pl.BlockSpec((tm, w), lambda i: (i, 0))
    vec = pl.BlockSpec((None, 1, D), lambda i: (l, 0, 0))
    return pl.pallas_call(
        _outproj_kernel,
        grid=(M // tm,),
        in_specs=[row(DN_W), row(POOL_W), row(SWA_W),
                  pl.BlockSpec(w_out.shape, lambda i: (0, 0)), row(D), vec, vec],
        out_specs=[row(D), row(D)],
        out_shape=[jax.ShapeDtypeStruct((M, D), F32), jax.ShapeDtypeStruct((M, D), BF16)],
        compiler_params=_cparams(("parallel",)),
        name="out_proj",
    )(ydn, ypool, yswa, w_out, x2, gpost, gffn)


FFN_TILE = 1024
FFN_ROWS = 512


def _ffn_chunks(tm):
    return (FFN_ROWS,) * (tm // FFN_ROWS) if tm % FFN_ROWS == 0 else (tm,)


def _ffn_up_kernel(h_ref, wa_ref, wb_ref, cwa_ref, cwb_ref, cba_ref, cbb_ref, g_ref, wa16, wb16, ua, ub,
                   *, tiles_per_seq):
    i = pl.program_id(1)
    tm = h_ref.shape[0]
    pad = SUBLANES

    @pl.when(i == 0)
    def _():
        wa16[...] = wa_ref[...].astype(BF16)
        wb16[...] = wb_ref[...].astype(BF16)

    @pl.when(i % tiles_per_seq == 0)
    def _():
        ua[0:pad, :] = jnp.zeros((pad, ua.shape[1]), F32)
        ub[0:pad, :] = jnp.zeros((pad, ub.shape[1]), F32)

    hb = h_ref[...]
    ua[pad:pad + tm, :] = jnp.dot(hb, wa16[...], preferred_element_type=F32)
    ub[pad:pad + tm, :] = jnp.dot(hb, wb16[...], preferred_element_type=F32)

    def conv(scr, cw_ref, cb_ref, r0, rows):
        acc = cb_ref[...] + scr[pad + r0:pad + r0 + rows, :] * cw_ref[FFN_CONV - 1:FFN_CONV, :]
        for s in range(1, FFN_CONV):
            acc = acc + scr[pad + r0 - s:pad + r0 - s + rows, :] * cw_ref[FFN_CONV - 1 - s:FFN_CONV - s, :]
        return acc

    r0 = 0
    for rows in _ffn_chunks(tm):
        a = conv(ua, cwa_ref, cba_ref, r0, rows)
        b = conv(ub, cwb_ref, cbb_ref, r0, rows)
        g_ref[r0:r0 + rows, :] = (a * _sigmoid(a) * b).astype(g_ref.dtype)
        r0 += rows
    ua[0:pad, :] = ua[tm:tm + pad, :]
    ub[0:pad, :] = ub[tm:tm + pad, :]


def _ffn_up(h, w_up, conv_w, conv_b, l, T, tm, tn):
    M, D = h.shape
    F = w_up.shape[2] // 2
    nj = F // tn
    return pl.pallas_call(
        functools.partial(_ffn_up_kernel, tiles_per_seq=T // tm),
        grid=(nj, M // tm),
        in_specs=[pl.BlockSpec((tm, D), lambda j, i: (i, 0)),
                  pl.BlockSpec((None, D, tn), lambda j, i: (l, 0, j)),
                  pl.BlockSpec((None, D, tn), lambda j, i: (l, 0, nj + j)),
                  pl.BlockSpec((None, FFN_CONV, tn), lambda j, i: (l, 0, j)),
                  pl.BlockSpec((None, FFN_CONV, tn), lambda j, i: (l, 0, nj + j)),
                  pl.BlockSpec((None, 1, tn), lambda j, i: (l, 0, j)),
                  pl.BlockSpec((None, 1, tn), lambda j, i: (l, 0, nj + j))],
        out_specs=pl.BlockSpec((tm, tn), lambda j, i: (i, j)),
        out_shape=jax.ShapeDtypeStruct((M, F), BF16),
        scratch_shapes=[pltpu.VMEM((D, tn), BF16)] * 2 + [pltpu.VMEM((tm + SUBLANES, tn), F32)] * 2,
        compiler_params=_cparams(("parallel", "arbitrary")),
        name="ffn_up",
    )(h, w_up, w_up, conv_w, conv_w, conv_b, conv_b)


DOWN_ROWS = 128


def _ffn_down_kernel(g_ref, w_ref, x_ref, gpost_ref, gnext_ref, xo_ref, h_ref):
    for r0 in range(0, x_ref.shape[0], DOWN_ROWS):
        rows = slice(r0, r0 + DOWN_ROWS)
        f = jnp.dot(g_ref[rows, :], w_ref[...], preferred_element_type=F32)
        xn = x_ref[rows, :] + _rms(f, gpost_ref[...])
        xo_ref[rows, :] = xn
        h_ref[rows, :] = _rms(xn, gnext_ref[...]).astype(h_ref.dtype)


def _ffn_down(g, w_down, x2, gpost, gnext, l, lnext, tm):
    M, D = x2.shape
    F = g.shape[1]
    return pl.pallas_call(
        _ffn_down_kernel,
        grid=(M // tm,),
        in_specs=[pl.BlockSpec((tm, F), lambda i: (i, 0)),
                  pl.BlockSpec((F, D), lambda i: (0, 0), pipeline_mode=pl.Buffered(1)),
                  pl.BlockSpec((tm, D), lambda i: (i, 0)),
                  pl.BlockSpec((None, 1, D), lambda i: (l, 0, 0)),
                  pl.BlockSpec((None, 1, D), lambda i: (lnext, 0, 0))],
        out_specs=[pl.BlockSpec((tm, D), lambda i: (i, 0))] * 2,
        out_shape=[jax.ShapeDtypeStruct((M, D), F32), jax.ShapeDtypeStruct((M, D), BF16)],
        compiler_params=_cparams(("parallel",)),
        name="ffn_down",
    )(g, w_down, x2, gpost, gnext)


def _rows(v):
    return v.astype(F32).reshape(v.shape[0], 1, v.shape[1])


def kernel(x, positions, norm_mix_pre, w_in, dn_conv_w, dn_a_log, dn_dt_bias, dn_norm_w, pool_w, pool_scale,
           swa_sinks, w_out, norm_mix_post, norm_ffn_pre, ffn_w_up, ffn_conv_w, ffn_conv_b, ffn_w_down,
           norm_ffn_post):
    B, T, D = x.shape
    depth = w_in.shape[0]
    M = B * T
    tm = min(ROW_TILE, T)
    G = SWA_HEADS // SWA_KV_HEADS
    x2 = x.reshape(M, D).astype(F32)

    w_in16 = _prep_w_in(w_in)
    gate_pad = ((0, 0), (DN_HEADS, LANES - 2 * DN_HEADS))
    alog_rows = _rows(jnp.pad(dn_a_log, gate_pad))
    dt_rows = _rows(jnp.pad(dn_dt_bias, gate_pad))
    sink_rows = jnp.pad(swa_sinks.astype(F32).reshape(depth, SWA_KV_HEADS, 1, G),
                        ((0, 0), (0, 0), (0, 0), (0, LANES - G)))
    g_mix_pre, g_mix_post = _rows(norm_mix_pre), _rows(norm_mix_post)
    g_ffn_pre, g_ffn_post = _rows(norm_ffn_pre), _rows(norm_ffn_post)
    dn_nw, pscale, conv_b = _rows(dn_norm_w), _rows(pool_scale), _rows(ffn_conv_b)

    cos, sin = _rope_tables(positions, tm)
    h = _prenorm(x2, g_mix_pre, tm)
    for l in range(depth):
        p, w_out16, w_down16 = _in_proj(h, w_in16, w_out, ffn_w_down, l, min(IN_PROJ_ROWS, T),
                                        P_W // IN_PROJ_PANELS)
        y_dn = _dn_layer(p, dn_conv_w, alog_rows, dt_rows, dn_nw, l, B, T)
        y_pool = _pool(p, pool_w, pscale, l, T, min(MIXER_ROWS, T))
        y_swa = _swa(p, cos, sin, sink_rows, l, T, min(MIXER_ROWS, T))
        x2, h = _out_proj(y_dn, y_pool, y_swa, w_out16, x2, g_mix_post, g_ffn_pre, l, tm)
        g = _ffn_up(h, ffn_w_up, ffn_conv_w, conv_b, l, T, min(FFN_TILE, T), FFN_PANEL)
        x2, h = _ffn_down(g, w_down16, x2, g_ffn_post, g_mix_pre, l, min(l + 1, depth - 1),
                          min(FFN_DOWN_ROWS, T))
    return x2.reshape(B, T, D).astype(x.dtype)
```

```python
import functools
import math

import numpy as np
import jax
import jax.numpy as jnp
from jax import lax
from jax.experimental import pallas as pl
from jax.experimental.pallas import tpu as pltpu

HEAD_DIM = 128
DN_HEADS = 6
DN_CONV = 4
DN_CHUNK = 64
POOL_WINDOWS = (2, 4, 8, 16)
POOL_GROUPS = 4
POOL_GROUP_DIM = 128
SWA_HEADS = 6
SWA_KV_HEADS = 2
SWA_WINDOW = 128
SWA_BLOCK = 128
ROPE_THETA = 10000.0
FFN_CONV = 3
NORM_EPS = 1e-6

DN_W = DN_HEADS * HEAD_DIM
POOL_W = POOL_GROUPS * POOL_GROUP_DIM
SWA_W = SWA_HEADS * HEAD_DIM
SWA_KV_W = SWA_KV_HEADS * HEAD_DIM
SWA_GROUP_W = SWA_W // SWA_KV_HEADS
LANES = 128
SUBLANES = 8

R_QKV = 0
R_Z = R_QKV + 3 * DN_W
R_B = R_Z + DN_W
R_POOL = R_B + 2 * DN_HEADS
R_SQ = R_POOL + POOL_W
R_SK = R_SQ + SWA_W
R_SV = R_SK + SWA_KV_W
R_W = R_SV + SWA_KV_W
P_Z = 0
P_SQ = P_Z + DN_W
P_POOL = P_SQ + SWA_W
P_SK = P_POOL + POOL_W
P_SV = P_SK + SWA_KV_W
P_BA = P_SV + SWA_KV_W
P_QKV = P_BA + LANES
P_W = P_QKV + 3 * DN_W
assert P_Z % DN_W == 0 and P_SQ % SWA_GROUP_W == 0 and P_POOL % POOL_W == 0
assert P_SK % HEAD_DIM == 0 and P_SV % HEAD_DIM == 0 and P_BA % LANES == 0 and P_QKV % LANES == 0

VMEM_LIMIT = 56 * 1024 * 1024

ROW_TILE = 512
MIXER_ROWS = 2048
IN_PROJ_ROWS = 1024
IN_PROJ_PANELS = 3
FFN_PANEL = 512
FFN_DOWN_ROWS = 256
assert (P_W // IN_PROJ_PANELS) % LANES == 0 and P_W % IN_PROJ_PANELS == 0

F32 = jnp.float32
BF16 = jnp.bfloat16


def _cparams(sem):
    return pltpu.CompilerParams(dimension_semantics=sem, vmem_limit_bytes=VMEM_LIMIT)


def _rms(v, gain):
    return v * lax.rsqrt(jnp.mean(v * v, axis=-1, keepdims=True) + NORM_EPS) * gain


def _sigmoid(v):
    return 1.0 / (1.0 + jnp.exp(-v))


def _bdot(a, b):
    return jnp.einsum("gik,gkj->gij", a.astype(BF16), b.astype(BF16), preferred_element_type=F32)


def _bdot_nt(a, b):
    return jnp.einsum("gik,gjk->gij", a.astype(BF16), b.astype(BF16), preferred_element_type=F32)


def _win_kernel(src_ref, w_ref, o_ref):
    L = w_ref.shape[1]
    row = lax.broadcasted_iota(jnp.int32, (w_ref.shape[0], w_ref.shape[2]), 0)
    keep = jnp.logical_or(pl.program_id(0) != P_BA // LANES, row < 2 * DN_HEADS)
    for l in range(L):
        o_ref[l] = jnp.where(keep, w_ref[:, l, :], 0.0).astype(o_ref.dtype)


def _prep_w_in(w_in):
    L, K, _ = w_in.shape
    wt = jnp.transpose(w_in, (2, 0, 1))
    sections = ((P_Z, R_Z, DN_W), (P_SQ, R_SQ, SWA_W), (P_POOL, R_POOL, POOL_W), (P_SK, R_SK, SWA_KV_W),
                (P_SV, R_SV, SWA_KV_W), (P_BA, R_B, LANES), (P_QKV, R_QKV, 3 * DN_W))
    src = np.zeros((P_W // LANES,), np.int32)
    for p0, r0, width in sections:
        for k in range(width // LANES):
            src[p0 // LANES + k] = r0 + k * LANES
    return pl.pallas_call(
        _win_kernel,
        grid_spec=pltpu.PrefetchScalarGridSpec(
            num_scalar_prefetch=1,
            grid=(P_W // LANES,),
            in_specs=[pl.BlockSpec((pl.Element(LANES), pl.Element(L), pl.Element(K)),
                                   lambda c, src_ref: (src_ref[c], 0, 0))],
            out_specs=pl.BlockSpec((L, LANES, K), lambda c, src_ref: (0, c, 0)),
        ),
        out_shape=jax.ShapeDtypeStruct((L, P_W, K), BF16),
        compiler_params=_cparams(("parallel",)),
        name="prep_w_in",
    )(jnp.asarray(src), wt)


def _prenorm_kernel(x_ref, g_ref, h_ref):
    h_ref[...] = _rms(x_ref[...], g_ref[...]).astype(h_ref.dtype)


def _prenorm(x2, gains, tm):
    M, D = x2.shape
    return pl.pallas_call(
        _prenorm_kernel,
        grid=(M // tm,),
        in_specs=[pl.BlockSpec((tm, D), lambda i: (i, 0)), pl.BlockSpec((None, 1, D), lambda i: (0, 0, 0))],
        out_specs=pl.BlockSpec((tm, D), lambda i: (i, 0)),
        out_shape=jax.ShapeDtypeStruct((M, D), BF16),
        compiler_params=_cparams(("parallel",)),
        name="prenorm",
    )(x2, gains)


def _rope_kernel(pos_ref, freq_ref, cos_ref, sin_ref):
    ang = pos_ref[...] * freq_ref[...]
    lane = lax.broadcasted_iota(jnp.int32, ang.shape, 1)
    cos_ref[...] = jnp.cos(ang)
    sin_ref[...] = jnp.where(lane < HEAD_DIM // 2, -jnp.sin(ang), jnp.sin(ang))


def _rope_tables(positions, tm):
    M = positions.size
    pos = positions.astype(F32).reshape(M, 1)
    inv = (1.0 / (ROPE_THETA ** (np.arange(0, HEAD_DIM, 2, dtype=np.float32) / HEAD_DIM))).astype(np.float32)
    freq = jnp.asarray(np.concatenate([inv, inv])[None, :])
    return pl.pallas_call(
        _rope_kernel,
        grid=(M // tm,),
        in_specs=[pl.BlockSpec((tm, 1), lambda i: (i, 0)), pl.BlockSpec((1, HEAD_DIM), lambda i: (0, 0))],
        out_specs=[pl.BlockSpec((tm, HEAD_DIM), lambda i: (i, 0))] * 2,
        out_shape=[jax.ShapeDtypeStruct((M, HEAD_DIM), F32)] * 2,
        compiler_params=_cparams(("parallel",)),
        name="rope_tables",
    )(pos, freq)


def _mm_kernel(a_ref, wt_ref, wo_ref, wd_ref, o_ref, wo16_ref, wd16_ref):
    o_ref[...] = lax.dot_general(a_ref[...], wt_ref[...], (((1,), (1,)), ((), ())), preferred_element_type=F32)
    wo16_ref[...] = wo_ref[...].astype(wo16_ref.dtype)
    wd16_ref[...] = wd_ref[...].astype(wd16_ref.dtype)


def _cast_slices(rows, steps):
    return max(n for n in range(1, steps + 1) if rows % n == 0 and (rows // n) % (2 * SUBLANES) == 0)


def _in_proj(h, wt, w_out, w_down, l, tm, tn):
    M, K = h.shape
    N = wt.shape[1]
    nj, nt = N // tn, M // tm
    specs, shapes = [], []
    for w in (w_out, w_down):
        rows, cols = w.shape[1], w.shape[2]
        n = _cast_slices(rows, nj * nt)
        idx = lambda j, i, n=n: jnp.minimum(j * nt + i, n - 1)
        specs.append((pl.BlockSpec((None, rows // n, cols), lambda j, i, idx=idx: (l, idx(j, i), 0)),
                      pl.BlockSpec((rows // n, cols), lambda j, i, idx=idx: (idx(j, i), 0))))
        shapes.append(jax.ShapeDtypeStruct((rows, cols), BF16))
    return pl.pallas_call(
        _mm_kernel,
        grid=(nj, nt),
        in_specs=[pl.BlockSpec((tm, K), lambda j, i: (i, 0)), pl.BlockSpec((None, tn, K), lambda j, i: (l, j, 0)),
                  specs[0][0], specs[1][0]],
        out_specs=[pl.BlockSpec((tm, tn), lambda j, i: (i, j)), specs[0][1], specs[1][1]],
        out_shape=[jax.ShapeDtypeStruct((M, N), F32)] + shapes,
        compiler_params=_cparams(("arbitrary", "arbitrary")),
        name="in_proj",
    )(h, wt, w_out, w_down)


def _gate_kernel(ba_ref, alog_ref, dt_ref, beta_ref, gc_ref):
    p = ba_ref[...]
    beta_ref[...] = _sigmoid(p)
    sp = p + dt_ref[...]
    softplus = jnp.maximum(sp, 0.0) + jnp.log(1.0 + jnp.exp(-jnp.abs(sp)))
    g = -jnp.exp(alog_ref[...]) * softplus
    row = lax.broadcasted_iota(jnp.int32, g.shape, 0) % DN_CHUNK
    s = 1
    while s < DN_CHUNK:
        g = g + jnp.where(row >= s, pltpu.roll(g, s, axis=0), 0.0)
        s *= 2
    gc_ref[...] = g


def _gates(p, alog_rows, dt_rows, l, B, T):
    M = p.shape[0]
    cb = P_BA // LANES
    vec = pl.BlockSpec((None, 1, LANES), lambda b: (l, 0, 0))
    return pl.pallas_call(
        _gate_kernel,
        grid=(B,),
        in_specs=[pl.BlockSpec((T, LANES), lambda b: (b, cb)), vec, vec],
        out_specs=[pl.BlockSpec((T, LANES), lambda b: (b, 0))] * 2,
        out_shape=[jax.ShapeDtypeStruct((M, LANES), F32)] * 2,
        compiler_params=_cparams(("parallel",)),
        name="dn_gates",
    )(p, alog_rows, dt_rows)


def _dn_prep_kernel(q_ref, k_ref, v_ref, qh_ref, kh_ref, vh_ref, beta_ref, gc_ref, gcrow_ref,
                    cwq_ref, cwk_ref, cwv_ref, wq_ref, u_ref, kd_ref, a_ref, c_ref, *, tiles_per_seq):
    h = pl.program_id(1)
    first = (pl.program_id(2) % tiles_per_seq) == 0
    tt = q_ref.shape[0]
    C = DN_CHUNK
    G = tt // C
    D = HEAD_DIM
    pad = SUBLANES

    def conv_silu(x_ref, halo_ref, cw_ref):
        halo = jnp.where(first, 0.0, halo_ref[...])
        ext = jnp.concatenate([halo, x_ref[...]], axis=0)
        c0, c1, c2, c3 = (cw_ref[j:j + 1, :] for j in range(DN_CONV))
        back1 = pltpu.roll(ext, 1, axis=0)
        z = ext * c1 + back1 * c0
        acc = (ext * c3 + back1 * c2 + pltpu.roll(z, 2, axis=0))[pad:pad + tt]
        return acc * _sigmoid(acc)

    q = conv_silu(q_ref, qh_ref, cwq_ref)
    k = conv_silu(k_ref, kh_ref, cwk_ref)
    v = conv_silu(v_ref, vh_ref, cwv_ref)
    q = q * lax.rsqrt(jnp.sum(q * q, axis=-1, keepdims=True) + NORM_EPS) * (D ** -0.5)
    k = k * lax.rsqrt(jnp.sum(k * k, axis=-1, keepdims=True) + NORM_EPS)
    lane = lax.broadcasted_iota(jnp.int32, (tt, LANES), 1)
    beta = jnp.sum(jnp.where(lane == h, beta_ref[...], 0.0), axis=-1, keepdims=True)
    gcc = jnp.sum(jnp.where(lane == h + DN_HEADS, gc_ref[...], 0.0), axis=-1, keepdims=True)
    eg = jnp.exp(gcc)
    kb = k * beta
    vb = v * beta

    q3 = q.reshape(G, C, D)
    k3 = k.reshape(G, C, D)
    kb3 = kb.reshape(G, C, D)
    gcc3 = gcc.reshape(G, C, 1)
    gcr3 = gcrow_ref[...]
    ii = lax.broadcasted_iota(jnp.int32, (G, C, C), 1)
    jj = lax.broadcasted_iota(jnp.int32, (G, C, C), 2)
    decay = jnp.exp(jnp.where(ii >= jj, gcc3 - gcr3, -jnp.inf))
    eye = (ii == jj).astype(F32)
    kq = _bdot_nt(jnp.concatenate([kb3, q3], axis=1), k3)
    nil = jnp.where(ii > jj, -kq[:, 0:C] * decay, 0.0)
    inv = eye + nil
    powk = _bdot(nil, nil)
    for _ in range(int(math.log2(C)) - 2):
        both = _bdot(jnp.concatenate([inv, powk], axis=1), powk)
        inv = inv + both[:, 0:C]
        powk = both[:, C:2 * C]
    inv = inv + _bdot(inv, powk)
    rhs = jnp.concatenate([vb, kb * eg], axis=-1).reshape(G, C, 2 * D)
    uw = _bdot(inv, rhs)
    glast = gcr3[:, :, C - 1:C]
    u_ref[...] = uw[:, :, :D]
    wq_ref[:, 0:C, :] = uw[:, :, D:].astype(wq_ref.dtype)
    wq_ref[:, C:2 * C, :] = (q * eg).reshape(G, C, D).astype(wq_ref.dtype)
    kd_ref[...] = (k3 * jnp.exp(glast - gcc3)).astype(kd_ref.dtype)
    a_ref[...] = (kq[:, C:2 * C] * decay).astype(a_ref.dtype)
    c_ref[...] = jnp.broadcast_to(jnp.exp(glast), c_ref.shape)


def _dn_prep(p, beta, gc, gcrow, conv_w, l, B, T, tt):
    H = DN_HEADS
    C = DN_CHUNK
    N = T // C
    G = tt // C
    nt = T // tt
    hb = tt // SUBLANES
    qb, kb, vb = P_QKV // LANES, (P_QKV + DN_W) // LANES, (P_QKV + 2 * DN_W) // LANES
    tok = lambda off: pl.BlockSpec((tt, HEAD_DIM), lambda b, h, i: (b * nt + i, off + h))
    halo = lambda off: pl.BlockSpec((SUBLANES, HEAD_DIM),
                                    lambda b, h, i: (jnp.maximum((b * nt + i) * hb - 1, 0), off + h))
    gate = pl.BlockSpec((tt, LANES), lambda b, h, i: (b * nt + i, 0))
    cw = lambda off: pl.BlockSpec((None, DN_CONV, HEAD_DIM), lambda b, h, i: (l, 0, off + h))
    chunked = lambda r, c: pl.BlockSpec((None, None, G, r, c), lambda b, h, i: (b, h, i, 0, 0))
    return pl.pallas_call(
        functools.partial(_dn_prep_kernel, tiles_per_seq=nt),
        grid=(B, H, nt),
        in_specs=[tok(qb), tok(kb), tok(vb), halo(qb), halo(kb), halo(vb), gate, gate, chunked(1, C),
                  cw(0), cw(H), cw(2 * H)],
        out_specs=[chunked(2 * C, HEAD_DIM), chunked(C, HEAD_DIM), chunked(C, HEAD_DIM), chunked(C, C),
                   chunked(1, LANES)],
        out_shape=[jax.ShapeDtypeStruct((B, H, N, 2 * C, HEAD_DIM), BF16),
                   jax.ShapeDtypeStruct((B, H, N, C, HEAD_DIM), F32),
                   jax.ShapeDtypeStruct((B, H, N, C, HEAD_DIM), BF16),
                   jax.ShapeDtypeStruct((B, H, N, C, C), BF16),
                   jax.ShapeDtypeStruct((B, H, N, 1, LANES), F32)],
        compiler_params=_cparams(("parallel", "parallel", "parallel")),
        name="dn_prep",
    )(p, p, p, p, p, p, beta, gc, gcrow, conv_w, conv_w, conv_w)


def _dn_scan_kernel(wq_ref, u_ref, kd_ref, a_ref, c_ref, z_ref, nw_ref, y_ref, s_ref):
    B, H, G = wq_ref.shape[0], wq_ref.shape[1], wq_ref.shape[2]
    C = DN_CHUNK
    D = HEAD_DIM

    @pl.when(pl.program_id(0) == 0)
    def _():
        s_ref[...] = jnp.zeros_like(s_ref)

    nw = nw_ref[...]
    chains = [(b, h) for b in range(B) for h in range(H)]

    def body(n, carry):
        r0 = pl.multiple_of(n * C, C)
        S = [s_ref[b * H + h] for b, h in chains]
        m1 = [jnp.dot(wq_ref[b, h, n], S[i].astype(BF16), preferred_element_type=F32)
              for i, (b, h) in enumerate(chains)]
        v_new = [(u_ref[b, h, n] - m1[i][0:C]).astype(BF16) for i, (b, h) in enumerate(chains)]
        o = [m1[i][C:2 * C] + jnp.dot(a_ref[b, h, n], v_new[i], preferred_element_type=F32)
             for i, (b, h) in enumerate(chains)]
        kv = [lax.dot_general(kd_ref[b, h, n], v_new[i], (((0,), (0,)), ((), ())), preferred_element_type=F32)
              for i, (b, h) in enumerate(chains)]
        for i, (b, h) in enumerate(chains):
            s_ref[b * H + h] = S[i] * c_ref[b, h, n] + kv[i]
            zz = z_ref[b, pl.ds(r0, C), h * D:(h + 1) * D]
            y_ref[b, pl.ds(r0, C), h * D:(h + 1) * D] = (_rms(o[i], nw) * (zz * _sigmoid(zz))).astype(y_ref.dtype)
        return carry

    lax.fori_loop(0, G, body, 0)


def _dn_scan(wq, u, kd, a, c, p, norm_w, l, B, T, tt):
    H = DN_HEADS
    C = DN_CHUNK
    G = tt // C
    nt = T // tt
    chunked = lambda r, cc: pl.BlockSpec((B, H, G, r, cc), lambda i: (0, 0, i, 0, 0))
    y = pl.pallas_call(
        _dn_scan_kernel,
        grid=(nt,),
        in_specs=[chunked(2 * C, HEAD_DIM), chunked(C, HEAD_DIM), chunked(C, HEAD_DIM), chunked(C, C),
                  chunked(1, LANES),
                  pl.BlockSpec((B, tt, DN_W), lambda i: (0, i, P_Z // DN_W)),
                  pl.BlockSpec((None, 1, HEAD_DIM), lambda i: (l, 0, 0))],
        out_specs=pl.BlockSpec((B, tt, DN_W), lambda i: (0, i, 0)),
        out_shape=jax.ShapeDtypeStruct((B, T, DN_W), BF16),
        scratch_shapes=[pltpu.VMEM((B * H, HEAD_DIM, HEAD_DIM), F32)],
        compiler_params=_cparams(("arbitrary",)),
        name="dn_scan",
    )(wq, u, kd, a, c, p.reshape(B, T, p.shape[1]), norm_w)
    return y.reshape(B * T, DN_W)


DN_PREP_ROWS = 4096
DN_SCAN_ROWS = 512


def _dn_layer(p, conv_w, alog_rows, dt_rows, norm_w, l, B, T):
    N = T // DN_CHUNK
    beta, gc = _gates(p, alog_rows, dt_rows, l, B, T)
    gcrow = gc[:, DN_HEADS:2 * DN_HEADS].reshape(B, N, DN_CHUNK, DN_HEADS).transpose(0, 3, 1, 2)
    gcrow = gcrow.reshape(B, DN_HEADS, N, 1, DN_CHUNK)
    wq, u, kd, a, c = _dn_prep(p, beta, gc, gcrow, conv_w, l, B, T, min(DN_PREP_ROWS, T))
    return _dn_scan(wq, u, kd, a, c, p, norm_w, l, B, T, min(DN_SCAN_ROWS, T))


POOL_HALO = 16


def _pool_kernel(x_ref, halo_ref, w_ref, scale_ref, o_ref, *, tiles_per_seq):
    i = pl.program_id(0)
    tm = x_ref.shape[0]
    t0 = (i % tiles_per_seq) * tm
    first = (i % tiles_per_seq) == 0
    pos = t0 + 1 + lax.broadcasted_iota(jnp.int32, (tm, POOL_GROUP_DIM), 0)
    for gi, win in enumerate(POOL_WINDOWS):
        cols = slice(gi * POOL_GROUP_DIM, (gi + 1) * POOL_GROUP_DIM)
        halo = jnp.where(first, 0.0, halo_ref[:, cols])
        xg = x_ref[:, cols]
        ext = jnp.concatenate([halo, xg], axis=0)
        acc = ext
        s = 1
        while s < win:
            acc = acc + pltpu.roll(acc, s, axis=0)
            s *= 2
        cnt = jnp.minimum(pos, win).astype(F32)
        y = acc[POOL_HALO:] / cnt - xg
        y = jnp.dot(y.astype(BF16), w_ref[gi].astype(BF16), preferred_element_type=F32)
        o_ref[:, cols] = (y * scale_ref[:, cols]).astype(o_ref.dtype)


def _pool(p, pool_w, pool_scale, l, T, tm):
    M = p.shape[0]
    cb = P_POOL // POOL_W
    hb = tm // POOL_HALO
    return pl.pallas_call(
        functools.partial(_pool_kernel, tiles_per_seq=T // tm),
        grid=(M // tm,),
        in_specs=[pl.BlockSpec((tm, POOL_W), lambda i: (i, cb)),
                  pl.BlockSpec((POOL_HALO, POOL_W), lambda i: (jnp.maximum(i * hb - 1, 0), cb)),
                  pl.BlockSpec((None, POOL_GROUPS, POOL_GROUP_DIM, POOL_GROUP_DIM), lambda i: (l, 0, 0, 0)),
                  pl.BlockSpec((None, 1, POOL_W), lambda i: (l, 0, 0))],
        out_specs=pl.BlockSpec((tm, POOL_W), lambda i: (i, 0)),
        out_shape=jax.ShapeDtypeStruct((M, POOL_W), BF16),
        compiler_params=_cparams(("parallel",)),
        name="pool",
    )(p, p, pool_w, pool_scale)


def _swa_kernel(q_ref, k_ref, kp_ref, v_ref, vp_ref, cos_ref, sin_ref, cosp_ref, sinp_ref, sink_ref, o_ref,
                *, tiles_per_seq):
    i = pl.program_id(1)
    first = (i % tiles_per_seq) == 0
    tm = q_ref.shape[0]
    blk = SWA_BLOCK
    G = SWA_HEADS // SWA_KV_HEADS
    D = HEAD_DIM

    def rope(x, cos, sin):
        return x * cos + pltpu.roll(x, D // 2, axis=1) * sin

    cos = cos_ref[...]
    sin = sin_ref[...]
    cosx = jnp.concatenate([cosp_ref[...], cos], axis=0)
    sinx = jnp.concatenate([sinp_ref[...], sin], axis=0)
    kx = rope(jnp.concatenate([kp_ref[...], k_ref[...]], axis=0), cosx, sinx).astype(BF16)
    vx = jnp.concatenate([vp_ref[...], v_ref[...]], axis=0).astype(BF16)
    vx = jnp.concatenate([vx, jnp.ones_like(vx)], axis=1)
    scale = D ** -0.5
    qh = [(rope(q_ref[:, g * D:(g + 1) * D], cos, sin) * scale).astype(BF16) for g in range(G)]

    ri = lax.broadcasted_iota(jnp.int32, (G * blk, 2 * blk), 0)
    ii = ri % blk
    jj = lax.broadcasted_iota(jnp.int32, (G * blk, 2 * blk), 1)
    band = jnp.logical_or(jnp.logical_and(jj < blk, jj > ii), jnp.logical_and(jj >= blk, jj - blk <= ii))
    no_prev = jnp.where(first, blk, 0)
    band_first = jnp.logical_and(band, jj >= no_prev)
    sink = sink_ref[:, 0:1]
    for g in range(1, G):
        sink = jnp.where(ri[:, 0:1] >= g * blk, sink_ref[:, g:g + 1], sink)
    for b in range(tm // blk):
        qs = jnp.concatenate([q[b * blk:(b + 1) * blk] for q in qh], axis=0)
        keys = kx[b * blk:(b + 2) * blk]
        s = lax.dot_general(qs, keys, (((1,), (1,)), ((), ())), preferred_element_type=F32)
        s = jnp.where(band_first if b == 0 else band, s, -jnp.inf)
        m = jnp.maximum(jnp.max(s, axis=-1, keepdims=True), sink)
        e = jnp.exp(s - m).astype(BF16)
        pv = jnp.dot(e, vx[b * blk:(b + 2) * blk], preferred_element_type=F32)
        o = pv[:, 0:D] / (pv[:, D:2 * D] + jnp.exp(sink - m))
        for g in range(G):
            o_ref[b * blk:(b + 1) * blk, g * D:(g + 1) * D] = o[g * blk:(g + 1) * blk].astype(o_ref.dtype)


def _swa(p, cos, sin, sink_rows, l, T, tm):
    M = p.shape[0]
    blk = SWA_BLOCK
    r = tm // blk
    qb, kb, vb = P_SQ // SWA_GROUP_W, P_SK // HEAD_DIM, P_SV // HEAD_DIM
    prev = lambda i: jnp.maximum(i * r - 1, 0)
    return pl.pallas_call(
        functools.partial(_swa_kernel, tiles_per_seq=T // tm),
        grid=(SWA_KV_HEADS, M // tm),
        in_specs=[pl.BlockSpec((tm, SWA_GROUP_W), lambda kv, i: (i, qb + kv)),
                  pl.BlockSpec((tm, HEAD_DIM), lambda kv, i: (i, kb + kv)),
                  pl.BlockSpec((blk, HEAD_DIM), lambda kv, i: (prev(i), kb + kv)),
                  pl.BlockSpec((tm, HEAD_DIM), lambda kv, i: (i, vb + kv)),
                  pl.BlockSpec((blk, HEAD_DIM), lambda kv, i: (prev(i), vb + kv)),
                  pl.BlockSpec((tm, HEAD_DIM), lambda kv, i: (i, 0)),
                  pl.BlockSpec((tm, HEAD_DIM), lambda kv, i: (i, 0)),
                  pl.BlockSpec((blk, HEAD_DIM), lambda kv, i: (prev(i), 0)),
                  pl.BlockSpec((blk, HEAD_DIM), lambda kv, i: (prev(i), 0)),
                  pl.BlockSpec((None, None, 1, LANES), lambda kv, i: (l, kv, 0, 0))],
        out_specs=pl.BlockSpec((tm, SWA_GROUP_W), lambda kv, i: (i, kv)),
        out_shape=jax.ShapeDtypeStruct((M, SWA_W), BF16),
        compiler_params=_cparams(("parallel", "parallel")),
        name="swa",
    )(p, p, p, p, p, cos, sin, cos, sin, sink_rows)


OUT_ROWS = 128


def _outproj_kernel(ydn_ref, ypool_ref, yswa_ref, w_ref, x_ref, gpost_ref, gffn_ref, xo_ref, h_ref):
    for r0 in range(0, x_ref.shape[0], OUT_ROWS):
        rows = slice(r0, r0 + OUT_ROWS)
        y = jnp.concatenate([ydn_ref[rows, :], ypool_ref[rows, :], yswa_ref[rows, :]], axis=1)
        mix = jnp.dot(y, w_ref[...], preferred_element_type=F32)
        xn = x_ref[rows, :] + _rms(mix, gpost_ref[...])
        xo_ref[rows, :] = xn
        h_ref[rows, :] = _rms(xn, gffn_ref[...]).astype(h_ref.dtype)


def _out_proj(ydn, ypool, yswa, w_out, x2, gpost, gffn, l, tm):
    M, D = x2.shape
    row = lambda w: pl.BlockSpec((tm, w), lambda i: (i, 0))
    vec = pl.BlockSpec((None, 1, D), lambda i: (l, 0, 0))
    return pl.pallas_call(
        _outproj_kernel,
        grid=(M // tm,),
        in_specs=[row(DN_W), row(POOL_W), row(SWA_W),
                  pl.BlockSpec(w_out.shape, lambda i: (0, 0)), row(D), vec, vec],
        out_specs=[row(D), row(D)],
        out_shape=[jax.ShapeDtypeStruct((M, D), F32), jax.ShapeDtypeStruct((M, D), BF16)],
        compiler_params=_cparams(("parallel",)),
        name="out_proj",
    )(ydn, ypool, yswa, w_out, x2, gpost, gffn)


FFN_TILE = 1024
FFN_ROWS = 512


def _ffn_chunks(tm):
    return (FFN_ROWS,) * (tm // FFN_ROWS) if tm % FFN_ROWS == 0 else (tm,)


def _ffn_up_kernel(h_ref, wa_ref, wb_ref, cwa_ref, cwb_ref, cba_ref, cbb_ref, g_ref, wa16, wb16, ua, ub,
                   *, tiles_per_seq):
    i = pl.program_id(1)
    tm = h_ref.shape[0]
    pad = SUBLANES

    @pl.when(i == 0)
    def _():
        wa16[...] = wa_ref[...].astype(BF16)
        wb16[...] = wb_ref[...].astype(BF16)

    @pl.when(i % tiles_per_seq == 0)
    def _():
        ua[0:pad, :] = jnp.zeros((pad, ua.shape[1]), F32)
        ub[0:pad, :] = jnp.zeros((pad, ub.shape[1]), F32)

    hb = h_ref[...]
    ua[pad:pad + tm, :] = jnp.dot(hb, wa16[...], preferred_element_type=F32)
    ub[pad:pad + tm, :] = jnp.dot(hb, wb16[...], preferred_element_type=F32)

    def conv(scr, cw_ref, cb_ref, r0, rows):
        acc = cb_ref[...] + scr[pad + r0:pad + r0 + rows, :] * cw_ref[FFN_CONV - 1:FFN_CONV, :]
        for s in range(1, FFN_CONV):
            acc = acc + scr[pad + r0 - s:pad + r0 - s + rows, :] * cw_ref[FFN_CONV - 1 - s:FFN_CONV - s, :]
        return acc

    r0 = 0
    for rows in _ffn_chunks(tm):
        a = conv(ua, cwa_ref, cba_ref, r0, rows)
        b = conv(ub, cwb_ref, cbb_ref, r0, rows)
        g_ref[r0:r0 + rows, :] = (a * _sigmoid(a) * b).astype(g_ref.dtype)
        r0 += rows
    ua[0:pad, :] = ua[tm:tm + pad, :]
    ub[0:pad, :] = ub[tm:tm + pad, :]


def _ffn_up(h, w_up, conv_w, conv_b, l, T, tm, tn):
    M, D = h.shape
    F = w_up.shape[2] // 2
    nj = F // tn
    return pl.pallas_call(
        functools.partial(_ffn_up_kernel, tiles_per_seq=T // tm),
        grid=(nj, M // tm),
        in_specs=[pl.BlockSpec((tm, D), lambda j, i: (i, 0)),
                  pl.BlockSpec((None, D, tn), lambda j, i: (l, 0, j)),
                  pl.BlockSpec((None, D, tn), lambda j, i: (l, 0, nj + j)),
                  pl.BlockSpec((None, FFN_CONV, tn), lambda j, i: (l, 0, j)),
                  pl.BlockSpec((None, FFN_CONV, tn), lambda j, i: (l, 0, nj + j)),
                  pl.BlockSpec((None, 1, tn), lambda j, i: (l, 0, j)),
                  pl.BlockSpec((None, 1, tn), lambda j, i: (l, 0, nj + j))],
        out_specs=pl.BlockSpec((tm, tn), lambda j, i: (i, j)),
        out_shape=jax.ShapeDtypeStruct((M, F), BF16),
        scratch_shapes=[pltpu.VMEM((D, tn), BF16)] * 2 + [pltpu.VMEM((tm + SUBLANES, tn), F32)] * 2,
        compiler_params=_cparams(("parallel", "arbitrary")),
        name="ffn_up",
    )(h, w_up, w_up, conv_w, conv_w, conv_b, conv_b)


DOWN_ROWS = 128


def _ffn_down_kernel(g_ref, w_ref, x_ref, gpost_ref, gnext_ref, xo_ref, h_ref=None):
    for r0 in range(0, x_ref.shape[0], DOWN_ROWS):
        rows = slice(r0, r0 + DOWN_ROWS)
        f = jnp.dot(g_ref[rows, :], w_ref[...], preferred_element_type=F32)
        xn = x_ref[rows, :] + _rms(f, gpost_ref[...])
        xo_ref[rows, :] = xn
        if h_ref is not None:
            h_ref[rows, :] = _rms(xn, gnext_ref[...]).astype(h_ref.dtype)


def _ffn_down(g, w_down, x2, gpost, gnext, l, lnext, tm):
    M, D = x2.shape
    F = g.shape[1]
    n_out = 1 if lnext is None else 2
    out = pl.pallas_call(
        _ffn_down_kernel,
        grid=(M // tm,),
        in_specs=[pl.BlockSpec((tm, F), lambda i: (i, 0)),
                  pl.BlockSpec((F, D), lambda i: (0, 0), pipeline_mode=pl.Buffered(1)),
                  pl.BlockSpec((tm, D), lambda i: (i, 0)),
                  pl.BlockSpec((None, 1, D), lambda i: (l, 0, 0)),
                  pl.BlockSpec((None, 1, D), lambda i: (l if lnext is None else lnext, 0, 0))],
        out_specs=[pl.BlockSpec((tm, D), lambda i: (i, 0))] * n_out,
        out_shape=[jax.ShapeDtypeStruct((M, D), F32), jax.ShapeDtypeStruct((M, D), BF16)][:n_out],
        compiler_params=_cparams(("parallel",)),
        name="ffn_down",
    )(g, w_down, x2, gpost, gnext)
    return (out[0], None) if lnext is None else tuple(out)


def _rows(v):
    return v.astype(F32).reshape(v.shape[0], 1, v.shape[1])


def kernel(x, positions, norm_mix_pre, w_in, dn_conv_w, dn_a_log, dn_dt_bias, dn_norm_w, pool_w, pool_scale,
           swa_sinks, w_out, norm_mix_post, norm_ffn_pre, ffn_w_up, ffn_conv_w, ffn_conv_b, ffn_w_down,
           norm_ffn_post):
    B, T, D = x.shape
    depth = w_in.shape[0]
    M = B * T
    tm = min(ROW_TILE, T)
    G = SWA_HEADS // SWA_KV_HEADS
    x2 = x.reshape(M, D).astype(F32)

    w_in16 = _prep_w_in(w_in)
    gate_pad = ((0, 0), (DN_HEADS, LANES - 2 * DN_HEADS))
    alog_rows = _rows(jnp.pad(dn_a_log, gate_pad))
    dt_rows = _rows(jnp.pad(dn_dt_bias, gate_pad))
    sink_rows = jnp.pad(swa_sinks.astype(F32).reshape(depth, SWA_KV_HEADS, 1, G),
                        ((0, 0), (0, 0), (0, 0), (0, LANES - G)))
    g_mix_pre, g_mix_post = _rows(norm_mix_pre), _rows(norm_mix_post)
    g_ffn_pre, g_ffn_post = _rows(norm_ffn_pre), _rows(norm_ffn_post)
    dn_nw, pscale, conv_b = _rows(dn_norm_w), _rows(pool_scale), _rows(ffn_conv_b)

    cos, sin = _rope_tables(positions, tm)
    h = _prenorm(x2, g_mix_pre, tm)
    for l in range(depth):
        p, w_out16, w_down16 = _in_proj(h, w_in16, w_out, ffn_w_down, l, min(IN_PROJ_ROWS, T),
                                        P_W // IN_PROJ_PANELS)
        y_dn = _dn_layer(p, dn_conv_w, alog_rows, dt_rows, dn_nw, l, B, T)
        y_pool = _pool(p, pool_w, pscale, l, T, min(MIXER_ROWS, T))
        y_swa = _swa(p, cos, sin, sink_rows, l, T, min(MIXER_ROWS, T))
        x2, h = _out_proj(y_dn, y_pool, y_swa, w_out16, x2, g_mix_post, g_ffn_pre, l, tm)
        g = _ffn_up(h, ffn_w_up, ffn_conv_w, conv_b, l, T, min(FFN_TILE, T), FFN_PANEL)
        x2, h = _ffn_down(g, w_down16, x2, g_ffn_post, g_mix_pre, l, l + 1 if l + 1 < depth else None,
                          min(FFN_DOWN_ROWS, T))
    return x2.reshape(B, T, D).astype(x.dtype)
```
